```python
import math
import jax
import jax.numpy as jnp
from jax import lax
import numpy as np

D_MODEL = 1024
BATCH = 4
SEQ = 8192
DEPTH = 4

CTX_LEN = 256
GRID_W = 64
D_MIX = D_MODEL
D_HYENA = D_MIX // 2
S5_WIDTH = D_MIX - D_HYENA
S5_GROUP = 16
S5_GROUPS = S5_WIDTH // S5_GROUP
S5_STATE = 64
D_PROJ = 3 * D_HYENA + S5_WIDTH
N_DIR = 2
FILTER_EMB = 33
FILTER_HIDDEN = 64
FILTER_INNER = 2
DECAY_TARGET = 1e-2
FAST_DECAY_PCT = 0.3
SLOW_DECAY_PCT = 1.5
D_FF = 2816
RMS_EPS = 1e-6
N_MOD = 6

kernel_name = 'hybrid_hyena_s5_prefix_block'


def rms_norm(x, g):
    x32 = x.astype(jnp.float32)
    y = x32 * lax.rsqrt(jnp.mean(x32 * x32, axis=-1, keepdims=True) + RMS_EPS)
    return (y * g.astype(jnp.float32)).astype(x.dtype)


def modulated_norm(x, g, shift, scale):
    return rms_norm(x, g) * (1 + scale) + shift


def dwconv3_seq(u, w, b):
    up = jnp.pad(u, ((0, 0), (1, 1), (0, 0)))
    return up[:, :-2] * w[0] + up[:, 1:-1] * w[1] + up[:, 2:] * w[2] + b


def hyena_filter(L, w_in, b_in, w_hid, b_hid, freq, w_out):
    f32 = jnp.float32
    t = jnp.linspace(0.0, 1.0, L, dtype=f32)[:, None]
    bands = (FILTER_EMB - 1) // 2
    w = (2.0 * math.pi / L) * jnp.arange(L, dtype=f32)[:, None]
    f = jnp.linspace(1e-4, bands - 1, bands, dtype=f32)[None, :]
    z = jnp.concatenate([t, jnp.cos(f * w), -jnp.sin(f * w)], axis=-1)
    fr = freq.astype(f32)
    h = jnp.sin(fr * (z @ w_in.astype(f32) + b_in.astype(f32)))
    for i in range(FILTER_INNER):
        h = jnp.sin(fr * (h @ w_hid[i].astype(f32) + b_hid[i].astype(f32)))
    h = (h @ w_out.astype(f32)).reshape(L, N_DIR, D_HYENA)
    deltas = jnp.abs(jnp.linspace(math.log(DECAY_TARGET) / FAST_DECAY_PCT,
                                  math.log(DECAY_TARGET) / SLOW_DECAY_PCT, D_HYENA, dtype=f32))
    h = h * jnp.exp(-t * deltas)[:, None, :]
    return jnp.concatenate([h[:, 0], jnp.zeros((1, D_HYENA), f32), h[:0:-1, 1]], axis=0)


def fft_long_conv(u, k, bias):
    L = u.shape[1]
    n = 2 * L
    u_f = jnp.fft.rfft(u, n=n, axis=1)
    k_f = jnp.fft.rfft(k, n=n, axis=0)
    y = jnp.fft.irfft(u_f * k_f[None], n=n, axis=1)[:, :L]
    return y + u * bias


def hyena_mixer(p, short_w, short_b, f_w_in, f_b_in, f_w_hid, f_b_hid, f_freq, f_w_out, bias):
    L = p.shape[1]
    p = dwconv3_seq(p, short_w, short_b)
    x0, x1, v = jnp.split(p, 3, axis=-1)
    k = hyena_filter(L, f_w_in, f_b_in, f_w_hid, f_b_hid, f_freq, f_w_out)
    y = fft_long_conv((x1 * v).astype(jnp.float32), k, bias.astype(jnp.float32))
    return x0 * y.astype(p.dtype)


def s5_discretize(lam_re, lam_im, log_step, b_re, b_im):
    f32 = jnp.float32
    lr, li = lam_re.astype(f32), lam_im.astype(f32)
    dt = jnp.exp(log_step.astype(f32))[:, None]
    mag = jnp.exp(lr * dt)
    a_re, a_im = mag * jnp.cos(li * dt), mag * jnp.sin(li * dt)
    den = lr * lr + li * li
    q_re = ((a_re - 1.0) * lr + a_im * li) / den
    q_im = (a_im * lr - (a_re - 1.0) * li) / den
    br, bi = b_re.astype(f32), b_im.astype(f32)
    bb_re = q_re[..., None] * br - q_im[..., None] * bi
    bb_im = q_re[..., None] * bi + q_im[..., None] * br
    return a_re, a_im, bb_re, bb_im


def complex_affine_combine(e1, e2):
    a1r, a1i, b1r, b1i = e1
    a2r, a2i, b2r, b2i = e2
    return (a2r * a1r - a2i * a1i, a2r * a1i + a2i * a1r,
            a2r * b1r - a2i * b1i + b2r, a2r * b1i + a2i * b1r + b2i)


def s5_scan(ug, a_re, a_im, bb_re, bb_im, s0, reverse):
    L = ug.shape[1]
    bu_re = jnp.einsum('blgh,gph->blgp', ug, bb_re)
    bu_im = jnp.einsum('blgh,gph->blgp', ug, bb_im)
    if s0 is not None:
        s0_re, s0_im = s0
        edge = L - 1 if reverse else 0
        bu_re = bu_re.at[:, edge].add(a_re * s0_re - a_im * s0_im)
        bu_im = bu_im.at[:, edge].add(a_re * s0_im + a_im * s0_re)
    shape = (1, L) + a_re.shape
    elems = (jnp.broadcast_to(a_re, shape), jnp.broadcast_to(a_im, shape), bu_re, bu_im)
    _, _, s_re, s_im = lax.associative_scan(complex_affine_combine, elems, reverse=reverse, axis=1)
    return s_re, s_im


def s5_mixer(u, lam_re, lam_im, log_step, b_re, b_im, c_re, c_im, d, w_glu, b_glu, init, with_output):
    bsz, L, _ = u.shape
    f32 = jnp.float32
    ug = u.astype(f32).reshape(bsz, L, S5_GROUPS, S5_GROUP)
    y = None
    finals = []
    for dr in range(N_DIR):
        rev = dr == 1
        a_re, a_im, bb_re, bb_im = s5_discretize(lam_re[dr], lam_im[dr], log_step[dr], b_re[dr], b_im[dr])
        s0 = None if init is None else init[dr]
        s_re, s_im = s5_scan(ug, a_re, a_im, bb_re, bb_im, s0, rev)
        edge = 0 if rev else L - 1
        finals.append((s_re[:, edge], s_im[:, edge]))
        if with_output:
            y_dir = (jnp.einsum('blgp,ghp->blgh', s_re, c_re[dr].astype(f32))
                     - jnp.einsum('blgp,ghp->blgh', s_im, c_im[dr].astype(f32)))
            y = y_dir if y is None else y + y_dir
    if not with_output:
        return None, finals
    y = y + ug * d.astype(f32).reshape(S5_GROUPS, S5_GROUP)
    y = jax.nn.gelu(y.reshape(bsz, L, S5_WIDTH)).astype(u.dtype)
    return y * jax.nn.sigmoid(y @ w_glu + b_glu), finals


def conv_glu_ffn(h, rows, w_up, conv_w, conv_b, w_down):
    bsz, L, _ = h.shape
    up = h @ w_up
    g, v = up[..., :D_FF], up[..., D_FF:]
    g = lax.conv_general_dilated(g.reshape(bsz, rows, L // rows, D_FF),
                                 conv_w[:, :, None, :].astype(g.dtype), (1, 1), 'SAME',
                                 dimension_numbers=('NHWC', 'HWIO', 'NHWC'),
                                 feature_group_count=D_FF)
    g = g.reshape(bsz, L, D_FF) + conv_b
    return (jax.nn.gelu(g) * v) @ w_down


def setup_inputs(seed: int = 0) -> dict:
    key = jax.random.key(seed)
    ks = iter(jax.random.split(key, 40))
    f32 = jnp.float32

    def nrm(shape, s):
        return s * jax.random.normal(next(ks), shape, f32)

    P, G, H = S5_STATE, S5_GROUPS, S5_GROUP
    return {
        'x': nrm((BATCH, SEQ, D_MODEL), 1.0),
        'c': nrm((BATCH, D_MODEL), 1.0),
        'ctx': nrm((BATCH, CTX_LEN, D_MODEL), 1.0),
        'c_ctx': nrm((D_MODEL,), 1.0),
        'w_ada': nrm((DEPTH, D_MODEL, N_MOD * D_MODEL), D_MODEL ** -0.5),
        'b_ada': nrm((DEPTH, N_MOD * D_MODEL), 0.01),
        'norm_g': 1.0 + nrm((DEPTH, 4, D_MODEL), 0.05),
        'w_in': nrm((DEPTH, D_MODEL, D_PROJ), D_MODEL ** -0.5),
        'hy_short_w': nrm((DEPTH, 3, 3 * D_HYENA), 3 ** -0.5),
        'hy_short_b': nrm((DEPTH, 3 * D_HYENA), 0.01),
        'filt_w_in': nrm((DEPTH, FILTER_EMB, FILTER_HIDDEN), FILTER_EMB ** -0.5),
        'filt_b_in': nrm((DEPTH, FILTER_HIDDEN), 0.01),
        'filt_w_hid': nrm((DEPTH, FILTER_INNER, FILTER_HIDDEN, FILTER_HIDDEN), FILTER_HIDDEN ** -0.5),
        'filt_b_hid': nrm((DEPTH, FILTER_INNER, FILTER_HIDDEN), 0.01),
        'filt_freq': 1.0 + nrm((DEPTH, FILTER_HIDDEN), 0.05),
        'filt_w_out': nrm((DEPTH, FILTER_HIDDEN, N_DIR * D_HYENA), FILTER_HIDDEN ** -0.5),
        'hy_bias': nrm((DEPTH, D_HYENA), 1.0),
        's5_lam_re': -0.5 + nrm((DEPTH, N_DIR, G, P), 0.01),
        's5_lam_im': math.pi * jnp.arange(P, dtype=f32) + nrm((DEPTH, N_DIR, G, P), 0.01),
        's5_log_step': jax.random.uniform(next(ks), (DEPTH, N_DIR, G), f32, math.log(1e-3), math.log(1e-1)),
        's5_b_re': nrm((DEPTH, N_DIR, G, P, H), (2 * H) ** -0.5),
        's5_b_im': nrm((DEPTH, N_DIR, G, P, H), (2 * H) ** -0.5),
        's5_c_re': nrm((DEPTH, N_DIR, G, H, P), 0.5),
        's5_c_im': nrm((DEPTH, N_DIR, G, H, P), 0.5),
        's5_d': nrm((DEPTH, S5_WIDTH), 1.0),
        's5_w_glu': nrm((DEPTH, S5_WIDTH, S5_WIDTH), S5_WIDTH ** -0.5),
        's5_b_glu': nrm((DEPTH, S5_WIDTH), 0.01),
        'w_out': nrm((DEPTH, D_MIX, D_MODEL), D_MIX ** -0.5),
        'ffn_w_up': nrm((DEPTH, D_MODEL, 2 * D_FF), D_MODEL ** -0.5),
        'ffn_conv_w': nrm((DEPTH, 3, 3, D_FF), 1.0 / 3.0),
        'ffn_conv_b': nrm((DEPTH, D_FF), 0.01),
        'ffn_w_down': nrm((DEPTH, D_FF, D_MODEL), D_FF ** -0.5),
    }


def reference(x, c, ctx, c_ctx, w_ada, b_ada, norm_g, w_in, hy_short_w, hy_short_b,
              filt_w_in, filt_b_in, filt_w_hid, filt_b_hid, filt_freq, filt_w_out, hy_bias,
              s5_lam_re, s5_lam_im, s5_log_step, s5_b_re, s5_b_im, s5_c_re, s5_c_im, s5_d,
              s5_w_glu, s5_b_glu, w_out, ffn_w_up, ffn_conv_w, ffn_conv_b, ffn_w_down):
    seq_len = x.shape[1]
    rows = seq_len // GRID_W
    cond_x = jax.nn.silu(c)
    cond_c = jax.nn.silu(c_ctx)
    h_split = 3 * D_HYENA
    for l in range(DEPTH):
        last = l == DEPTH - 1
        mod_x = jnp.split((cond_x @ w_ada[l] + b_ada[l])[:, None, :], N_MOD, axis=-1)
        mod_c = jnp.split(cond_c @ w_ada[l] + b_ada[l], N_MOD, axis=-1)
        hyena_args = (hy_short_w[l], hy_short_b[l], filt_w_in[l], filt_b_in[l], filt_w_hid[l],
                      filt_b_hid[l], filt_freq[l], filt_w_out[l], hy_bias[l])
        s5_args = (s5_lam_re[l], s5_lam_im[l], s5_log_step[l], s5_b_re[l], s5_b_im[l],
                   s5_c_re[l], s5_c_im[l], s5_d[l], s5_w_glu[l], s5_b_glu[l])
        ffn_args = (ffn_w_up[l], ffn_conv_w[l], ffn_conv_b[l], ffn_w_down[l])

        pc = modulated_norm(ctx, norm_g[l, 0], mod_c[0], mod_c[1]) @ w_in[l]
        yc_s5, ctx_state = s5_mixer(pc[..., h_split:], *s5_args, None, not last)

        px = modulated_norm(x, norm_g[l, 0], mod_x[0], mod_x[1]) @ w_in[l]
        yx_s5, _ = s5_mixer(px[..., h_split:], *s5_args, ctx_state, True)
        yx_hy = hyena_mixer(px[..., :h_split], *hyena_args)
        yx = jnp.concatenate([yx_hy, yx_s5], axis=-1) @ w_out[l]
        x = x + mod_x[2] * rms_norm(yx, norm_g[l, 1])
        hx = modulated_norm(x, norm_g[l, 2], mod_x[3], mod_x[4])
        x = x + mod_x[5] * rms_norm(conv_glu_ffn(hx, rows, *ffn_args), norm_g[l, 3])

        if not last:
            yc_hy = hyena_mixer(pc[..., :h_split], *hyena_args)
            yc = jnp.concatenate([yc_hy, yc_s5], axis=-1) @ w_out[l]
            ctx = ctx + mod_c[2] * rms_norm(yc, norm_g[l, 1])
            hc = modulated_norm(ctx, norm_g[l, 2], mod_c[3], mod_c[4])
            ctx = ctx + mod_c[5] * rms_norm(conv_glu_ffn(hc, 1, *ffn_args), norm_g[l, 3])
    return x
```

```python
import functools
import math

import numpy as np
import jax
import jax.numpy as jnp
from jax import lax
from jax.experimental import pallas as pl
from jax.experimental.pallas import tpu as pltpu

GRID_W = 64
RMS_EPS = 1e-6
DECAY_TARGET = 1e-2
FAST_DECAY_PCT = 0.3
SLOW_DECAY_PCT = 1.5
S5_CHUNK = 16
LANES = 128
HALO = 16
ROW_TILE = 512
VMEM_LIMIT = 56 * 1024 * 1024

F32 = jnp.float32
BF16 = jnp.bfloat16
HIGHEST = lax.Precision.HIGHEST


def _cparams(*sem):
    return pltpu.CompilerParams(dimension_semantics=sem, vmem_limit_bytes=VMEM_LIMIT)


def _dot(a, b, **kw):
    return jnp.dot(a, b, preferred_element_type=F32, **kw)


def _ada_kernel(cond_ref, w_ref, b_ref, o_ref):
    cv = cond_ref[...]
    s = cv * jax.nn.sigmoid(cv)
    o_ref[0] = _dot(s, w_ref[0], precision=HIGHEST) + b_ref[0]


def _ada_mod(cond, w_ada, b_ada):
    depth, d, n = w_ada.shape
    tn = n // 4
    return pl.pallas_call(
        _ada_kernel,
        grid=(depth, n // tn),
        in_specs=[pl.BlockSpec((8, d), lambda l, j: (0, 0)),
                  pl.BlockSpec((1, d, tn), lambda l, j: (l, 0, j)),
                  pl.BlockSpec((1, 1, tn), lambda l, j: (l, 0, j))],
        out_specs=pl.BlockSpec((1, 8, tn), lambda l, j: (l, 0, j)),
        out_shape=jax.ShapeDtypeStruct((depth, 8, n), F32),
        compiler_params=_cparams("parallel", "parallel"),
        name="ada_mod",
    )(cond, w_ada, b_ada.reshape(depth, 1, n))


def _rms(v, g):
    ms = jnp.mean(v * v, axis=-1, keepdims=True)
    return v * lax.rsqrt(ms + RMS_EPS) * g


def _inproj_kernel(xm_ref, xp_ref, xn_ref, mod_ref, g_ref, w_ref, sw_ref, sb_ref,
                   x0_ref, z_ref, u_ref, p_scr, *, tm, d, dh):
    i = pl.program_id(1)
    nt = pl.num_programs(1)
    shift = mod_ref[0, :, 0:d]
    scale = mod_ref[0, :, d:2 * d]
    xa = jnp.concatenate([xp_ref[0], xm_ref[0], xn_ref[0]], axis=0)
    xn = (_rms(xa, g_ref[...]) * (1.0 + scale) + shift).astype(BF16)
    p = _dot(xn, w_ref[...])
    p_scr[...] = p[:, :3 * dh]

    @pl.when(i == 0)
    def _():
        p_scr[HALO - 8:HALO, :] = jnp.zeros((8, 3 * dh), F32)

    @pl.when(i == nt - 1)
    def _():
        p_scr[HALO + tm:HALO + tm + 8, :] = jnp.zeros((8, 3 * dh), F32)

    sw = sw_ref[...]
    conv = (p_scr[pl.ds(HALO - 1, tm), :] * sw[0:1] + p_scr[pl.ds(HALO, tm), :] * sw[1:2]
            + p_scr[pl.ds(HALO + 1, tm), :] * sw[2:3] + sb_ref[...])
    x0_ref[0] = conv[:, :dh].astype(BF16)
    z_ref[0] = (conv[:, dh:2 * dh] * conv[:, 2 * dh:]).astype(BF16)
    u_ref[0] = p[HALO:HALO + tm, 3 * dh:].astype(BF16)


def _inproj(x, mod_l, mod_row, g, w_in, sw, sb, dh, tm):
    b, s, d = x.shape
    dp = w_in.shape[1]
    ds5 = dp - 3 * dh
    nt = s // tm
    r = tm // HALO
    nh = s // HALO
    row = (lambda bi: bi) if mod_row is None else (lambda bi: mod_row)
    kern = functools.partial(_inproj_kernel, tm=tm, d=d, dh=dh)
    return pl.pallas_call(
        kern,
        grid=(b, nt),
        in_specs=[pl.BlockSpec((1, tm, d), lambda bi, i: (bi, i, 0)),
                  pl.BlockSpec((1, HALO, d), lambda bi, i: (bi, jnp.maximum(i * r - 1, 0), 0)),
                  pl.BlockSpec((1, HALO, d), lambda bi, i: (bi, jnp.minimum((i + 1) * r, nh - 1), 0)),
                  pl.BlockSpec((1, 1, mod_l.shape[-1]), lambda bi, i: (row(bi), 0, 0)),
                  pl.BlockSpec((1, d), lambda bi, i: (0, 0)),
                  pl.BlockSpec((d, dp), lambda bi, i: (0, 0)),
                  pl.BlockSpec((3, 3 * dh), lambda bi, i: (0, 0)),
                  pl.BlockSpec((1, 3 * dh), lambda bi, i: (0, 0))],
        out_specs=[pl.BlockSpec((1, tm, dh), lambda bi, i: (bi, i, 0)),
                   pl.BlockSpec((1, tm, dh), lambda bi, i: (bi, i, 0)),
                   pl.BlockSpec((1, tm, ds5), lambda bi, i: (bi, i, 0))],
        out_shape=[jax.ShapeDtypeStruct((b, s, dh), BF16),
                   jax.ShapeDtypeStruct((b, s, dh), BF16),
                   jax.ShapeDtypeStruct((b, s, ds5), BF16)],
        scratch_shapes=[pltpu.VMEM((tm + 2 * HALO, 3 * dh), F32)],
        compiler_params=_cparams("parallel", "arbitrary"),
        name="inproj",
    )(x, x, x, mod_l, g, w_in, sw, sb)


def _filter_feats(L, emb):
    bands = (emb - 1) // 2
    t = np.linspace(0.0, 1.0, L, dtype=np.float32).astype(np.float64)[:, None]
    w = (2.0 * math.pi / L) * np.arange(L, dtype=np.float64)[:, None]
    f = np.linspace(1e-4, bands - 1, bands, dtype=np.float32).astype(np.float64)[None, :]
    z = np.concatenate([t, np.cos(f * w), -np.sin(f * w)], axis=-1)
    zp = np.zeros((L, LANES), np.float32)
    zp[:, :emb] = z
    return zp


def _filter_kernel(z_ref, win_ref, bin_ref, whid_ref, bhid_ref, fr_ref, wout_ref, dl_ref, o_ref, *, dh):
    z = z_ref[...]
    fr = fr_ref[0]
    h = jnp.sin(fr * (_dot(z, win_ref[0], precision=HIGHEST) + bin_ref[0]))
    for i in range(whid_ref.shape[1]):
        h = jnp.sin(fr * (_dot(h, whid_ref[0, i], precision=HIGHEST) + bhid_ref[0, i]))
    h = _dot(h, wout_ref[0], precision=HIGHEST)
    dec = jnp.exp(-z[:, 0:1] * dl_ref[...])
    o_ref[0, 0] = h[:, :dh] * dec
    o_ref[0, 1] = h[:, dh:] * dec


def _hyena_filters(L, f_w_in, f_b_in, f_w_hid, f_b_hid, f_freq, f_w_out, dh):
    depth, emb, hid = f_w_in.shape
    n_inner = f_w_hid.shape[1]
    tl = min(L, 1024)
    z = jnp.asarray(_filter_feats(L, emb))
    win = jnp.zeros((depth, LANES, hid), F32).at[:, :emb].set(f_w_in)
    deltas = np.abs(np.linspace(math.log(DECAY_TARGET) / FAST_DECAY_PCT,
                                math.log(DECAY_TARGET) / SLOW_DECAY_PCT, dh, dtype=np.float32))[None, :]
    kern = functools.partial(_filter_kernel, dh=dh)
    return pl.pallas_call(
        kern,
        grid=(depth, L // tl),
        in_specs=[pl.BlockSpec((tl, LANES), lambda l, i: (i, 0)),
                  pl.BlockSpec((1, LANES, hid), lambda l, i: (l, 0, 0)),
                  pl.BlockSpec((1, 1, hid), lambda l, i: (l, 0, 0)),
                  pl.BlockSpec((1, n_inner, hid, hid), lambda l, i: (l, 0, 0, 0)),
                  pl.BlockSpec((1, n_inner, 1, hid), lambda l, i: (l, 0, 0, 0)),
                  pl.BlockSpec((1, 1, hid), lambda l, i: (l, 0, 0)),
                  pl.BlockSpec((1, hid, 2 * dh), lambda l, i: (l, 0, 0)),
                  pl.BlockSpec((1, dh), lambda l, i: (0, 0))],
        out_specs=pl.BlockSpec((1, 2, tl, dh), lambda l, i: (l, 0, i, 0)),
        out_shape=jax.ShapeDtypeStruct((depth, 2, L, dh), F32),
        compiler_params=_cparams("parallel", "parallel"),
        name="hyena_filter",
    )(z, win, f_b_in.reshape(depth, 1, hid), f_w_hid, f_b_hid.reshape(depth, n_inner, 1, hid),
      f_freq.reshape(depth, 1, hid), f_w_out, jnp.asarray(deltas))


def _two_sided(h):
    fwd, bwd = h[:, 0], h[:, 1]
    zero = jnp.zeros_like(fwd[:, :1])
    return jnp.concatenate([fwd, zero, jnp.flip(bwd[:, 1:], axis=1)], axis=1)


def _dft_consts(n1, n2):
    n = n1 * n2
    n2h = n2 // 2
    k2 = np.arange(n2)[:, None]
    a = -2.0 * np.pi * k2 * np.arange(n2)[None, :] / n2
    fr, fi = np.cos(a), np.sin(a)
    f1 = np.block([[fr[:, :n2h], -fi[:, :n2h]], [fi[:, :n2h], fr[:, :n2h]]])
    f1k = np.concatenate([fr, fi], axis=0)
    cr, ci = fr[:n2h] / n, -fi[:n2h] / n
    f3 = np.block([[cr, -ci], [ci, cr]])
    b = -2.0 * np.pi * np.arange(n1)[:, None] * np.arange(n1)[None, :] / n1
    t = -2.0 * np.pi * np.arange(n2)[:, None] * np.arange(n1)[None, :] / n
    return dict(f1=f1.astype(np.float32), f1k=f1k.astype(np.float32), f3=f3.astype(np.float32),
                gr=np.cos(b).astype(np.float32), gi=np.sin(b).astype(np.float32),
                tr=np.cos(t).astype(np.float32)[:, None, :], ti=np.sin(t).astype(np.float32)[:, None, :])


def _tables_kernel(gr_ref, gi_ref, tr_ref, ti_ref, m1_ref, m2_ref, *, kb):
    gr, gi = gr_ref[...], gi_ref[...]
    for q in range(kb):
        tr, ti = tr_ref[q], ti_ref[q]
        re = gr * tr - gi * ti
        im = gr * ti + gi * tr
        m1_ref[q] = jnp.concatenate([jnp.concatenate([re, -im], axis=1),
                                     jnp.concatenate([im, re], axis=1)], axis=0).astype(BF16)
        ret, imt = re.T, im.T
        m2_ref[q] = jnp.concatenate([jnp.concatenate([ret, imt], axis=1),
                                     jnp.concatenate([-imt, ret], axis=1)], axis=0).astype(BF16)


def _dft_tables(c, n1, n2):
    kb = 8
    kern = functools.partial(_tables_kernel, kb=kb)
    shp = jax.ShapeDtypeStruct((n2, 2 * n1, 2 * n1), BF16)
    return pl.pallas_call(
        kern,
        grid=(n2 // kb,),
        in_specs=[pl.BlockSpec((n1, n1), lambda i: (0, 0)),
                  pl.BlockSpec((n1, n1), lambda i: (0, 0)),
                  pl.BlockSpec((kb, 1, n1), lambda i: (i, 0, 0)),
                  pl.BlockSpec((kb, 1, n1), lambda i: (i, 0, 0))],
        out_specs=[pl.BlockSpec((kb, 2 * n1, 2 * n1), lambda i: (i, 0, 0)),
                   pl.BlockSpec((kb, 2 * n1, 2 * n1), lambda i: (i, 0, 0))],
        out_shape=[shp, shp],
        compiler_params=_cparams("parallel"),
        name="dft_tables",
    )(jnp.asarray(c["gr"]), jnp.asarray(c["gi"]), jnp.asarray(c["tr"]), jnp.asarray(c["ti"]))


def _s1_kernel(z_ref, f_ref, o_ref, *, n2):
    x = jnp.concatenate([z_ref[0], z_ref[1]], axis=0)
    y = _dot(f_ref[...], x)
    o_ref[0, 0] = y[:n2]
    o_ref[0, 1] = y[n2:]


def _s1k_kernel(k_ref, f_ref, o_ref, *, n2):
    y = _dot(f_ref[...], k_ref[0].astype(BF16))
    o_ref[0, 0] = y[:n2]
    o_ref[0, 1] = y[n2:]


def _dft_stage1(zv, f1, n2, cb):
    b, n2h, w = zv.shape
    kern = functools.partial(_s1_kernel, n2=n2)
    return pl.pallas_call(
        kern,
        grid=(b // 2, w // cb),
        in_specs=[pl.BlockSpec((2, n2h, cb), lambda p, j: (p, 0, j)),
                  pl.BlockSpec((2 * n2, 2 * n2h), lambda p, j: (0, 0))],
        out_specs=pl.BlockSpec((1, 2, n2, cb), lambda p, j: (p, 0, 0, j)),
        out_shape=jax.ShapeDtypeStruct((b // 2, 2, n2, w), F32),
        compiler_params=_cparams("parallel", "parallel"),
        name="dft_stage1",
    )(zv, f1)


def _dft_stage1_filter(kv, f1k, n2, cb):
    depth, _, w = kv.shape
    kern = functools.partial(_s1k_kernel, n2=n2)
    return pl.pallas_call(
        kern,
        grid=(depth, w // cb),
        in_specs=[pl.BlockSpec((1, n2, cb), lambda l, j: (l, 0, j)),
                  pl.BlockSpec((2 * n2, n2), lambda l, j: (0, 0))],
        out_specs=pl.BlockSpec((1, 2, n2, cb), lambda l, j: (l, 0, 0, j)),
        out_shape=jax.ShapeDtypeStruct((depth, 2, n2, w), F32),
        compiler_params=_cparams("parallel", "parallel"),
        name="dft_stage1_filter",
    )(kv, f1k)


def _s2k_kernel(b_ref, m1_ref, o_ref, *, kb, n1):
    for q in range(kb):
        rows = slice(q * n1, (q + 1) * n1)
        xin = jnp.concatenate([b_ref[0, 0, rows, :], b_ref[0, 1, rows, :]], axis=0).astype(BF16)
        xf = _dot(m1_ref[q], xin)
        o_ref[0, 0, rows, :] = xf[:n1]
        o_ref[0, 1, rows, :] = xf[n1:]


def _dft_stage2_filter(bv, m1, n1, kb):
    depth, _, n, c = bv.shape
    kern = functools.partial(_s2k_kernel, kb=kb, n1=n1)
    return pl.pallas_call(
        kern,
        grid=(n // (kb * n1), depth),
        in_specs=[pl.BlockSpec((1, 2, kb * n1, c), lambda k, l: (l, 0, k, 0)),
                  pl.BlockSpec((kb, 2 * n1, 2 * n1), lambda k, l: (k, 0, 0))],
        out_specs=pl.BlockSpec((1, 2, kb * n1, c), lambda k, l: (l, 0, k, 0)),
        out_shape=jax.ShapeDtypeStruct((depth, 2, n, c), F32),
        compiler_params=_cparams("parallel", "parallel"),
        name="dft_stage2_filter",
    )(bv, m1)


def _s2_kernel(b_ref, m1_ref, kf_ref, m2_ref, o_ref, *, kb, n1):
    for q in range(kb):
        rows = slice(q * n1, (q + 1) * n1)
        xin = jnp.concatenate([b_ref[0, 0, rows, :], b_ref[0, 1, rows, :]], axis=0).astype(BF16)
        xf = _dot(m1_ref[q], xin)
        xr, xi = xf[:n1], xf[n1:]
        kr, ki = kf_ref[0, 0, rows, :], kf_ref[0, 1, rows, :]
        yin = jnp.concatenate([xr * kr - xi * ki, xr * ki + xi * kr], axis=0).astype(BF16)
        g = _dot(m2_ref[q], yin)
        o_ref[0, 0, rows, :] = g[:n1]
        o_ref[0, 1, rows, :] = g[n1:]


def _dft_stage2(bv, m1, kf, layer, m2, n1, kb):
    p, _, n, c = bv.shape
    kern = functools.partial(_s2_kernel, kb=kb, n1=n1)
    return pl.pallas_call(
        kern,
        grid=(n // (kb * n1), p),
        in_specs=[pl.BlockSpec((1, 2, kb * n1, c), lambda k, q: (q, 0, k, 0)),
                  pl.BlockSpec((kb, 2 * n1, 2 * n1), lambda k, q: (k, 0, 0)),
                  pl.BlockSpec((1, 2, kb * n1, c), lambda k, q: (layer, 0, k, 0)),
                  pl.BlockSpec((kb, 2 * n1, 2 * n1), lambda k, q: (k, 0, 0))],
        out_specs=pl.BlockSpec((1, 2, kb * n1, c), lambda k, q: (q, 0, k, 0)),
        out_shape=jax.ShapeDtypeStruct((p, 2, n, c), F32),
        compiler_params=_cparams("parallel", "arbitrary"),
        name="dft_stage2",
    )(bv, m1, kf, m2)


def _s3_kernel(g_ref, f_ref, z_ref, x0_ref, bias_ref, o_ref, *, n2h):
    gin = jnp.concatenate([g_ref[0, 0], g_ref[0, 1]], axis=0).astype(BF16)
    y = _dot(f_ref[...], gin)
    for s in range(2):
        conv = y[s * n2h:(s + 1) * n2h]
        yh = conv + z_ref[s].astype(F32) * bias_ref[...]
        o_ref[s] = (x0_ref[s].astype(F32) * yh).astype(BF16)


def _dft_stage3(gv, f3, zv, x0v, bias_t, cb):
    p, _, n2, w = gv.shape
    n2h = n2 // 2
    kern = functools.partial(_s3_kernel, n2h=n2h)
    return pl.pallas_call(
        kern,
        grid=(p, w // cb),
        in_specs=[pl.BlockSpec((1, 2, n2, cb), lambda q, j: (q, 0, 0, j)),
                  pl.BlockSpec((2 * n2h, 2 * n2), lambda q, j: (0, 0)),
                  pl.BlockSpec((2, n2h, cb), lambda q, j: (q, 0, j)),
                  pl.BlockSpec((2, n2h, cb), lambda q, j: (q, 0, j)),
                  pl.BlockSpec((1, cb), lambda q, j: (0, 0))],
        out_specs=pl.BlockSpec((2, n2h, cb), lambda q, j: (q, 0, j)),
        out_shape=jax.ShapeDtypeStruct((2 * p, n2h, w), BF16),
        compiler_params=_cparams("parallel", "parallel"),
        name="dft_stage3",
    )(gv, f3, zv, x0v, bias_t)


class _LongConv:
    def __init__(self, L, c, filt):
        n = 2 * L
        n1 = 1 << (int(math.log2(n)) // 2)
        n2 = n // n1
        assert n1 * n2 == n and n1 == n2, "long-convolution path needs 2L to be a square power of two"
        self.L, self.c, self.n1, self.n2 = L, c, n1, n2
        self.cb = min(n1 * c, 4096)
        self.kb = 4
        cst = _dft_consts(n1, n2)
        self.f1 = jnp.asarray(cst["f1"], BF16)
        self.f3 = jnp.asarray(cst["f3"], BF16)
        self.m1, self.m2 = _dft_tables(cst, n1, n2)
        depth = filt.shape[0]
        kv = _two_sided(filt).reshape(depth, n2, n1 * c)
        bk = _dft_stage1_filter(kv, jnp.asarray(cst["f1k"], BF16), n2, self.cb)
        self.kf = _dft_stage2_filter(bk.reshape(depth, 2, n, c), self.m1, n1, self.kb)

    def __call__(self, z, x0, bias, layer):
        b, L, c = z.shape
        n1, n2 = self.n1, self.n2
        zv = z.reshape(b, n2 // 2, n1 * c)
        x0v = x0.reshape(b, n2 // 2, n1 * c)
        bv = _dft_stage1(zv, self.f1, n2, self.cb)
        gv = _dft_stage2(bv.reshape(b // 2, 2, n1 * n2, c), self.m1, self.kf, layer, self.m2, n1, self.kb)
        bias_t = jnp.tile(bias.reshape(1, c), (1, self.cb // c))
        y = _dft_stage3(gv.reshape(b // 2, 2, n2, n1 * c), self.f3, zv, x0v, bias_t, self.cb)
        return y.reshape(b, L, c)


def _short_consts(L):
    n = 2 * L
    a = -2.0 * np.pi * np.arange(n)[:, None] * np.arange(n)[None, :] / n
    fr, fi = np.cos(a), np.sin(a)
    ff = np.block([[fr[:, :L], -fi[:, :L]], [fi[:, :L], fr[:, :L]]])
    fk = np.concatenate([fr, fi], axis=0)
    cr, ci = fr[:L] / n, -fi[:L] / n
    finv = np.block([[cr, -ci], [ci, cr]])
    return ff.astype(np.float32), fk.astype(np.float32), finv.astype(np.float32)


def _short_conv_kernel(z_ref, x0_ref, k_ref, ff_ref, fk_ref, fi_ref, bias_ref, o_ref, *, L):
    n = 2 * L
    x = jnp.concatenate([z_ref[0], z_ref[1]], axis=0)
    xf = _dot(ff_ref[...], x)
    kf = _dot(fk_ref[...], k_ref[0].astype(BF16))
    xr, xi, kr, ki = xf[:n], xf[n:], kf[:n], kf[n:]
    yin = jnp.concatenate([xr * kr - xi * ki, xr * ki + xi * kr], axis=0).astype(BF16)
    y = _dot(fi_ref[...], yin)
    for s in range(2):
        yh = y[s * L:(s + 1) * L] + z_ref[s].astype(F32) * bias_ref[...]
        o_ref[s] = (x0_ref[s].astype(F32) * yh).astype(BF16)


def _short_conv(z, x0, k2s, layer, consts, bias):
    b, L, c = z.shape
    n = 2 * L
    cb = min(c, 256)
    ff, fk, finv = consts
    kern = functools.partial(_short_conv_kernel, L=L)
    return pl.pallas_call(
        kern,
        grid=(b // 2, c // cb),
        in_specs=[pl.BlockSpec((2, L, cb), lambda p, j: (p, 0, j)),
                  pl.BlockSpec((2, L, cb), lambda p, j: (p, 0, j)),
                  pl.BlockSpec((1, n, cb), lambda p, j: (layer, 0, j)),
                  pl.BlockSpec((2 * n, 2 * L), lambda p, j: (0, 0)),
                  pl.BlockSpec((2 * n, n), lambda p, j: (0, 0)),
                  pl.BlockSpec((2 * L, 2 * n), lambda p, j: (0, 0)),
                  pl.BlockSpec((1, cb), lambda p, j: (0, j))],
        out_specs=pl.BlockSpec((2, L, cb), lambda p, j: (p, 0, j)),
        out_shape=jax.ShapeDtypeStruct((b, L, c), BF16),
        compiler_params=_cparams("parallel", "parallel"),
        name="short_conv",
    )(z, x0, k2s, ff, fk, finv, bias.reshape(1, c))


def _s5_matrices(lam_re, lam_im, log_step, b_re, b_im, c_re, c_im, d):
    f32 = F32
    T = S5_CHUNK
    lr, li = lam_re.astype(f32), lam_im.astype(f32)
    dt = jnp.exp(log_step.astype(f32))[..., None]
    mag = jnp.exp(lr * dt)
    a_r, a_i = mag * jnp.cos(li * dt), mag * jnp.sin(li * dt)
    den = lr * lr + li * li
    q_r = ((a_r - 1.0) * lr + a_i * li) / den
    q_i = (a_i * lr - (a_r - 1.0) * li) / den
    br, bi = b_re.astype(f32), b_im.astype(f32)
    bb_r = q_r[..., None] * br - q_i[..., None] * bi
    bb_i = q_r[..., None] * bi + q_i[..., None] * br
    pr, pi = [jnp.ones_like(a_r)], [jnp.zeros_like(a_i)]
    for _ in range(T):
        pr.append(pr[-1] * a_r - pi[-1] * a_i)
        pi.append(pr[-2] * a_i + pi[-1] * a_r)
    pw_r, pw_i = jnp.stack(pr), jnp.stack(pi)
    cr, ci = c_re.astype(f32), c_im.astype(f32)
    ca_r = cr[None] * pw_r[:, :, :, None, :] - ci[None] * pw_i[:, :, :, None, :]
    ca_i = cr[None] * pw_i[:, :, :, None, :] + ci[None] * pw_r[:, :, :, None, :]
    kk = (jnp.einsum('mdgop,dgph->mdgoh', ca_r[:T], bb_r, precision=HIGHEST)
          - jnp.einsum('mdgop,dgph->mdgoh', ca_i[:T], bb_i, precision=HIGHEST))
    kf, kb = kk[:, 0], kk[:, 1]
    G, H, P = cr.shape[1], cr.shape[2], cr.shape[3]
    zero = jnp.zeros_like(kf[:1])
    lagk = jnp.concatenate([jnp.flip(kb[1:], 0), kf[:1] + kb[:1], kf[1:]], axis=0)
    idx = (np.arange(T)[None, :] - np.arange(T)[:, None]) + (T - 1)
    toe = lagk[idx]
    m_mat = jnp.transpose(toe, (2, 0, 4, 1, 3)).reshape(G, T * H, T * H)
    del zero
    def cmul_pb(pwr, pwi, d_):
        er = pwr[..., None] * bb_r[d_][None] - pwi[..., None] * bb_i[d_][None]
        ei = pwr[..., None] * bb_i[d_][None] + pwi[..., None] * bb_r[d_][None]
        tr = lambda v: jnp.transpose(v, (1, 0, 3, 2)).reshape(G, T * H, P)
        return tr(er), tr(ei)
    ef_r, ef_i = cmul_pb(jnp.flip(pw_r[:T, 0], 0), jnp.flip(pw_i[:T, 0], 0), 0)
    eb_r, eb_i = cmul_pb(pw_r[:T, 1], pw_i[:T, 1], 1)
    e_mat = jnp.concatenate([ef_r, eb_r, ef_i, eb_i], axis=-1)
    def carry(car, cai):
        rr = jnp.transpose(car, (1, 3, 0, 2)).reshape(G, P, T * H)
        ri = jnp.transpose(-cai, (1, 3, 0, 2)).reshape(G, P, T * H)
        return rr, ri
    cf_r, cf_i = carry(ca_r[1:T + 1, 0], ca_i[1:T + 1, 0])
    cb_r, cb_i = carry(jnp.flip(ca_r[1:T + 1, 1], 0), jnp.flip(ca_i[1:T + 1, 1], 0))
    zp = jnp.zeros_like(cf_r)
    wf = jnp.concatenate([cf_r, zp, cf_i, zp], axis=1)
    wb = jnp.concatenate([zp, cb_r, zp, cb_i], axis=1)
    w_out = jnp.concatenate([m_mat, wf, wb], axis=1)
    at_r = jnp.concatenate([pw_r[T, 0], pw_r[T, 1]], axis=-1)[:, None, :]
    at_i = jnp.concatenate([pw_i[T, 0], pw_i[T, 1]], axis=-1)[:, None, :]
    d_t = jnp.tile(d.astype(f32).reshape(G, 1, H), (1, 1, T))
    return e_mat.astype(BF16), w_out.astype(BF16), at_r, at_i, d_t


def _s5_kernel(x_ref, e_ref, w_ref, ar_ref, ai_ref, d_ref, init_ref, y_ref, fin_ref,
               e_scr, sa_scr, sb_scr, *, gb, nb, nc, p2):
    nt = nc // 2
    lane = lax.broadcasted_iota(jnp.int32, (2 * nb, p2), 1)
    is_fwd = lane < (p2 // 2)
    first = lax.broadcasted_iota(jnp.int32, (2 * nb, p2), 0) < nb
    swap = lambda v: pltpu.roll(v, nb, axis=0)
    for g in range(gb):
        e_scr[g] = _dot(x_ref[g], e_ref[g])

    def step(k, carry):
        out = []
        rf = pl.ds(pl.multiple_of(k * 2 * nb, 2 * nb), 2 * nb)
        rb = pl.ds(pl.multiple_of((nt - 1 - k) * 2 * nb, 2 * nb), 2 * nb)
        for g in range(gb):
            cr, ci = carry[2 * g], carry[2 * g + 1]
            ar, ai = ar_ref[g], ai_ref[g]
            er = jnp.where(is_fwd, e_scr[g, rf, 0:p2], swap(e_scr[g, rb, 0:p2]))
            ei = jnp.where(is_fwd, e_scr[g, rf, p2:2 * p2], swap(e_scr[g, rb, p2:2 * p2]))
            ur = ar * cr - ai * ci + er
            ui = ar * ci + ai * cr + ei
            ur4, ui4 = swap(ur), swap(ui)
            sr = jnp.where(first, cr, ur4)
            si = jnp.where(first, ci, ui4)
            sa_scr[g, rf, 0:p2] = sr
            sa_scr[g, rf, p2:2 * p2] = si
            sb_scr[g, rb, 0:p2] = swap(sr)
            sb_scr[g, rb, p2:2 * p2] = swap(si)
            xr = jnp.where(first, ur, ur4)
            xi = jnp.where(first, ui, ui4)
            zr = ar * xr - ai * xi + er
            zi = ar * xi + ai * xr + ei
            out.append(jnp.where(first, swap(zr), zr))
            out.append(jnp.where(first, swap(zi), zi))
        return tuple(out)

    init = []
    for g in range(gb):
        init += [init_ref[g, :, 0:p2], init_ref[g, :, p2:2 * p2]]
    fin = lax.fori_loop(0, nt, step, tuple(init))
    for g in range(gb):
        fin_ref[g] = jnp.concatenate([fin[2 * g], fin[2 * g + 1]], axis=1)
        x = x_ref[g]
        lhs = jnp.concatenate([x, sa_scr[g].astype(BF16), sb_scr[g].astype(BF16)], axis=1)
        y = _dot(lhs, w_ref[g]) + x.astype(F32) * d_ref[g]
        y_ref[g] = jax.nn.gelu(y).astype(BF16)


def _s5_scan(xg, mats, init, nb):
    e_mat, w_out, at_r, at_i, d_t = mats
    G, R, th = xg.shape
    p4 = e_mat.shape[-1]
    p2 = p4 // 2
    gb = 2
    nc = R // nb
    assert 2 * nb == 8 and nc % 2 == 0, "two chunks of batch rows must fill one 8-sublane tile"
    kern = functools.partial(_s5_kernel, gb=gb, nb=nb, nc=nc, p2=p2)
    g3 = lambda i: (i, 0, 0)
    return pl.pallas_call(
        kern,
        grid=(G // gb,),
        in_specs=[pl.BlockSpec((gb, R, th), g3),
                  pl.BlockSpec((gb, th, p4), g3),
                  pl.BlockSpec((gb, th + 2 * p4, th), g3),
                  pl.BlockSpec((gb, 1, p2), g3),
                  pl.BlockSpec((gb, 1, p2), g3),
                  pl.BlockSpec((gb, 1, th), g3),
                  pl.BlockSpec((gb, 2 * nb, p4), g3)],
        out_specs=[pl.BlockSpec((gb, R, th), g3),
                   pl.BlockSpec((gb, 2 * nb, p4), g3)],
        out_shape=[jax.ShapeDtypeStruct((G, R, th), BF16),
                   jax.ShapeDtypeStruct((G, 2 * nb, p4), F32)],
        scratch_shapes=[pltpu.VMEM((gb, R, p4), F32),
                        pltpu.VMEM((gb, R, p4), F32),
                        pltpu.VMEM((gb, R, p4), F32)],
        compiler_params=_cparams("parallel"),
        name="s5_scan",
    )(xg, e_mat, w_out, at_r, at_i, d_t, init)


def _s5_mixer(u, mats, init, G, H):
    b, L, _ = u.shape
    T = S5_CHUNK
    lc = L // T
    xg = u.reshape(b, lc, T, G, H).transpose(3, 1, 0, 2, 4).reshape(G, lc * b, T * H)
    yg, fin = _s5_scan(xg, mats, init, b)
    y = yg.reshape(G, lc, b, T, H).transpose(2, 1, 3, 0, 4).reshape(b, L, G * H)
    return y, fin


def _outproj_kernel(yh_ref, ys_ref, x_ref, mod_ref, g1_ref, g2_ref, wg_ref, bg_ref, wo_ref,
                    xo_ref, hx_ref, *, d, dh):
    gate = mod_ref[0, :, 2 * d:3 * d]
    shift = mod_ref[0, :, 3 * d:4 * d]
    scale = mod_ref[0, :, 4 * d:5 * d]
    ys = ys_ref[0]
    glu = ys.astype(F32) * jax.nn.sigmoid(_dot(ys, wg_ref[...]) + bg_ref[...])
    yx = _dot(yh_ref[0], wo_ref[0:dh, :]) + _dot(glu.astype(BF16), wo_ref[dh:, :])
    xo = x_ref[0] + gate * _rms(yx, g1_ref[...])
    xo_ref[0] = xo
    hx_ref[0] = (_rms(xo, g2_ref[...]) * (1.0 + scale) + shift).astype(BF16)


def _outproj(yh, ys, x, mod_l, mod_row, g1, g2, w_glu, b_glu, w_out, tm):
    b, s, d = x.shape
    dh = yh.shape[-1]
    ds5 = ys.shape[-1]
    row = (lambda bi: bi) if mod_row is None else (lambda bi: mod_row)
    kern = functools.partial(_outproj_kernel, d=d, dh=dh)
    c2 = lambda bi, i: (0, 0)
    t3 = lambda bi, i: (bi, i, 0)
    return pl.pallas_call(
        kern,
        grid=(b, s // tm),
        in_specs=[pl.BlockSpec((1, tm, dh), t3),
                  pl.BlockSpec((1, tm, ds5), t3),
                  pl.BlockSpec((1, tm, d), t3),
                  pl.BlockSpec((1, 1, mod_l.shape[-1]), lambda bi, i: (row(bi), 0, 0)),
                  pl.BlockSpec((1, d), c2), pl.BlockSpec((1, d), c2),
                  pl.BlockSpec((ds5, ds5), c2), pl.BlockSpec((1, ds5), c2),
                  pl.BlockSpec((dh + ds5, d), c2)],
        out_specs=[pl.BlockSpec((1, tm, d), t3), pl.BlockSpec((1, tm, d), t3)],
        out_shape=[jax.ShapeDtypeStruct((b, s, d), F32), jax.ShapeDtypeStruct((b, s, d), BF16)],
        compiler_params=_cparams("parallel", "parallel"),
        name="outproj",
    )(yh, ys, x, mod_l, g1, g2, w_glu, b_glu, w_out)


def _ffn_kernel(hm_ref, hp_ref, hn_ref, x_ref, mod_ref, g3_ref, wg_ref, wv_ref, cw_ref, cb_ref, wd_ref,
                o_ref, acc_ref, *, tm, d, wg, vertical):
    i = pl.program_id(1)
    nt = pl.num_programs(1)
    j = pl.program_id(2)
    nj = pl.num_programs(2)
    hm = hm_ref[0]
    cw = cw_ref[...]
    if vertical:
        top = jnp.where(i > 0, 1.0, 0.0).astype(BF16)
        bot = jnp.where(i < nt - 1, 1.0, 0.0).astype(BF16)
        ha = jnp.concatenate([hp_ref[0] * top, hm, hn_ref[0] * bot], axis=0)
        rows = tm + 2 * wg
    else:
        ha = hm
        rows = tm
    g = _dot(ha, wg_ref[...])
    col = lax.broadcasted_iota(jnp.int32, g.shape, 0) & (wg - 1)
    gl = jnp.where(col > 0, pltpu.roll(g, 1, axis=0), 0.0)
    gr = jnp.where(col < wg - 1, pltpu.roll(g, rows - 1, axis=0), 0.0)

    def hrow(dy):
        return gl * cw[3 * dy:3 * dy + 1] + g * cw[3 * dy + 1:3 * dy + 2] + gr * cw[3 * dy + 2:3 * dy + 3]

    if vertical:
        up = hrow(0)
        dn = hrow(2)
        conv = hrow(1)[wg:wg + tm] + up[0:tm] + dn[2 * wg:2 * wg + tm]
    else:
        conv = hrow(1)
    conv = conv + cb_ref[...]
    v = _dot(hm, wv_ref[...])
    hmid = (jax.nn.gelu(conv) * v).astype(BF16)
    part = _dot(hmid, wd_ref[...])

    @pl.when(j == 0)
    def _():
        acc_ref[...] = part

    @pl.when(j > 0)
    def _():
        acc_ref[...] += part

    @pl.when(j == nj - 1)
    def _():
        gate = mod_ref[0, :, 5 * d:6 * d]
        o_ref[0] = x_ref[0] + gate * _rms(acc_ref[...], g3_ref[...])


def _ffn(hx, x, mod_l, mod_row, g3, w_up, conv_w, conv_b, w_down, tm, wg, vertical):
    b, s, d = x.shape
    f = w_down.shape[0]
    fc = f // 2 if (f // 2) % LANES == 0 else f
    nf = f // fc
    nt = s // tm
    r = tm // wg if vertical else 1
    hb = wg if vertical else 16
    nhb = s // hb
    row = (lambda bi: bi) if mod_row is None else (lambda bi: mod_row)
    kern = functools.partial(_ffn_kernel, tm=tm, d=d, wg=wg, vertical=vertical)
    t3 = lambda bi, i, j: (bi, i, 0)
    return pl.pallas_call(
        kern,
        grid=(b, nt, nf),
        in_specs=[pl.BlockSpec((1, tm, d), t3),
                  pl.BlockSpec((1, hb, d), lambda bi, i, j: (bi, jnp.maximum(i * r - 1, 0), 0)),
                  pl.BlockSpec((1, hb, d), lambda bi, i, j: (bi, jnp.minimum((i + 1) * r, nhb - 1), 0)),
                  pl.BlockSpec((1, tm, d), t3),
                  pl.BlockSpec((1, 1, mod_l.shape[-1]), lambda bi, i, j: (row(bi), 0, 0)),
                  pl.BlockSpec((1, d), lambda bi, i, j: (0, 0)),
                  pl.BlockSpec((d, fc), lambda bi, i, j: (0, j)),
                  pl.BlockSpec((d, fc), lambda bi, i, j: (0, nf + j)),
                  pl.BlockSpec((9, fc), lambda bi, i, j: (0, j)),
                  pl.BlockSpec((1, fc), lambda bi, i, j: (0, j)),
                  pl.BlockSpec((fc, d), lambda bi, i, j: (j, 0))],
        out_specs=pl.BlockSpec((1, tm, d), t3),
        out_shape=jax.ShapeDtypeStruct((b, s, d), F32),
        scratch_shapes=[pltpu.VMEM((tm, d), F32)],
        compiler_params=_cparams("parallel", "parallel", "arbitrary"),
        name="conv_glu_ffn",
    )(hx, hx, hx, x, mod_l, g3, w_up, w_up, conv_w, conv_b, w_down)


def kernel(x, c, ctx, c_ctx, w_ada, b_ada, norm_g, w_in, hy_short_w, hy_short_b,
           filt_w_in, filt_b_in, filt_w_hid, filt_b_hid, filt_freq, filt_w_out, hy_bias,
           s5_lam_re, s5_lam_im, s5_log_step, s5_b_re, s5_b_im, s5_c_re, s5_c_im, s5_d,
           s5_w_glu, s5_b_glu, w_out, ffn_w_up, ffn_conv_w, ffn_conv_b, ffn_w_down):
    depth = w_ada.shape[0]
    bsz, seq, d = x.shape
    lctx = ctx.shape[1]
    dh = hy_bias.shape[-1]
    G, P, H = s5_b_re.shape[2], s5_b_re.shape[3], s5_b_re.shape[4]
    dff = ffn_w_down.shape[1]
    assert bsz % 2 == 0 and bsz <= 4 and seq % GRID_W == 0 and GRID_W & (GRID_W - 1) == 0

    cond = jnp.zeros((8, d), F32).at[:bsz].set(c).at[bsz].set(c_ctx)
    mod = _ada_mod(cond, w_ada, b_ada)
    ctx_row = bsz

    filt_args = (filt_w_in, filt_b_in, filt_w_hid, filt_b_hid, filt_freq, filt_w_out)
    long_conv = _LongConv(seq, dh, _hyena_filters(seq, *filt_args, dh))
    k_ctx = _two_sided(_hyena_filters(lctx, *filt_args, dh))
    short_c = tuple(jnp.asarray(a, BF16) for a in _short_consts(lctx))

    w_in_b = w_in.astype(BF16)
    w_glu_b = s5_w_glu.astype(BF16)
    w_out_b = w_out.astype(BF16)
    w_up_b = ffn_w_up.astype(BF16)
    w_down_b = ffn_w_down.astype(BF16)

    tm = min(seq, ROW_TILE)
    for l in range(depth):
        last = l == depth - 1
        mod_l = mod[l].reshape(8, 1, 6 * d)
        g = norm_g[l]
        mats = _s5_matrices(s5_lam_re[l], s5_lam_im[l], s5_log_step[l], s5_b_re[l], s5_b_im[l],
                            s5_c_re[l], s5_c_im[l], s5_d[l])
        sb = hy_short_b[l].reshape(1, 3 * dh)
        cw = ffn_conv_w[l].reshape(9, dff)
        cb = ffn_conv_b[l].reshape(1, dff)
        bg = s5_b_glu[l].reshape(1, -1)

        x0c, zc, uc = _inproj(ctx, mod_l, ctx_row, g[0:1], w_in_b[l], hy_short_w[l], sb, dh, lctx)
        ysc, ctx_state = _s5_mixer(uc, mats, jnp.zeros((G, 2 * bsz, 4 * P), F32), G, H)

        x0, z, u = _inproj(x, mod_l, None, g[0:1], w_in_b[l], hy_short_w[l], sb, dh, tm)
        ys, _ = _s5_mixer(u, mats, ctx_state, G, H)
        yh = long_conv(z, x0, hy_bias[l], l)
        x, hx = _outproj(yh, ys, x, mod_l, None, g[1:2], g[2:3], w_glu_b[l], bg, w_out_b[l], tm)
        x = _ffn(hx, x, mod_l, None, g[3:4], w_up_b[l], cw, cb, w_down_b[l], tm, GRID_W, True)

        if not last:
            yhc = _short_conv(zc, x0c, k_ctx, l, short_c, hy_bias[l])
            ctx, hc = _outproj(yhc, ysc, ctx, mod_l, ctx_row, g[1:2], g[2:3], w_glu_b[l], bg, w_out_b[l], lctx)
            ctx = _ffn(hc, ctx, mod_l, ctx_row, g[3:4], w_up_b[l], cw, cb, w_down_b[l], lctx, lctx, False)
    return x
```

```python
import functools
import math

import numpy as np
import jax
import jax.numpy as jnp
from jax import lax
from jax.experimental import pallas as pl
from jax.experimental.pallas import tpu as pltpu

GRID_W = 64
RMS_EPS = 1e-6
DECAY_TARGET = 1e-2
FAST_DECAY_PCT = 0.3
SLOW_DECAY_PCT = 1.5
S5_CHUNK = 16
LANES = 128
HALO = 16
ROW_TILE = 512
VMEM_LIMIT = 56 * 1024 * 1024

F32 = jnp.float32
BF16 = jnp.bfloat16
HIGHEST = lax.Precision.HIGHEST


def _cparams(*sem):
    return pltpu.CompilerParams(dimension_semantics=sem, vmem_limit_bytes=VMEM_LIMIT)


def _dot(a, b, **kw):
    return jnp.dot(a, b, preferred_element_type=F32, **kw)


def _ada_kernel(cond_ref, w_ref, b_ref, o_ref):
    cv = cond_ref[...]
    s = cv * jax.nn.sigmoid(cv)
    o_ref[0] = _dot(s, w_ref[0], precision=HIGHEST) + b_ref[0]


def _ada_mod(cond, w_ada, b_ada):
    depth, d, n = w_ada.shape
    tn = n // 4
    return pl.pallas_call(
        _ada_kernel,
        grid=(depth, n // tn),
        in_specs=[pl.BlockSpec((8, d), lambda l, j: (0, 0)),
                  pl.BlockSpec((1, d, tn), lambda l, j: (l, 0, j)),
                  pl.BlockSpec((1, 1, tn), lambda l, j: (l, 0, j))],
        out_specs=pl.BlockSpec((1, 8, tn), lambda l, j: (l, 0, j)),
        out_shape=jax.ShapeDtypeStruct((depth, 8, n), F32),
        compiler_params=_cparams("parallel", "parallel"),
        name="ada_mod",
    )(cond, w_ada, b_ada.reshape(depth, 1, n))


def _rms(v, g):
    ms = jnp.mean(v * v, axis=-1, keepdims=True)
    return v * lax.rsqrt(ms + RMS_EPS) * g


def _inproj_kernel(xm_ref, xp_ref, xn_ref, mod_ref, g_ref, w_ref, sw_ref, sb_ref,
                   x0_ref, z_ref, u_ref, p_scr, *, tm, d, dh):
    i = pl.program_id(1)
    nt = pl.num_programs(1)
    shift = mod_ref[0, :, 0:d]
    scale = mod_ref[0, :, d:2 * d]
    xa = jnp.concatenate([xp_ref[0], xm_ref[0], xn_ref[0]], axis=0)
    xn = (_rms(xa, g_ref[0, 0:1, :]) * (1.0 + scale) + shift).astype(BF16)
    p = _dot(xn, w_ref[0])
    p_scr[...] = p[:, :3 * dh]

    @pl.when(i == 0)
    def _():
        p_scr[HALO - 8:HALO, :] = jnp.zeros((8, 3 * dh), F32)

    @pl.when(i == nt - 1)
    def _():
        p_scr[HALO + tm:HALO + tm + 8, :] = jnp.zeros((8, 3 * dh), F32)

    sw = sw_ref[0]
    conv = (p_scr[pl.ds(HALO - 1, tm), :] * sw[0:1] + p_scr[pl.ds(HALO, tm), :] * sw[1:2]
            + p_scr[pl.ds(HALO + 1, tm), :] * sw[2:3] + sb_ref[0])
    x0_ref[0] = conv[:, :dh].astype(BF16)
    z_ref[0] = (conv[:, dh:2 * dh] * conv[:, 2 * dh:]).astype(BF16)
    u = p[HALO:HALO + tm, 3 * dh:].astype(BF16)
    u_ref[:, 0, :, :] = u.reshape(tm // S5_CHUNK, S5_CHUNK, u.shape[-1])


def _inproj(x, mod, mod_row, layer, norm_g, w_in, sw, sb, dh, tm):
    b, s, d = x.shape
    dp = w_in.shape[-1]
    ds5 = dp - 3 * dh
    nt = s // tm
    r = tm // HALO
    nh = s // HALO
    tc = tm // S5_CHUNK
    kern = functools.partial(_inproj_kernel, tm=tm, d=d, dh=dh)
    lyr = lambda bi, i: (layer, 0, 0)
    return pl.pallas_call(
        kern,
        grid=(b, nt),
        in_specs=[pl.BlockSpec((1, tm, d), lambda bi, i: (bi, i, 0)),
                  pl.BlockSpec((1, HALO, d), lambda bi, i: (bi, jnp.maximum(i * r - 1, 0), 0)),
                  pl.BlockSpec((1, HALO, d), lambda bi, i: (bi, jnp.minimum((i + 1) * r, nh - 1), 0)),
                  pl.BlockSpec((1, 1, mod.shape[-1]), lambda bi, i: (mod_row(bi), 0, 0)),
                  pl.BlockSpec((1,) + norm_g.shape[1:], lyr),
                  pl.BlockSpec((1, d, dp), lyr),
                  pl.BlockSpec((1, 3, 3 * dh), lyr),
                  pl.BlockSpec((1, 1, 3 * dh), lyr)],
        out_specs=[pl.BlockSpec((1, tm, dh), lambda bi, i: (bi, i, 0)),
                   pl.BlockSpec((1, tm, dh), lambda bi, i: (bi, i, 0)),
                   pl.BlockSpec((tc, 1, S5_CHUNK, ds5), lambda bi, i: (i, bi, 0, 0))],
        out_shape=[jax.ShapeDtypeStruct((b, s, dh), BF16),
                   jax.ShapeDtypeStruct((b, s, dh), BF16),
                   jax.ShapeDtypeStruct((s // S5_CHUNK, b, S5_CHUNK, ds5), BF16)],
        scratch_shapes=[pltpu.VMEM((tm + 2 * HALO, 3 * dh), F32)],
        compiler_params=_cparams("parallel", "arbitrary"),
        name="inproj",
    )(x, x, x, mod, norm_g, w_in, sw, sb)


def _filter_feats(L, emb):
    bands = (emb - 1) // 2
    t = np.linspace(0.0, 1.0, L, dtype=np.float32).astype(np.float64)[:, None]
    w = (2.0 * math.pi / L) * np.arange(L, dtype=np.float64)[:, None]
    f = np.linspace(1e-4, bands - 1, bands, dtype=np.float32).astype(np.float64)[None, :]
    z = np.concatenate([t, np.cos(f * w), -np.sin(f * w)], axis=-1)
    zp = np.zeros((L, LANES), np.float32)
    zp[:, :emb] = z
    return zp


def _filter_kernel(z_ref, win_ref, bin_ref, whid_ref, bhid_ref, fr_ref, wout_ref, dl_ref, o_ref, *, L, tl):
    z = z_ref[...]
    fr = fr_ref[0]
    h = jnp.sin(fr * (_dot(z, win_ref[0], precision=HIGHEST) + bin_ref[0]))
    for i in range(whid_ref.shape[1]):
        h = jnp.sin(fr * (_dot(h, whid_ref[0, i], precision=HIGHEST) + bhid_ref[0, i]))
    h = _dot(h, wout_ref[0], precision=HIGHEST)
    dec = jnp.exp(-z[:, 0:1] * dl_ref[...])
    pos = pl.program_id(1) * tl + lax.broadcasted_iota(jnp.int32, (tl, 1), 0)
    o_ref[0] = jnp.where(pos == L, 0.0, h * dec)


def _hyena_filters(L, f_w_in, f_b_in, f_w_hid, f_b_hid, f_freq, f_w_out, dh):
    depth, emb, hid = f_w_in.shape
    n_inner = f_w_hid.shape[1]
    tl = min(L, 1024)
    nl = L // tl
    z1 = _filter_feats(L, emb)
    z = jnp.asarray(np.concatenate([z1, z1[:1], z1[:0:-1]], axis=0))
    win = jnp.zeros((depth, LANES, hid), F32).at[:, :emb].set(f_w_in)
    deltas = np.abs(np.linspace(math.log(DECAY_TARGET) / FAST_DECAY_PCT,
                                math.log(DECAY_TARGET) / SLOW_DECAY_PCT, dh, dtype=np.float32))[None, :]
    kern = functools.partial(_filter_kernel, L=L, tl=tl)
    return pl.pallas_call(
        kern,
        grid=(depth, 2 * nl),
        in_specs=[pl.BlockSpec((tl, LANES), lambda l, i: (i, 0)),
                  pl.BlockSpec((1, LANES, hid), lambda l, i: (l, 0, 0)),
                  pl.BlockSpec((1, 1, hid), lambda l, i: (l, 0, 0)),
                  pl.BlockSpec((1, n_inner, hid, hid), lambda l, i: (l, 0, 0, 0)),
                  pl.BlockSpec((1, n_inner, 1, hid), lambda l, i: (l, 0, 0, 0)),
                  pl.BlockSpec((1, 1, hid), lambda l, i: (l, 0, 0)),
                  pl.BlockSpec((1, hid, dh), lambda l, i: (l, 0, jnp.where(i >= nl, 1, 0))),
                  pl.BlockSpec((1, dh), lambda l, i: (0, 0))],
        out_specs=pl.BlockSpec((1, tl, dh), lambda l, i: (l, i, 0)),
        out_shape=jax.ShapeDtypeStruct((depth, 2 * L, dh), F32),
        compiler_params=_cparams("parallel", "parallel"),
        name="hyena_filter",
    )(z, win, f_b_in.reshape(depth, 1, hid), f_w_hid, f_b_hid.reshape(depth, n_inner, 1, hid),
      f_freq.reshape(depth, 1, hid), f_w_out, jnp.asarray(deltas))


def _dft_consts(n1, n2):
    n = n1 * n2
    n2h = n2 // 2
    k2 = np.arange(n2)[:, None]
    a = -2.0 * np.pi * k2 * np.arange(n2)[None, :] / n2
    fr, fi = np.cos(a), np.sin(a)
    f1 = np.block([[fr[:, :n2h], -fi[:, :n2h]], [fi[:, :n2h], fr[:, :n2h]]])
    f1k = np.concatenate([fr, fi], axis=0)
    cr, ci = fr[:n2h] / n, -fi[:n2h] / n
    f3 = np.block([[cr, -ci], [ci, cr]])
    b = -2.0 * np.pi * np.arange(n1)[:, None] * np.arange(n1)[None, :] / n1
    t = -2.0 * np.pi * np.arange(n2)[:, None] * np.arange(n1)[None, :] / n
    return dict(f1=f1.astype(np.float32), f1k=f1k.astype(np.float32), f3=f3.astype(np.float32),
                gr=np.cos(b).astype(np.float32), gi=np.sin(b).astype(np.float32),
                tr=np.cos(t).astype(np.float32)[:, None, :], ti=np.sin(t).astype(np.float32)[:, None, :])


def _tables_kernel(gr_ref, gi_ref, tr_ref, ti_ref, m1_ref, m2_ref, *, kb):
    gr, gi = gr_ref[...], gi_ref[...]
    for q in range(kb):
        tr, ti = tr_ref[q], ti_ref[q]
        re = gr * tr - gi * ti
        im = gr * ti + gi * tr
        m1_ref[q] = jnp.concatenate([jnp.concatenate([re, -im], axis=1),
                                     jnp.concatenate([im, re], axis=1)], axis=0).astype(BF16)
        ret, imt = re.T, im.T
        m2_ref[q] = jnp.concatenate([jnp.concatenate([ret, imt], axis=1),
                                     jnp.concatenate([-imt, ret], axis=1)], axis=0).astype(BF16)


def _dft_tables(c, n1, n2):
    kb = 8
    kern = functools.partial(_tables_kernel, kb=kb)
    shp = jax.ShapeDtypeStruct((n2, 2 * n1, 2 * n1), BF16)
    return pl.pallas_call(
        kern,
        grid=(n2 // kb,),
        in_specs=[pl.BlockSpec((n1, n1), lambda i: (0, 0)),
                  pl.BlockSpec((n1, n1), lambda i: (0, 0)),
                  pl.BlockSpec((kb, 1, n1), lambda i: (i, 0, 0)),
                  pl.BlockSpec((kb, 1, n1), lambda i: (i, 0, 0))],
        out_specs=[pl.BlockSpec((kb, 2 * n1, 2 * n1), lambda i: (i, 0, 0)),
                   pl.BlockSpec((kb, 2 * n1, 2 * n1), lambda i: (i, 0, 0))],
        out_shape=[shp, shp],
        compiler_params=_cparams("parallel"),
        name="dft_tables",
    )(jnp.asarray(c["gr"]), jnp.asarray(c["gi"]), jnp.asarray(c["tr"]), jnp.asarray(c["ti"]))


def _s1_kernel(z_ref, f_ref, o_ref, *, n2, nb1):
    for j in range(nb1):
        x = jnp.concatenate([z_ref[0, :, j, :], z_ref[1, :, j, :]], axis=0)
        y = _dot(f_ref[...], x)
        o_ref[0, 0, :, j, :] = y[:n2]
        o_ref[0, 1, :, j, :] = y[n2:]


def _s1k_kernel(k_ref, f_ref, o_ref, *, n2, nb1):
    for j in range(nb1):
        y = _dot(f_ref[...], k_ref[0, :, j, :].astype(BF16))
        o_ref[0, 0, :, j, :] = y[:n2]
        o_ref[0, 1, :, j, :] = y[n2:]


def _dft_stage1(z4, f1, n2, nb1):
    b, n2h, n1, c = z4.shape
    kern = functools.partial(_s1_kernel, n2=n2, nb1=nb1)
    return pl.pallas_call(
        kern,
        grid=(b // 2, n1 // nb1),
        in_specs=[pl.BlockSpec((2, n2h, nb1, c), lambda p, j: (p, 0, j, 0)),
                  pl.BlockSpec((2 * n2, 2 * n2h), lambda p, j: (0, 0))],
        out_specs=pl.BlockSpec((1, 2, n2, nb1, c), lambda p, j: (p, 0, 0, j, 0)),
        out_shape=jax.ShapeDtypeStruct((b // 2, 2, n2, n1, c), F32),
        compiler_params=_cparams("parallel", "parallel"),
        name="dft_stage1",
    )(z4, f1)


def _dft_stage1_filter(k4, f1k, nb1):
    depth, n2, n1, c = k4.shape
    kern = functools.partial(_s1k_kernel, n2=n2, nb1=nb1)
    return pl.pallas_call(
        kern,
        grid=(depth, n1 // nb1),
        in_specs=[pl.BlockSpec((1, n2, nb1, c), lambda l, j: (l, 0, j, 0)),
                  pl.BlockSpec((2 * n2, n2), lambda l, j: (0, 0))],
        out_specs=pl.BlockSpec((1, 2, n2, nb1, c), lambda l, j: (l, 0, 0, j, 0)),
        out_shape=jax.ShapeDtypeStruct((depth, 2, n2, n1, c), F32),
        compiler_params=_cparams("parallel", "parallel"),
        name="dft_stage1_filter",
    )(k4, f1k)


def _s2k_kernel(b_ref, m1_ref, o_ref, *, kb, n1):
    for q in range(kb):
        rows = slice(q * n1, (q + 1) * n1)
        xin = jnp.concatenate([b_ref[0, 0, rows, :], b_ref[0, 1, rows, :]], axis=0).astype(BF16)
        xf = _dot(m1_ref[q], xin)
        o_ref[0, 0, rows, :] = xf[:n1]
        o_ref[0, 1, rows, :] = xf[n1:]


def _dft_stage2_filter(bv, m1, n1, kb):
    depth, _, n, c = bv.shape
    kern = functools.partial(_s2k_kernel, kb=kb, n1=n1)
    return pl.pallas_call(
        kern,
        grid=(n // (kb * n1), depth),
        in_specs=[pl.BlockSpec((1, 2, kb * n1, c), lambda k, l: (l, 0, k, 0)),
                  pl.BlockSpec((kb, 2 * n1, 2 * n1), lambda k, l: (k, 0, 0))],
        out_specs=pl.BlockSpec((1, 2, kb * n1, c), lambda k, l: (l, 0, k, 0)),
        out_shape=jax.ShapeDtypeStruct((depth, 2, n, c), F32),
        compiler_params=_cparams("parallel", "parallel"),
        name="dft_stage2_filter",
    )(bv, m1)


def _s2_kernel(b_ref, m1_ref, kf_ref, m2_ref, o_ref, *, kb, n1):
    for q in range(kb):
        rows = slice(q * n1, (q + 1) * n1)
        xin = jnp.concatenate([b_ref[0, 0, rows, :], b_ref[0, 1, rows, :]], axis=0).astype(BF16)
        xf = _dot(m1_ref[q], xin)
        xr, xi = xf[:n1], xf[n1:]
        kr, ki = kf_ref[0, 0, rows, :], kf_ref[0, 1, rows, :]
        yin = jnp.concatenate([xr * kr - xi * ki, xr * ki + xi * kr], axis=0).astype(BF16)
        g = _dot(m2_ref[q], yin)
        o_ref[0, 0, rows, :] = g[:n1]
        o_ref[0, 1, rows, :] = g[n1:]


def _dft_stage2(bv, m1, kf, layer, m2, n1, kb):
    p, _, n, c = bv.shape
    kern = functools.partial(_s2_kernel, kb=kb, n1=n1)
    return pl.pallas_call(
        kern,
        grid=(n // (kb * n1), p),
        in_specs=[pl.BlockSpec((1, 2, kb * n1, c), lambda k, q: (q, 0, k, 0)),
                  pl.BlockSpec((kb, 2 * n1, 2 * n1), lambda k, q: (k, 0, 0)),
                  pl.BlockSpec((1, 2, kb * n1, c), lambda k, q: (layer, 0, k, 0)),
                  pl.BlockSpec((kb, 2 * n1, 2 * n1), lambda k, q: (k, 0, 0))],
        out_specs=pl.BlockSpec((1, 2, kb * n1, c), lambda k, q: (q, 0, k, 0)),
        out_shape=jax.ShapeDtypeStruct((p, 2, n, c), F32),
        compiler_params=_cparams("parallel", "arbitrary"),
        name="dft_stage2",
    )(bv, m1, kf, m2)


def _s3_kernel(g_ref, f_ref, z_ref, x0_ref, bias_ref, o_ref, *, n2h, nb1):
    bias = bias_ref[0]
    for j in range(nb1):
        gin = jnp.concatenate([g_ref[0, 0, :, j, :], g_ref[0, 1, :, j, :]], axis=0).astype(BF16)
        y = _dot(f_ref[...], gin)
        for s in range(2):
            yh = y[s * n2h:(s + 1) * n2h] + z_ref[s, :, j, :].astype(F32) * bias
            o_ref[s, :, j, :] = (x0_ref[s, :, j, :].astype(F32) * yh).astype(BF16)


def _dft_stage3(g5, f3, z4, x04, bias, layer, nb1):
    p, _, n2, n1, c = g5.shape
    n2h = n2 // 2
    kern = functools.partial(_s3_kernel, n2h=n2h, nb1=nb1)
    return pl.pallas_call(
        kern,
        grid=(p, n1 // nb1),
        in_specs=[pl.BlockSpec((1, 2, n2, nb1, c), lambda q, j: (q, 0, 0, j, 0)),
                  pl.BlockSpec((2 * n2h, 2 * n2), lambda q, j: (0, 0)),
                  pl.BlockSpec((2, n2h, nb1, c), lambda q, j: (q, 0, j, 0)),
                  pl.BlockSpec((2, n2h, nb1, c), lambda q, j: (q, 0, j, 0)),
                  pl.BlockSpec((1, 1, c), lambda q, j: (layer, 0, 0))],
        out_specs=pl.BlockSpec((2, n2h, nb1, c), lambda q, j: (q, 0, j, 0)),
        out_shape=jax.ShapeDtypeStruct((2 * p, n2h, n1, c), BF16),
        compiler_params=_cparams("parallel", "parallel"),
        name="dft_stage3",
    )(g5, f3, z4, x04, bias)


class _LongConv:
    def __init__(self, L, c, k2s):
        n = 2 * L
        n1 = 1 << (int(math.log2(n)) // 2)
        n2 = n // n1
        assert n1 * n2 == n and n1 == n2, "long-convolution path needs 2L to be a square power of two"
        self.L, self.c, self.n1, self.n2 = L, c, n1, n2
        self.nb1 = min(n1, 16)
        self.kb = 4
        cst = _dft_consts(n1, n2)
        self.f1 = jnp.asarray(cst["f1"], BF16)
        self.f3 = jnp.asarray(cst["f3"], BF16)
        self.m1, self.m2 = _dft_tables(cst, n1, n2)
        depth = k2s.shape[0]
        bk = _dft_stage1_filter(k2s.reshape(depth, n2, n1, c), jnp.asarray(cst["f1k"], BF16), self.nb1)
        self.kf = _dft_stage2_filter(bk.reshape(depth, 2, n, c), self.m1, n1, self.kb)

    def __call__(self, z, x0, bias, layer):
        b, L, c = z.shape
        n1, n2 = self.n1, self.n2
        z4 = z.reshape(b, n2 // 2, n1, c)
        x04 = x0.reshape(b, n2 // 2, n1, c)
        b5 = _dft_stage1(z4, self.f1, n2, self.nb1)
        gv = _dft_stage2(b5.reshape(b // 2, 2, n1 * n2, c), self.m1, self.kf, layer, self.m2, n1, self.kb)
        y = _dft_stage3(gv.reshape(b // 2, 2, n2, n1, c), self.f3, z4, x04, bias, layer, self.nb1)
        return y.reshape(b, L, c)


def _short_consts(L):
    n = 2 * L
    a = -2.0 * np.pi * np.arange(n)[:, None] * np.arange(n)[None, :] / n
    fr, fi = np.cos(a), np.sin(a)
    ff = np.block([[fr[:, :L], -fi[:, :L]], [fi[:, :L], fr[:, :L]]])
    fk = np.concatenate([fr, fi], axis=0)
    cr, ci = fr[:L] / n, -fi[:L] / n
    finv = np.block([[cr, -ci], [ci, cr]])
    return ff.astype(np.float32), fk.astype(np.float32), finv.astype(np.float32)


def _short_conv_kernel(z_ref, x0_ref, k_ref, ff_ref, fk_ref, fi_ref, bias_ref, o_ref, *, L):
    n = 2 * L
    x = jnp.concatenate([z_ref[0], z_ref[1]], axis=0)
    xf = _dot(ff_ref[...], x)
    kf = _dot(fk_ref[...], k_ref[0].astype(BF16))
    xr, xi, kr, ki = xf[:n], xf[n:], kf[:n], kf[n:]
    yin = jnp.concatenate([xr * kr - xi * ki, xr * ki + xi * kr], axis=0).astype(BF16)
    y = _dot(fi_ref[...], yin)
    for s in range(2):
        yh = y[s * L:(s + 1) * L] + z_ref[s].astype(F32) * bias_ref[0]
        o_ref[s] = (x0_ref[s].astype(F32) * yh).astype(BF16)


def _short_conv(z, x0, k2s, layer, consts, bias):
    b, L, c = z.shape
    n = 2 * L
    cb = min(c, 256)
    ff, fk, finv = consts
    kern = functools.partial(_short_conv_kernel, L=L)
    return pl.pallas_call(
        kern,
        grid=(b // 2, c // cb),
        in_specs=[pl.BlockSpec((2, L, cb), lambda p, j: (p, 0, j)),
                  pl.BlockSpec((2, L, cb), lambda p, j: (p, 0, j)),
                  pl.BlockSpec((1, n, cb), lambda p, j: (layer, 0, j)),
                  pl.BlockSpec((2 * n, 2 * L), lambda p, j: (0, 0)),
                  pl.BlockSpec((2 * n, n), lambda p, j: (0, 0)),
                  pl.BlockSpec((2 * L, 2 * n), lambda p, j: (0, 0)),
                  pl.BlockSpec((1, 1, cb), lambda p, j: (layer, 0, j))],
        out_specs=pl.BlockSpec((2, L, cb), lambda p, j: (p, 0, j)),
        out_shape=jax.ShapeDtypeStruct((b, L, c), BF16),
        compiler_params=_cparams("parallel", "parallel"),
        name="short_conv",
    )(z, x0, k2s, ff, fk, finv, bias)


def _s5_matrices(lam_re, lam_im, log_step, b_re, b_im, c_re, c_im, d):
    f32 = F32
    T = S5_CHUNK
    lr, li = lam_re.astype(f32), lam_im.astype(f32)
    dt = jnp.exp(log_step.astype(f32))[..., None]
    mag = jnp.exp(lr * dt)
    a_r, a_i = mag * jnp.cos(li * dt), mag * jnp.sin(li * dt)
    den = lr * lr + li * li
    q_r = ((a_r - 1.0) * lr + a_i * li) / den
    q_i = (a_i * lr - (a_r - 1.0) * li) / den
    br, bi = b_re.astype(f32), b_im.astype(f32)
    bb_r = q_r[..., None] * br - q_i[..., None] * bi
    bb_i = q_r[..., None] * bi + q_i[..., None] * br
    pr, pi = [jnp.ones_like(a_r)], [jnp.zeros_like(a_i)]
    for _ in range(T):
        pr.append(pr[-1] * a_r - pi[-1] * a_i)
        pi.append(pr[-2] * a_i + pi[-1] * a_r)
    pw_r, pw_i = jnp.stack(pr), jnp.stack(pi)
    cr, ci = c_re.astype(f32), c_im.astype(f32)
    ca_r = cr[None] * pw_r[:, :, :, None, :] - ci[None] * pw_i[:, :, :, None, :]
    ca_i = cr[None] * pw_i[:, :, :, None, :] + ci[None] * pw_r[:, :, :, None, :]
    kk = (jnp.einsum('mdgop,dgph->mdgoh', ca_r[:T], bb_r, precision=HIGHEST)
          - jnp.einsum('mdgop,dgph->mdgoh', ca_i[:T], bb_i, precision=HIGHEST))
    kf, kb = kk[:, 0], kk[:, 1]
    G, H, P = cr.shape[1], cr.shape[2], cr.shape[3]
    lagk = jnp.concatenate([jnp.flip(kb[1:], 0), kf[:1] + kb[:1], kf[1:]], axis=0)
    idx = (np.arange(T)[None, :] - np.arange(T)[:, None]) + (T - 1)
    toe = lagk[idx]
    m_mat = jnp.transpose(toe, (2, 0, 4, 1, 3)).reshape(G, T * H, T * H)
    def cmul_pb(pwr, pwi, d_):
        er = pwr[..., None] * bb_r[d_][None] - pwi[..., None] * bb_i[d_][None]
        ei = pwr[..., None] * bb_i[d_][None] + pwi[..., None] * bb_r[d_][None]
        tr = lambda v: jnp.transpose(v, (1, 0, 3, 2)).reshape(G, T * H, P)
        return tr(er), tr(ei)
    ef_r, ef_i = cmul_pb(jnp.flip(pw_r[:T, 0], 0), jnp.flip(pw_i[:T, 0], 0), 0)
    eb_r, eb_i = cmul_pb(pw_r[:T, 1], pw_i[:T, 1], 1)
    e_mat = jnp.concatenate([ef_r, eb_r, ef_i, eb_i], axis=-1)
    def carry(car, cai):
        rr = jnp.transpose(car, (1, 3, 0, 2)).reshape(G, P, T * H)
        ri = jnp.transpose(-cai, (1, 3, 0, 2)).reshape(G, P, T * H)
        return rr, ri
    cf_r, cf_i = carry(ca_r[1:T + 1, 0], ca_i[1:T + 1, 0])
    cb_r, cb_i = carry(jnp.flip(ca_r[1:T + 1, 1], 0), jnp.flip(ca_i[1:T + 1, 1], 0))
    zp = jnp.zeros_like(cf_r)
    wf = jnp.concatenate([cf_r, zp, cf_i, zp], axis=1)
    wb = jnp.concatenate([zp, cb_r, zp, cb_i], axis=1)
    w_out = jnp.concatenate([m_mat, wf, wb], axis=1)
    at_r = jnp.concatenate([pw_r[T, 0], pw_r[T, 1]], axis=-1)[:, None, :]
    at_i = jnp.concatenate([pw_i[T, 0], pw_i[T, 1]], axis=-1)[:, None, :]
    d_t = jnp.tile(d.astype(f32).reshape(G, 1, H), (1, 1, T))
    return e_mat.astype(BF16), w_out.astype(BF16), at_r, at_i, d_t


def _s5_kernel(x_ref, e_ref, w_ref, ar_ref, ai_ref, d_ref, init_ref, y_ref, fin_ref,
               e_scr, sa_scr, sb_scr, *, gb, nb, nc, p2):
    nt = nc // 2
    lane = lax.broadcasted_iota(jnp.int32, (2 * nb, p2), 1)
    is_fwd = lane < (p2 // 2)
    first = lax.broadcasted_iota(jnp.int32, (2 * nb, p2), 0) < nb
    swap = lambda v: pltpu.roll(v, nb, axis=0)
    for g in range(gb):
        e_scr[g] = _dot(x_ref[g], e_ref[g])

    def step(k, carry):
        out = []
        rf = pl.ds(pl.multiple_of(k * 2 * nb, 2 * nb), 2 * nb)
        rb = pl.ds(pl.multiple_of((nt - 1 - k) * 2 * nb, 2 * nb), 2 * nb)
        for g in range(gb):
            cr, ci = carry[2 * g], carry[2 * g + 1]
            ar, ai = ar_ref[g], ai_ref[g]
            er = jnp.where(is_fwd, e_scr[g, rf, 0:p2], swap(e_scr[g, rb, 0:p2]))
            ei = jnp.where(is_fwd, e_scr[g, rf, p2:2 * p2], swap(e_scr[g, rb, p2:2 * p2]))
            ur = ar * cr - ai * ci + er
            ui = ar * ci + ai * cr + ei
            ur4, ui4 = swap(ur), swap(ui)
            sr = jnp.where(first, cr, ur4)
            si = jnp.where(first, ci, ui4)
            sa_scr[g, rf, 0:p2] = sr
            sa_scr[g, rf, p2:2 * p2] = si
            sb_scr[g, rb, 0:p2] = swap(sr)
            sb_scr[g, rb, p2:2 * p2] = swap(si)
            xr = jnp.where(first, ur, ur4)
            xi = jnp.where(first, ui, ui4)
            zr = ar * xr - ai * xi + er
            zi = ar * xi + ai * xr + ei
            out.append(jnp.where(first, swap(zr), zr))
            out.append(jnp.where(first, swap(zi), zi))
        return tuple(out)

    init = []
    for g in range(gb):
        init += [init_ref[g, :, 0:p2], init_ref[g, :, p2:2 * p2]]
    fin = lax.fori_loop(0, nt, step, tuple(init))
    for g in range(gb):
        fin_ref[g] = jnp.concatenate([fin[2 * g], fin[2 * g + 1]], axis=1)
        x = x_ref[g]
        lhs = jnp.concatenate([x, sa_scr[g].astype(BF16), sb_scr[g].astype(BF16)], axis=1)
        y = _dot(lhs, w_ref[g]) + x.astype(F32) * d_ref[g]
        y_ref[g] = jax.nn.gelu(y).astype(BF16)


def _s5_scan(xg, mats, init, nb):
    e_mat, w_out, at_r, at_i, d_t = mats
    G, R, th = xg.shape
    p4 = e_mat.shape[-1]
    p2 = p4 // 2
    gb = 2
    nc = R // nb
    assert 2 * nb == 8 and nc % 2 == 0, "two chunks of batch rows must fill one 8-sublane tile"
    kern = functools.partial(_s5_kernel, gb=gb, nb=nb, nc=nc, p2=p2)
    g3 = lambda i: (i, 0, 0)
    return pl.pallas_call(
        kern,
        grid=(G // gb,),
        in_specs=[pl.BlockSpec((gb, R, th), g3),
                  pl.BlockSpec((gb, th, p4), g3),
                  pl.BlockSpec((gb, th + 2 * p4, th), g3),
                  pl.BlockSpec((gb, 1, p2), g3),
                  pl.BlockSpec((gb, 1, p2), g3),
                  pl.BlockSpec((gb, 1, th), g3),
                  pl.BlockSpec((gb, 2 * nb, p4), g3)],
        out_specs=[pl.BlockSpec((gb, R, th), g3),
                   pl.BlockSpec((gb, 2 * nb, p4), g3)],
        out_shape=[jax.ShapeDtypeStruct((G, R, th), BF16),
                   jax.ShapeDtypeStruct((G, 2 * nb, p4), F32)],
        scratch_shapes=[pltpu.VMEM((gb, R, p4), F32),
                        pltpu.VMEM((gb, R, p4), F32),
                        pltpu.VMEM((gb, R, p4), F32)],
        compiler_params=_cparams("parallel"),
        name="s5_scan",
    )(xg, e_mat, w_out, at_r, at_i, d_t, init)


def _lane_group(rows, h):
    return lax.broadcasted_iota(jnp.int32, (rows, LANES), 1) // h


def _s5_pack_kernel(u_ref, o_ref, *, h, rows):
    gl = LANES // h
    grp = _lane_group(rows, h)
    for g in range(gl):
        for half in range(S5_CHUNK // gl):
            acc = None
            for jj in range(gl):
                v = u_ref[:, :, half * gl + jj, :].astype(F32).reshape(rows, LANES)
                r = pltpu.roll(v, ((jj - g) * h) % LANES, axis=1)
                acc = r if acc is None else jnp.where(grp == jj, r, acc)
            o_ref[g, :, half * LANES:(half + 1) * LANES] = acc.astype(BF16)


def _s5_unpack_kernel(y_ref, o_ref, *, h, rows, cc, nb):
    gl = LANES // h
    grp = _lane_group(rows, h)
    for half in range(S5_CHUNK // gl):
        ys = [y_ref[g, :, half * LANES:(half + 1) * LANES].astype(F32) for g in range(gl)]
        for tt in range(gl):
            acc = None
            for g in range(gl):
                r = pltpu.roll(ys[g], ((g - tt) * h) % LANES, axis=1)
                acc = r if acc is None else jnp.where(grp == g, r, acc)
            o_ref[:, :, half * gl + tt, :] = acc.reshape(cc, nb, LANES).astype(BF16)


def _s5_pack(u4, h):
    lc, nb, t, w = u4.shape
    gl = LANES // h
    cc = min(lc, 64)
    rows = cc * nb
    kern = functools.partial(_s5_pack_kernel, h=h, rows=rows)
    return pl.pallas_call(
        kern,
        grid=(w // LANES, lc // cc),
        in_specs=[pl.BlockSpec((cc, nb, t, LANES), lambda l, i: (i, 0, 0, l))],
        out_specs=pl.BlockSpec((gl, rows, t * h), lambda l, i: (l, i, 0)),
        out_shape=jax.ShapeDtypeStruct((w // h, lc * nb, t * h), BF16),
        compiler_params=_cparams("parallel", "parallel"),
        name="s5_pack",
    )(u4)


def _s5_unpack(yg, nb, h):
    G, R, th = yg.shape
    t = th // h
    lc = R // nb
    gl = LANES // h
    cc = min(lc, 64)
    rows = cc * nb
    kern = functools.partial(_s5_unpack_kernel, h=h, rows=rows, cc=cc, nb=nb)
    return pl.pallas_call(
        kern,
        grid=(G // gl, lc // cc),
        in_specs=[pl.BlockSpec((gl, rows, th), lambda l, i: (l, i, 0))],
        out_specs=pl.BlockSpec((cc, nb, t, LANES), lambda l, i: (i, 0, 0, l)),
        out_shape=jax.ShapeDtypeStruct((lc, nb, t, G * h), BF16),
        compiler_params=_cparams("parallel", "parallel"),
        name="s5_unpack",
    )(yg)


def _s5_mixer(u4, mats, init, h):
    nb = u4.shape[1]
    yg, fin = _s5_scan(_s5_pack(u4, h), mats, init, nb)
    return _s5_unpack(yg, nb, h), fin


def _outproj_kernel(yh_ref, ys_ref, x_ref, mod_ref, g_ref, wg_ref, bg_ref, wo_ref,
                    xo_ref, hx_ref, *, d, dh, tm):
    gate = mod_ref[0, :, 2 * d:3 * d]
    shift = mod_ref[0, :, 3 * d:4 * d]
    scale = mod_ref[0, :, 4 * d:5 * d]
    ys = ys_ref[:, 0, :, :].reshape(tm, ys_ref.shape[-1])
    glu = ys.astype(F32) * jax.nn.sigmoid(_dot(ys, wg_ref[0]) + bg_ref[0])
    yx = _dot(yh_ref[0], wo_ref[0, 0:dh, :]) + _dot(glu.astype(BF16), wo_ref[0, dh:, :])
    xo = x_ref[0] + gate * _rms(yx, g_ref[0, 1:2, :])
    xo_ref[0] = xo
    hx_ref[0] = (_rms(xo, g_ref[0, 2:3, :]) * (1.0 + scale) + shift).astype(BF16)


def _outproj(yh, ys4, x, mod, mod_row, layer, norm_g, w_glu, b_glu, w_out, tm):
    b, s, d = x.shape
    dh = yh.shape[-1]
    ds5 = ys4.shape[-1]
    tc = tm // S5_CHUNK
    kern = functools.partial(_outproj_kernel, d=d, dh=dh, tm=tm)
    lyr = lambda bi, i: (layer, 0, 0)
    t3 = lambda bi, i: (bi, i, 0)
    return pl.pallas_call(
        kern,
        grid=(b, s // tm),
        in_specs=[pl.BlockSpec((1, tm, dh), t3),
                  pl.BlockSpec((tc, 1, S5_CHUNK, ds5), lambda bi, i: (i, bi, 0, 0)),
                  pl.BlockSpec((1, tm, d), t3),
                  pl.BlockSpec((1, 1, mod.shape[-1]), lambda bi, i: (mod_row(bi), 0, 0)),
                  pl.BlockSpec((1,) + norm_g.shape[1:], lyr),
                  pl.BlockSpec((1, ds5, ds5), lyr), pl.BlockSpec((1, 1, ds5), lyr),
                  pl.BlockSpec((1, dh + ds5, d), lyr)],
        out_specs=[pl.BlockSpec((1, tm, d), t3), pl.BlockSpec((1, tm, d), t3)],
        out_shape=[jax.ShapeDtypeStruct((b, s, d), F32), jax.ShapeDtypeStruct((b, s, d), BF16)],
        compiler_params=_cparams("parallel", "parallel"),
        name="outproj",
    )(yh, ys4, x, mod, norm_g, w_glu, b_glu, w_out)


def _ffn_kernel(hm_ref, hp_ref, hn_ref, x_ref, mod_ref, g_ref, wg_ref, wv_ref, cw_ref, cb_ref, wd_ref,
                o_ref, acc_ref, *, tm, d, wg, vertical):
    i = pl.program_id(1)
    nt = pl.num_programs(1)
    j = pl.program_id(2)
    nj = pl.num_programs(2)
    hm = hm_ref[0]
    cw = cw_ref[0]
    if vertical:
        top = jnp.where(i > 0, 1.0, 0.0).astype(BF16)
        bot = jnp.where(i < nt - 1, 1.0, 0.0).astype(BF16)
        ha = jnp.concatenate([hp_ref[0] * top, hm, hn_ref[0] * bot], axis=0)
        rows = tm + 2 * wg
    else:
        ha = hm
        rows = tm
    g = _dot(ha, wg_ref[0])
    col = lax.broadcasted_iota(jnp.int32, g.shape, 0) & (wg - 1)
    gl = jnp.where(col > 0, pltpu.roll(g, 1, axis=0), 0.0)
    gr = jnp.where(col < wg - 1, pltpu.roll(g, rows - 1, axis=0), 0.0)

    def hrow(dy):
        return gl * cw[3 * dy:3 * dy + 1] + g * cw[3 * dy + 1:3 * dy + 2] + gr * cw[3 * dy + 2:3 * dy + 3]

    if vertical:
        up = hrow(0)
        dn = hrow(2)
        conv = hrow(1)[wg:wg + tm] + up[0:tm] + dn[2 * wg:2 * wg + tm]
    else:
        conv = hrow(1)
    conv = conv + cb_ref[0]
    v = _dot(hm, wv_ref[0])
    hmid = (jax.nn.gelu(conv) * v).astype(BF16)
    part = _dot(hmid, wd_ref[0])

    @pl.when(j == 0)
    def _():
        acc_ref[...] = part

    @pl.when(j > 0)
    def _():
        acc_ref[...] += part

    @pl.when(j == nj - 1)
    def _():
        gate = mod_ref[0, :, 5 * d:6 * d]
        o_ref[0] = x_ref[0] + gate * _rms(acc_ref[...], g_ref[0, 3:4, :])


def _ffn(hx, x, mod, mod_row, layer, norm_g, w_up, conv_w, conv_b, w_down, tm, wg, vertical):
    b, s, d = x.shape
    f = w_down.shape[1]
    fc = f // 2 if (f // 2) % LANES == 0 else f
    nf = f // fc
    nt = s // tm
    r = tm // wg if vertical else 1
    hb = wg if vertical else 16
    nhb = s // hb
    kern = functools.partial(_ffn_kernel, tm=tm, d=d, wg=wg, vertical=vertical)
    t3 = lambda bi, i, j: (bi, i, 0)
    return pl.pallas_call(
        kern,
        grid=(b, nt, nf),
        in_specs=[pl.BlockSpec((1, tm, d), t3),
                  pl.BlockSpec((1, hb, d), lambda bi, i, j: (bi, jnp.maximum(i * r - 1, 0), 0)),
                  pl.BlockSpec((1, hb, d), lambda bi, i, j: (bi, jnp.minimum((i + 1) * r, nhb - 1), 0)),
                  pl.BlockSpec((1, tm, d), t3),
                  pl.BlockSpec((1, 1, mod.shape[-1]), lambda bi, i, j: (mod_row(bi), 0, 0)),
                  pl.BlockSpec((1,) + norm_g.shape[1:], lambda bi, i, j: (layer, 0, 0)),
                  pl.BlockSpec((1, d, fc), lambda bi, i, j: (layer, 0, j)),
                  pl.BlockSpec((1, d, fc), lambda bi, i, j: (layer, 0, nf + j)),
                  pl.BlockSpec((1, 9, fc), lambda bi, i, j: (layer, 0, j)),
                  pl.BlockSpec((1, 1, fc), lambda bi, i, j: (layer, 0, j)),
                  pl.BlockSpec((1, fc, d), lambda bi, i, j: (layer, j, 0))],
        out_specs=pl.BlockSpec((1, tm, d), t3),
        out_shape=jax.ShapeDtypeStruct((b, s, d), F32),
        scratch_shapes=[pltpu.VMEM((tm, d), F32)],
        compiler_params=_cparams("parallel", "parallel", "arbitrary"),
        name="conv_glu_ffn",
    )(hx, hx, hx, x, mod, norm_g, w_up, w_up, conv_w, conv_b, w_down)


def kernel(x, c, ctx, c_ctx, w_ada, b_ada, norm_g, w_in, hy_short_w, hy_short_b,
           filt_w_in, filt_b_in, filt_w_hid, filt_b_hid, filt_freq, filt_w_out, hy_bias,
           s5_lam_re, s5_lam_im, s5_log_step, s5_b_re, s5_b_im, s5_c_re, s5_c_im, s5_d,
           s5_w_glu, s5_b_glu, w_out, ffn_w_up, ffn_conv_w, ffn_conv_b, ffn_w_down):
    depth = w_ada.shape[0]
    bsz, seq, d = x.shape
    lctx = ctx.shape[1]
    dh = hy_bias.shape[-1]
    G, P, H = s5_b_re.shape[2], s5_b_re.shape[3], s5_b_re.shape[4]
    dff = ffn_w_down.shape[1]
    assert bsz % 2 == 0 and bsz <= 4 and seq % GRID_W == 0 and GRID_W & (GRID_W - 1) == 0

    cond = jnp.zeros((8, d), F32).at[:bsz].set(c).at[bsz].set(c_ctx)
    mod = _ada_mod(cond, w_ada, b_ada).reshape(depth * 8, 1, 6 * d)

    filt_args = (filt_w_in, filt_b_in, filt_w_hid, filt_b_hid, filt_freq, filt_w_out)
    long_conv = _LongConv(seq, dh, _hyena_filters(seq, *filt_args, dh))
    k_ctx = _hyena_filters(lctx, *filt_args, dh)
    short_c = tuple(jnp.asarray(a, BF16) for a in _short_consts(lctx))

    w_in_b = w_in.astype(BF16)
    w_glu_b = s5_w_glu.astype(BF16)
    w_out_b = w_out.astype(BF16)
    w_up_b = ffn_w_up.astype(BF16)
    w_down_b = ffn_w_down.astype(BF16)
    sb = hy_short_b.reshape(depth, 1, 3 * dh)
    cw = ffn_conv_w.reshape(depth, 9, dff)
    cb = ffn_conv_b.reshape(depth, 1, dff)
    bg = s5_b_glu.reshape(depth, 1, -1)
    hb = hy_bias.reshape(depth, 1, dh)

    tm = min(seq, ROW_TILE)
    for l in range(depth):
        last = l == depth - 1
        row_x = lambda bi, l=l: 8 * l + bi
        row_c = lambda bi, l=l: 8 * l + bsz
        mats = _s5_matrices(s5_lam_re[l], s5_lam_im[l], s5_log_step[l], s5_b_re[l], s5_b_im[l],
                            s5_c_re[l], s5_c_im[l], s5_d[l])

        x0c, zc, uc = _inproj(ctx, mod, row_c, l, norm_g, w_in_b, hy_short_w, sb, dh, lctx)
        ysc, ctx_state = _s5_mixer(uc, mats, jnp.zeros((G, 2 * bsz, 4 * P), F32), H)

        x0, z, u = _inproj(x, mod, row_x, l, norm_g, w_in_b, hy_short_w, sb, dh, tm)
        ys, _ = _s5_mixer(u, mats, ctx_state, H)
        yh = long_conv(z, x0, hb, l)
        x, hx = _outproj(yh, ys, x, mod, row_x, l, norm_g, w_glu_b, bg, w_out_b, tm)
        x = _ffn(hx, x, mod, row_x, l, norm_g, w_up_b, cw, cb, w_down_b, tm, GRID_W, True)

        if not last:
            yhc = _short_conv(zc, x0c, k_ctx, l, short_c, hb)
            ctx, hc = _outproj(yhc, ysc, ctx, mod, row_c, l, norm_g, w_glu_b, bg, w_out_b, lctx)
            ctx = _ffn(hc, ctx, mod, row_c, l, norm_g, w_up_b, cw, cb, w_down_b, lctx, lctx, False)
    return x
```

```python
import functools
import math

import numpy as np
import jax
import jax.numpy as jnp
from jax import lax
from jax.experimental import pallas as pl
from jax.experimental.pallas import tpu as pltpu

GRID_W = 64
RMS_EPS = 1e-6
DECAY_TARGET = 1e-2
FAST_DECAY_PCT = 0.3
SLOW_DECAY_PCT = 1.5
S5_CHUNK = 16
LANES = 128
HALO = 16
ROW_TILE = 512
VMEM_LIMIT = 56 * 1024 * 1024

F32 = jnp.float32
BF16 = jnp.bfloat16
HIGHEST = lax.Precision.HIGHEST


def _cparams(*sem):
    return pltpu.CompilerParams(dimension_semantics=sem, vmem_limit_bytes=VMEM_LIMIT)


def _dot(a, b, **kw):
    return jnp.dot(a, b, preferred_element_type=F32, **kw)


def _ada_kernel(cond_ref, w_ref, b_ref, o_ref):
    cv = cond_ref[...]
    s = cv * jax.nn.sigmoid(cv)
    o_ref[0] = _dot(s, w_ref[0], precision=HIGHEST) + b_ref[0]


def _ada_mod(cond, w_ada, b_ada):
    depth, d, n = w_ada.shape
    tn = n // 4
    return pl.pallas_call(
        _ada_kernel,
        grid=(depth, n // tn),
        in_specs=[pl.BlockSpec((8, d), lambda l, j: (0, 0)),
                  pl.BlockSpec((1, d, tn), lambda l, j: (l, 0, j)),
                  pl.BlockSpec((1, 1, tn), lambda l, j: (l, 0, j))],
        out_specs=pl.BlockSpec((1, 8, tn), lambda l, j: (l, 0, j)),
        out_shape=jax.ShapeDtypeStruct((depth, 8, n), F32),
        compiler_params=_cparams("parallel", "parallel"),
        name="ada_mod",
    )(cond, w_ada, b_ada.reshape(depth, 1, n))


def _rms(v, g):
    ms = jnp.mean(v * v, axis=-1, keepdims=True)
    return v * lax.rsqrt(ms + RMS_EPS) * g


def _inproj_kernel(xm_ref, xp_ref, xn_ref, mod_ref, g_ref, w_ref, sw_ref, sb_ref,
                   x0_ref, z_ref, u_ref, p_scr, *, tm, d, dh):
    i = pl.program_id(1)
    nt = pl.num_programs(1)
    shift = mod_ref[0, :, 0:d]
    scale = mod_ref[0, :, d:2 * d]
    xa = jnp.concatenate([xp_ref[0], xm_ref[0], xn_ref[0]], axis=0)
    xn = (_rms(xa, g_ref[0, 0:1, :]) * (1.0 + scale) + shift).astype(BF16)
    p = _dot(xn, w_ref[0])
    p_scr[...] = p[:, :3 * dh]

    @pl.when(i == 0)
    def _():
        p_scr[HALO - 8:HALO, :] = jnp.zeros((8, 3 * dh), F32)

    @pl.when(i == nt - 1)
    def _():
        p_scr[HALO + tm:HALO + tm + 8, :] = jnp.zeros((8, 3 * dh), F32)

    sw = sw_ref[0]
    conv = (p_scr[pl.ds(HALO - 1, tm), :] * sw[0:1] + p_scr[pl.ds(HALO, tm), :] * sw[1:2]
            + p_scr[pl.ds(HALO + 1, tm), :] * sw[2:3] + sb_ref[0])
    x0_ref[0] = conv[:, :dh].astype(BF16)
    z_ref[0] = (conv[:, dh:2 * dh] * conv[:, 2 * dh:]).astype(BF16)
    u = p[HALO:HALO + tm, 3 * dh:]
    u_ref[:, 0, :, :] = u.reshape(tm // S5_CHUNK, S5_CHUNK, u.shape[-1])


def _inproj(x, mod, mod_row, layer, norm_g, w_in, sw, sb, dh, tm):
    b, s, d = x.shape
    dp = w_in.shape[-1]
    ds5 = dp - 3 * dh
    nt = s // tm
    r = tm // HALO
    nh = s // HALO
    tc = tm // S5_CHUNK
    kern = functools.partial(_inproj_kernel, tm=tm, d=d, dh=dh)
    lyr = lambda bi, i: (layer, 0, 0)
    return pl.pallas_call(
        kern,
        grid=(b, nt),
        in_specs=[pl.BlockSpec((1, tm, d), lambda bi, i: (bi, i, 0)),
                  pl.BlockSpec((1, HALO, d), lambda bi, i: (bi, jnp.maximum(i * r - 1, 0), 0)),
                  pl.BlockSpec((1, HALO, d), lambda bi, i: (bi, jnp.minimum((i + 1) * r, nh - 1), 0)),
                  pl.BlockSpec((1, 1, mod.shape[-1]), lambda bi, i: (mod_row(bi), 0, 0)),
                  pl.BlockSpec((1,) + norm_g.shape[1:], lyr),
                  pl.BlockSpec((1, d, dp), lyr),
                  pl.BlockSpec((1, 3, 3 * dh), lyr),
                  pl.BlockSpec((1, 1, 3 * dh), lyr)],
        out_specs=[pl.BlockSpec((1, tm, dh), lambda bi, i: (bi, i, 0)),
                   pl.BlockSpec((1, tm, dh), lambda bi, i: (bi, i, 0)),
                   pl.BlockSpec((tc, 1, S5_CHUNK, ds5), lambda bi, i: (i, bi, 0, 0))],
        out_shape=[jax.ShapeDtypeStruct((b, s, dh), BF16),
                   jax.ShapeDtypeStruct((b, s, dh), BF16),
                   jax.ShapeDtypeStruct((s // S5_CHUNK, b, S5_CHUNK, ds5), F32)],
        scratch_shapes=[pltpu.VMEM((tm + 2 * HALO, 3 * dh), F32)],
        compiler_params=_cparams("parallel", "arbitrary"),
        name="inproj",
    )(x, x, x, mod, norm_g, w_in, sw, sb)


def _filter_feats(L, emb):
    bands = (emb - 1) // 2
    t = np.linspace(0.0, 1.0, L, dtype=np.float32).astype(np.float64)[:, None]
    w = (2.0 * math.pi / L) * np.arange(L, dtype=np.float64)[:, None]
    f = np.linspace(1e-4, bands - 1, bands, dtype=np.float32).astype(np.float64)[None, :]
    z = np.concatenate([t, np.cos(f * w), -np.sin(f * w)], axis=-1)
    zp = np.zeros((L, LANES), np.float32)
    zp[:, :emb] = z
    return zp


def _filter_kernel(z_ref, win_ref, bin_ref, whid_ref, bhid_ref, fr_ref, wout_ref, dl_ref, o_ref, *, L, tl):
    z = z_ref[...]
    fr = fr_ref[0]
    h = jnp.sin(fr * (_dot(z, win_ref[0], precision=HIGHEST) + bin_ref[0]))
    for i in range(whid_ref.shape[1]):
        h = jnp.sin(fr * (_dot(h, whid_ref[0, i], precision=HIGHEST) + bhid_ref[0, i]))
    h = _dot(h, wout_ref[0], precision=HIGHEST)
    dec = jnp.exp(-z[:, 0:1] * dl_ref[...])
    pos = pl.program_id(1) * tl + lax.broadcasted_iota(jnp.int32, (tl, 1), 0)
    o_ref[0] = jnp.where(pos == L, 0.0, h * dec)


def _hyena_filters(L, f_w_in, f_b_in, f_w_hid, f_b_hid, f_freq, f_w_out, dh):
    depth, emb, hid = f_w_in.shape
    n_inner = f_w_hid.shape[1]
    tl = min(L, 1024)
    nl = L // tl
    z1 = _filter_feats(L, emb)
    z = jnp.asarray(np.concatenate([z1, z1[:1], z1[:0:-1]], axis=0))
    win = jnp.zeros((depth, LANES, hid), F32).at[:, :emb].set(f_w_in)
    deltas = np.abs(np.linspace(math.log(DECAY_TARGET) / FAST_DECAY_PCT,
                                math.log(DECAY_TARGET) / SLOW_DECAY_PCT, dh, dtype=np.float32))[None, :]
    kern = functools.partial(_filter_kernel, L=L, tl=tl)
    return pl.pallas_call(
        kern,
        grid=(depth, 2 * nl),
        in_specs=[pl.BlockSpec((tl, LANES), lambda l, i: (i, 0)),
                  pl.BlockSpec((1, LANES, hid), lambda l, i: (l, 0, 0)),
                  pl.BlockSpec((1, 1, hid), lambda l, i: (l, 0, 0)),
                  pl.BlockSpec((1, n_inner, hid, hid), lambda l, i: (l, 0, 0, 0)),
                  pl.BlockSpec((1, n_inner, 1, hid), lambda l, i: (l, 0, 0, 0)),
                  pl.BlockSpec((1, 1, hid), lambda l, i: (l, 0, 0)),
                  pl.BlockSpec((1, hid, dh), lambda l, i: (l, 0, jnp.where(i >= nl, 1, 0))),
                  pl.BlockSpec((1, dh), lambda l, i: (0, 0))],
        out_specs=pl.BlockSpec((1, tl, dh), lambda l, i: (l, i, 0)),
        out_shape=jax.ShapeDtypeStruct((depth, 2 * L, dh), F32),
        compiler_params=_cparams("parallel", "parallel"),
        name="hyena_filter",
    )(z, win, f_b_in.reshape(depth, 1, hid), f_w_hid, f_b_hid.reshape(depth, n_inner, 1, hid),
      f_freq.reshape(depth, 1, hid), f_w_out, jnp.asarray(deltas))


def _dft_consts(n1, n2):
    n = n1 * n2
    n2h = n2 // 2
    k2 = np.arange(n2)[:, None]
    a = -2.0 * np.pi * k2 * np.arange(n2)[None, :] / n2
    fr, fi = np.cos(a), np.sin(a)
    f1 = np.block([[fr[:, :n2h], -fi[:, :n2h]], [fi[:, :n2h], fr[:, :n2h]]])
    f1k = np.concatenate([fr, fi], axis=0)
    cr, ci = fr[:n2h] / n, -fi[:n2h] / n
    f3 = np.block([[cr, -ci], [ci, cr]])
    b = -2.0 * np.pi * np.arange(n1)[:, None] * np.arange(n1)[None, :] / n1
    t = -2.0 * np.pi * np.arange(n2)[:, None] * np.arange(n1)[None, :] / n
    return dict(f1=f1.astype(np.float32), f1k=f1k.astype(np.float32), f3=f3.astype(np.float32),
                gr=np.cos(b).astype(np.float32), gi=np.sin(b).astype(np.float32),
                tr=np.cos(t).astype(np.float32)[:, None, :], ti=np.sin(t).astype(np.float32)[:, None, :])


def _tables_kernel(gr_ref, gi_ref, tr_ref, ti_ref, m1_ref, m2_ref, *, kb):
    gr, gi = gr_ref[...], gi_ref[...]
    for q in range(kb):
        tr, ti = tr_ref[q], ti_ref[q]
        re = gr * tr - gi * ti
        im = gr * ti + gi * tr
        m1_ref[q] = jnp.concatenate([jnp.concatenate([re, -im], axis=1),
                                     jnp.concatenate([im, re], axis=1)], axis=0).astype(BF16)
        ret, imt = re.T, im.T
        m2_ref[q] = jnp.concatenate([jnp.concatenate([ret, imt], axis=1),
                                     jnp.concatenate([-imt, ret], axis=1)], axis=0).astype(BF16)


def _dft_tables(c, n1, n2):
    kb = 8
    kern = functools.partial(_tables_kernel, kb=kb)
    shp = jax.ShapeDtypeStruct((n2, 2 * n1, 2 * n1), BF16)
    return pl.pallas_call(
        kern,
        grid=(n2 // kb,),
        in_specs=[pl.BlockSpec((n1, n1), lambda i: (0, 0)),
                  pl.BlockSpec((n1, n1), lambda i: (0, 0)),
                  pl.BlockSpec((kb, 1, n1), lambda i: (i, 0, 0)),
                  pl.BlockSpec((kb, 1, n1), lambda i: (i, 0, 0))],
        out_specs=[pl.BlockSpec((kb, 2 * n1, 2 * n1), lambda i: (i, 0, 0)),
                   pl.BlockSpec((kb, 2 * n1, 2 * n1), lambda i: (i, 0, 0))],
        out_shape=[shp, shp],
        compiler_params=_cparams("parallel"),
        name="dft_tables",
    )(jnp.asarray(c["gr"]), jnp.asarray(c["gi"]), jnp.asarray(c["tr"]), jnp.asarray(c["ti"]))


def _swap_major(v):
    return pltpu.einshape("abc->bac", v)


def _s1_kernel(z_ref, f_ref, o_ref, y_scr, *, n2, nb1):
    zt = [_swap_major(z_ref[s].astype(F32)) for s in range(2)]
    for j in range(nb1):
        x = jnp.concatenate([zt[0][j], zt[1][j]], axis=0).astype(BF16)
        y = _dot(f_ref[...], x)
        y_scr[0, j] = y[:n2]
        y_scr[1, j] = y[n2:]
    for s in range(2):
        o_ref[0, s] = _swap_major(y_scr[s])


def _s1k_kernel(k_ref, f_ref, o_ref, y_scr, *, n2, nb1):
    kt = _swap_major(k_ref[0])
    for j in range(nb1):
        y = _dot(f_ref[...], kt[j].astype(BF16))
        y_scr[0, j] = y[:n2]
        y_scr[1, j] = y[n2:]
    for s in range(2):
        o_ref[0, s] = _swap_major(y_scr[s])


def _dft_stage1(z4, f1, n2, nb1):
    b, n2h, n1, c = z4.shape
    kern = functools.partial(_s1_kernel, n2=n2, nb1=nb1)
    return pl.pallas_call(
        kern,
        grid=(b // 2, n1 // nb1),
        in_specs=[pl.BlockSpec((2, n2h, nb1, c), lambda p, j: (p, 0, j, 0)),
                  pl.BlockSpec((2 * n2, 2 * n2h), lambda p, j: (0, 0))],
        out_specs=pl.BlockSpec((1, 2, n2, nb1, c), lambda p, j: (p, 0, 0, j, 0)),
        out_shape=jax.ShapeDtypeStruct((b // 2, 2, n2, n1, c), F32),
        scratch_shapes=[pltpu.VMEM((2, nb1, n2, c), F32)],
        compiler_params=_cparams("parallel", "parallel"),
        name="dft_stage1",
    )(z4, f1)


def _dft_stage1_filter(k4, f1k, nb1):
    depth, n2, n1, c = k4.shape
    kern = functools.partial(_s1k_kernel, n2=n2, nb1=nb1)
    return pl.pallas_call(
        kern,
        grid=(depth, n1 // nb1),
        in_specs=[pl.BlockSpec((1, n2, nb1, c), lambda l, j: (l, 0, j, 0)),
                  pl.BlockSpec((2 * n2, n2), lambda l, j: (0, 0))],
        out_specs=pl.BlockSpec((1, 2, n2, nb1, c), lambda l, j: (l, 0, 0, j, 0)),
        out_shape=jax.ShapeDtypeStruct((depth, 2, n2, n1, c), F32),
        scratch_shapes=[pltpu.VMEM((2, nb1, n2, c), F32)],
        compiler_params=_cparams("parallel", "parallel"),
        name="dft_stage1_filter",
    )(k4, f1k)


def _s2k_kernel(b_ref, m1_ref, o_ref, *, kb, n1):
    for q in range(kb):
        rows = slice(q * n1, (q + 1) * n1)
        xin = jnp.concatenate([b_ref[0, 0, rows, :], b_ref[0, 1, rows, :]], axis=0).astype(BF16)
        xf = _dot(m1_ref[q], xin)
        o_ref[0, 0, rows, :] = xf[:n1]
        o_ref[0, 1, rows, :] = xf[n1:]


def _dft_stage2_filter(bv, m1, n1, kb):
    depth, _, n, c = bv.shape
    kern = functools.partial(_s2k_kernel, kb=kb, n1=n1)
    return pl.pallas_call(
        kern,
        grid=(n // (kb * n1), depth),
        in_specs=[pl.BlockSpec((1, 2, kb * n1, c), lambda k, l: (l, 0, k, 0)),
                  pl.BlockSpec((kb, 2 * n1, 2 * n1), lambda k, l: (k, 0, 0))],
        out_specs=pl.BlockSpec((1, 2, kb * n1, c), lambda k, l: (l, 0, k, 0)),
        out_shape=jax.ShapeDtypeStruct((depth, 2, n, c), F32),
        compiler_params=_cparams("parallel", "parallel"),
        name="dft_stage2_filter",
    )(bv, m1)


def _s2_kernel(b_ref, m1_ref, kf_ref, m2_ref, o_ref, *, kb, n1):
    for q in range(kb):
        rows = slice(q * n1, (q + 1) * n1)
        xin = jnp.concatenate([b_ref[0, 0, rows, :], b_ref[0, 1, rows, :]], axis=0).astype(BF16)
        xf = _dot(m1_ref[q], xin)
        xr, xi = xf[:n1], xf[n1:]
        kr, ki = kf_ref[0, 0, rows, :], kf_ref[0, 1, rows, :]
        yin = jnp.concatenate([xr * kr - xi * ki, xr * ki + xi * kr], axis=0).astype(BF16)
        g = _dot(m2_ref[q], yin)
        o_ref[0, 0, rows, :] = g[:n1]
        o_ref[0, 1, rows, :] = g[n1:]


def _dft_stage2(bv, m1, kf, layer, m2, n1, kb):
    p, _, n, c = bv.shape
    kern = functools.partial(_s2_kernel, kb=kb, n1=n1)
    return pl.pallas_call(
        kern,
        grid=(n // (kb * n1), p),
        in_specs=[pl.BlockSpec((1, 2, kb * n1, c), lambda k, q: (q, 0, k, 0)),
                  pl.BlockSpec((kb, 2 * n1, 2 * n1), lambda k, q: (k, 0, 0)),
                  pl.BlockSpec((1, 2, kb * n1, c), lambda k, q: (layer, 0, k, 0)),
                  pl.BlockSpec((kb, 2 * n1, 2 * n1), lambda k, q: (k, 0, 0))],
        out_specs=pl.BlockSpec((1, 2, kb * n1, c), lambda k, q: (q, 0, k, 0)),
        out_shape=jax.ShapeDtypeStruct((p, 2, n, c), F32),
        compiler_params=_cparams("parallel", "arbitrary"),
        name="dft_stage2",
    )(bv, m1, kf, m2)


def _s3_kernel(g_ref, f_ref, z_ref, x0_ref, bias_ref, o_ref, y_scr, *, n2h, nb1):
    gt = [_swap_major(g_ref[0, s]) for s in range(2)]
    for j in range(nb1):
        gin = jnp.concatenate([gt[0][j], gt[1][j]], axis=0).astype(BF16)
        y = _dot(f_ref[...], gin)
        y_scr[0, j] = y[:n2h]
        y_scr[1, j] = y[n2h:]
    for s in range(2):
        yh = _swap_major(y_scr[s]) + z_ref[s].astype(F32) * bias_ref[0]
        o_ref[s] = (x0_ref[s].astype(F32) * yh).astype(BF16)


def _dft_stage3(g5, f3, z4, x04, bias, layer, nb1):
    p, _, n2, n1, c = g5.shape
    n2h = n2 // 2
    kern = functools.partial(_s3_kernel, n2h=n2h, nb1=nb1)
    return pl.pallas_call(
        kern,
        grid=(p, n1 // nb1),
        in_specs=[pl.BlockSpec((1, 2, n2, nb1, c), lambda q, j: (q, 0, 0, j, 0)),
                  pl.BlockSpec((2 * n2h, 2 * n2), lambda q, j: (0, 0)),
                  pl.BlockSpec((2, n2h, nb1, c), lambda q, j: (q, 0, j, 0)),
                  pl.BlockSpec((2, n2h, nb1, c), lambda q, j: (q, 0, j, 0)),
                  pl.BlockSpec((1, 1, c), lambda q, j: (layer, 0, 0))],
        out_specs=pl.BlockSpec((2, n2h, nb1, c), lambda q, j: (q, 0, j, 0)),
        out_shape=jax.ShapeDtypeStruct((2 * p, n2h, n1, c), BF16),
        scratch_shapes=[pltpu.VMEM((2, nb1, n2h, c), F32)],
        compiler_params=_cparams("parallel", "parallel"),
        name="dft_stage3",
    )(g5, f3, z4, x04, bias)


class _LongConv:
    def __init__(self, L, c, k2s):
        n = 2 * L
        n1 = 1 << (int(math.log2(n)) // 2)
        n2 = n // n1
        assert n1 * n2 == n and n1 == n2, "long-convolution path needs 2L to be a square power of two"
        self.L, self.c, self.n1, self.n2 = L, c, n1, n2
        self.nb1 = min(n1, 16)
        self.kb = 4
        cst = _dft_consts(n1, n2)
        self.f1 = jnp.asarray(cst["f1"], BF16)
        self.f3 = jnp.asarray(cst["f3"], BF16)
        self.m1, self.m2 = _dft_tables(cst, n1, n2)
        depth = k2s.shape[0]
        bk = _dft_stage1_filter(k2s.reshape(depth, n2, n1, c), jnp.asarray(cst["f1k"], BF16), self.nb1)
        self.kf = _dft_stage2_filter(bk.reshape(depth, 2, n, c), self.m1, n1, self.kb)

    def __call__(self, z, x0, bias, layer):
        b, L, c = z.shape
        n1, n2 = self.n1, self.n2
        z4 = z.reshape(b, n2 // 2, n1, c)
        x04 = x0.reshape(b, n2 // 2, n1, c)
        b5 = _dft_stage1(z4, self.f1, n2, self.nb1)
        gv = _dft_stage2(b5.reshape(b // 2, 2, n1 * n2, c), self.m1, self.kf, layer, self.m2, n1, self.kb)
        y = _dft_stage3(gv.reshape(b // 2, 2, n2, n1, c), self.f3, z4, x04, bias, layer, self.nb1)
        return y.reshape(b, L, c)


def _short_consts(L):
    n = 2 * L
    a = -2.0 * np.pi * np.arange(n)[:, None] * np.arange(n)[None, :] / n
    fr, fi = np.cos(a), np.sin(a)
    ff = np.block([[fr[:, :L], -fi[:, :L]], [fi[:, :L], fr[:, :L]]])
    fk = np.concatenate([fr, fi], axis=0)
    cr, ci = fr[:L] / n, -fi[:L] / n
    finv = np.block([[cr, -ci], [ci, cr]])
    return ff.astype(np.float32), fk.astype(np.float32), finv.astype(np.float32)


def _short_conv_kernel(z_ref, x0_ref, k_ref, ff_ref, fk_ref, fi_ref, bias_ref, o_ref, *, L):
    n = 2 * L
    x = jnp.concatenate([z_ref[0], z_ref[1]], axis=0)
    xf = _dot(ff_ref[...], x)
    kf = _dot(fk_ref[...], k_ref[0].astype(BF16))
    xr, xi, kr, ki = xf[:n], xf[n:], kf[:n], kf[n:]
    yin = jnp.concatenate([xr * kr - xi * ki, xr * ki + xi * kr], axis=0).astype(BF16)
    y = _dot(fi_ref[...], yin)
    for s in range(2):
        yh = y[s * L:(s + 1) * L] + z_ref[s].astype(F32) * bias_ref[0]
        o_ref[s] = (x0_ref[s].astype(F32) * yh).astype(BF16)


def _short_conv(z, x0, k2s, layer, consts, bias):
    b, L, c = z.shape
    n = 2 * L
    cb = min(c, 256)
    ff, fk, finv = consts
    kern = functools.partial(_short_conv_kernel, L=L)
    return pl.pallas_call(
        kern,
        grid=(b // 2, c // cb),
        in_specs=[pl.BlockSpec((2, L, cb), lambda p, j: (p, 0, j)),
                  pl.BlockSpec((2, L, cb), lambda p, j: (p, 0, j)),
                  pl.BlockSpec((1, n, cb), lambda p, j: (layer, 0, j)),
                  pl.BlockSpec((2 * n, 2 * L), lambda p, j: (0, 0)),
                  pl.BlockSpec((2 * n, n), lambda p, j: (0, 0)),
                  pl.BlockSpec((2 * L, 2 * n), lambda p, j: (0, 0)),
                  pl.BlockSpec((1, 1, cb), lambda p, j: (layer, 0, j))],
        out_specs=pl.BlockSpec((2, L, cb), lambda p, j: (p, 0, j)),
        out_shape=jax.ShapeDtypeStruct((b, L, c), BF16),
        compiler_params=_cparams("parallel", "parallel"),
        name="short_conv",
    )(z, x0, k2s, ff, fk, finv, bias)


def _s5_matrices(lam_re, lam_im, log_step, b_re, b_im, c_re, c_im, d):
    f32 = F32
    T = S5_CHUNK
    lr, li = lam_re.astype(f32), lam_im.astype(f32)
    dt = jnp.exp(log_step.astype(f32))[..., None]
    mag = jnp.exp(lr * dt)
    a_r, a_i = mag * jnp.cos(li * dt), mag * jnp.sin(li * dt)
    den = lr * lr + li * li
    q_r = ((a_r - 1.0) * lr + a_i * li) / den
    q_i = (a_i * lr - (a_r - 1.0) * li) / den
    br, bi = b_re.astype(f32), b_im.astype(f32)
    bb_r = q_r[..., None] * br - q_i[..., None] * bi
    bb_i = q_r[..., None] * bi + q_i[..., None] * br
    pr, pi = [jnp.ones_like(a_r)], [jnp.zeros_like(a_i)]
    for _ in range(T):
        pr.append(pr[-1] * a_r - pi[-1] * a_i)
        pi.append(pr[-2] * a_i + pi[-1] * a_r)
    pw_r, pw_i = jnp.stack(pr), jnp.stack(pi)
    cr, ci = c_re.astype(f32), c_im.astype(f32)
    ca_r = cr[None] * pw_r[:, :, :, None, :] - ci[None] * pw_i[:, :, :, None, :]
    ca_i = cr[None] * pw_i[:, :, :, None, :] + ci[None] * pw_r[:, :, :, None, :]
    kk = (jnp.einsum('mdgop,dgph->mdgoh', ca_r[:T], bb_r, precision=HIGHEST)
          - jnp.einsum('mdgop,dgph->mdgoh', ca_i[:T], bb_i, precision=HIGHEST))
    kf, kb = kk[:, 0], kk[:, 1]
    G, H, P = cr.shape[1], cr.shape[2], cr.shape[3]
    lagk = jnp.concatenate([jnp.flip(kb[1:], 0), kf[:1] + kb[:1], kf[1:]], axis=0)
    idx = (np.arange(T)[None, :] - np.arange(T)[:, None]) + (T - 1)
    toe = lagk[idx]
    m_mat = jnp.transpose(toe, (2, 0, 4, 1, 3)).reshape(G, T * H, T * H)
    def cmul_pb(pwr, pwi, d_):
        er = pwr[..., None] * bb_r[d_][None] - pwi[..., None] * bb_i[d_][None]
        ei = pwr[..., None] * bb_i[d_][None] + pwi[..., None] * bb_r[d_][None]
        tr = lambda v: jnp.transpose(v, (1, 0, 3, 2)).reshape(G, T * H, P)
        return tr(er), tr(ei)
    ef_r, ef_i = cmul_pb(jnp.flip(pw_r[:T, 0], 0), jnp.flip(pw_i[:T, 0], 0), 0)
    eb_r, eb_i = cmul_pb(pw_r[:T, 1], pw_i[:T, 1], 1)
    e_mat = jnp.concatenate([ef_r, eb_r, ef_i, eb_i], axis=-1)
    def carry(car, cai):
        rr = jnp.transpose(car, (1, 3, 0, 2)).reshape(G, P, T * H)
        ri = jnp.transpose(-cai, (1, 3, 0, 2)).reshape(G, P, T * H)
        return rr, ri
    cf_r, cf_i = carry(ca_r[1:T + 1, 0], ca_i[1:T + 1, 0])
    cb_r, cb_i = carry(jnp.flip(ca_r[1:T + 1, 1], 0), jnp.flip(ca_i[1:T + 1, 1], 0))
    zp = jnp.zeros_like(cf_r)
    wf = jnp.concatenate([cf_r, zp, cf_i, zp], axis=1)
    wb = jnp.concatenate([zp, cb_r, zp, cb_i], axis=1)
    w_out = jnp.concatenate([m_mat, wf, wb], axis=1)
    at_r = jnp.concatenate([pw_r[T, 0], pw_r[T, 1]], axis=-1)[:, None, :]
    at_i = jnp.concatenate([pw_i[T, 0], pw_i[T, 1]], axis=-1)[:, None, :]
    d_t = jnp.tile(d.astype(f32).reshape(G, 1, H), (1, 1, T))
    return e_mat.astype(BF16), w_out.astype(BF16), at_r, at_i, d_t


def _s5_kernel(x_ref, e_ref, w_ref, ar_ref, ai_ref, d_ref, init_ref, y_ref, fin_ref,
               e_scr, sa_scr, sb_scr, *, gb, nb, nc, p2):
    nt = nc // 2
    lane = lax.broadcasted_iota(jnp.int32, (2 * nb, p2), 1)
    is_fwd = lane < (p2 // 2)
    first = lax.broadcasted_iota(jnp.int32, (2 * nb, p2), 0) < nb
    swap = lambda v: pltpu.roll(v, nb, axis=0)
    for g in range(gb):
        e_scr[g] = _dot(x_ref[g], e_ref[g])

    def step(k, carry):
        out = []
        rf = pl.ds(pl.multiple_of(k * 2 * nb, 2 * nb), 2 * nb)
        rb = pl.ds(pl.multiple_of((nt - 1 - k) * 2 * nb, 2 * nb), 2 * nb)
        for g in range(gb):
            cr, ci = carry[2 * g], carry[2 * g + 1]
            ar, ai = ar_ref[g], ai_ref[g]
            er = jnp.where(is_fwd, e_scr[g, rf, 0:p2], swap(e_scr[g, rb, 0:p2]))
            ei = jnp.where(is_fwd, e_scr[g, rf, p2:2 * p2], swap(e_scr[g, rb, p2:2 * p2]))
            ur = ar * cr - ai * ci + er
            ui = ar * ci + ai * cr + ei
            ur4, ui4 = swap(ur), swap(ui)
            sr = jnp.where(first, cr, ur4)
            si = jnp.where(first, ci, ui4)
            sa_scr[g, rf, 0:p2] = sr
            sa_scr[g, rf, p2:2 * p2] = si
            sb_scr[g, rb, 0:p2] = swap(sr)
            sb_scr[g, rb, p2:2 * p2] = swap(si)
            xr = jnp.where(first, ur, ur4)
            xi = jnp.where(first, ui, ui4)
            zr = ar * xr - ai * xi + er
            zi = ar * xi + ai * xr + ei
            out.append(jnp.where(first, swap(zr), zr))
            out.append(jnp.where(first, swap(zi), zi))
        return tuple(out)

    init = []
    for g in range(gb):
        init += [init_ref[g, :, 0:p2], init_ref[g, :, p2:2 * p2]]
    fin = lax.fori_loop(0, nt, step, tuple(init))
    for g in range(gb):
        fin_ref[g] = jnp.concatenate([fin[2 * g], fin[2 * g + 1]], axis=1)
        x = x_ref[g]
        lhs = jnp.concatenate([x, sa_scr[g].astype(BF16), sb_scr[g].astype(BF16)], axis=1)
        y = _dot(lhs, w_ref[g]) + x.astype(F32) * d_ref[g]
        y_ref[g] = jax.nn.gelu(y).astype(BF16)


def _s5_scan(xg, mats, init, nb):
    e_mat, w_out, at_r, at_i, d_t = mats
    G, R, th = xg.shape
    p4 = e_mat.shape[-1]
    p2 = p4 // 2
    gb = 2
    nc = R // nb
    assert 2 * nb == 8 and nc % 2 == 0, "two chunks of batch rows must fill one 8-sublane tile"
    kern = functools.partial(_s5_kernel, gb=gb, nb=nb, nc=nc, p2=p2)
    g3 = lambda i: (i, 0, 0)
    return pl.pallas_call(
        kern,
        grid=(G // gb,),
        in_specs=[pl.BlockSpec((gb, R, th), g3),
                  pl.BlockSpec((gb, th, p4), g3),
                  pl.BlockSpec((gb, th + 2 * p4, th), g3),
                  pl.BlockSpec((gb, 1, p2), g3),
                  pl.BlockSpec((gb, 1, p2), g3),
                  pl.BlockSpec((gb, 1, th), g3),
                  pl.BlockSpec((gb, 2 * nb, p4), g3)],
        out_specs=[pl.BlockSpec((gb, R, th), g3),
                   pl.BlockSpec((gb, 2 * nb, p4), g3)],
        out_shape=[jax.ShapeDtypeStruct((G, R, th), BF16),
                   jax.ShapeDtypeStruct((G, 2 * nb, p4), F32)],
        scratch_shapes=[pltpu.VMEM((gb, R, p4), F32),
                        pltpu.VMEM((gb, R, p4), F32),
                        pltpu.VMEM((gb, R, p4), F32)],
        compiler_params=_cparams("parallel"),
        name="s5_scan",
    )(xg, e_mat, w_out, at_r, at_i, d_t, init)


def _lane_group(rows, h):
    return lax.broadcasted_iota(jnp.int32, (rows, LANES), 1) // h


PACK_ROWS = 32


def _s5_pack_kernel(u_ref, o_ref, t_scr, *, h, rows):
    gl = LANES // h
    t_scr[...] = _swap_major(u_ref[...])
    pr = min(rows, PACK_ROWS)
    grp = _lane_group(pr, h)
    for rb in range(rows // pr):
        rs = slice(rb * pr, (rb + 1) * pr)
        for half in range(S5_CHUNK // gl):
            acc = [None] * gl
            for jj in range(gl):
                v = t_scr[half * gl + jj, rs, :]
                for g in range(gl):
                    r = pltpu.roll(v, ((jj - g) * h) % LANES, axis=1) if jj != g else v
                    acc[g] = r if acc[g] is None else jnp.where(grp == jj, r, acc[g])
            for g in range(gl):
                o_ref[g, rs, half * LANES:(half + 1) * LANES] = acc[g].astype(BF16)


def _s5_unpack_kernel(y_ref, o_ref, t_scr, *, h, rows):
    gl = LANES // h
    pr = min(rows, PACK_ROWS)
    grp = _lane_group(pr, h)
    for rb in range(rows // pr):
        rs = slice(rb * pr, (rb + 1) * pr)
        for half in range(S5_CHUNK // gl):
            ys =[y_ref[g, rs, half * LANES:(half + 1) * LANES].astype(F32) for g in range(gl)]
            for tt in range(gl):
                acc = None
                for g in range(gl):
                    r = pltpu.roll(ys[g], ((g - tt) * h) % LANES, axis=1) if g != tt else ys[g]
                    acc = r if acc is None else jnp.where(grp == g, r, acc)
                t_scr[half * gl + tt, rs, :] = acc
    o_ref[...] = _swap_major(t_scr[...])


def _s5_pack(u3, h):
    R, t, w = u3.shape
    gl = LANES // h
    rows = min(R, 256)
    kern = functools.partial(_s5_pack_kernel, h=h, rows=rows)
    return pl.pallas_call(
        kern,
        grid=(w // LANES, R // rows),
        in_specs=[pl.BlockSpec((rows, t, LANES), lambda l, i: (i, 0, l))],
        out_specs=pl.BlockSpec((gl, rows, t * h), lambda l, i: (l, i, 0)),
        out_shape=jax.ShapeDtypeStruct((w // h, R, t * h), BF16),
        scratch_shapes=[pltpu.VMEM((t, rows, LANES), F32)],
        compiler_params=_cparams("parallel", "parallel"),
        name="s5_pack",
    )(u3)


def _s5_unpack(yg, h):
    G, R, th = yg.shape
    t = th // h
    gl = LANES // h
    rows = min(R, 256)
    kern = functools.partial(_s5_unpack_kernel, h=h, rows=rows)
    return pl.pallas_call(
        kern,
        grid=(G // gl, R // rows),
        in_specs=[pl.BlockSpec((gl, rows, th), lambda l, i: (l, i, 0))],
        out_specs=pl.BlockSpec((rows, t, LANES), lambda l, i: (i, 0, l)),
        out_shape=jax.ShapeDtypeStruct((R, t, G * h), F32),
        scratch_shapes=[pltpu.VMEM((t, rows, LANES), F32)],
        compiler_params=_cparams("parallel", "parallel"),
        name="s5_unpack",
    )(yg)


def _s5_mixer(u4, mats, init, h):
    lc, nb, t, w = u4.shape
    yg, fin = _s5_scan(_s5_pack(u4.reshape(lc * nb, t, w), h), mats, init, nb)
    return _s5_unpack(yg, h).reshape(lc, nb, t, w), fin


def _outproj_kernel(yh_ref, ys_ref, x_ref, mod_ref, g_ref, wg_ref, bg_ref, wo_ref,
                    xo_ref, hx_ref, *, d, dh, tm):
    gate = mod_ref[0, :, 2 * d:3 * d]
    shift = mod_ref[0, :, 3 * d:4 * d]
    scale = mod_ref[0, :, 4 * d:5 * d]
    ys = ys_ref[:, 0, :, :].reshape(tm, ys_ref.shape[-1])
    glu = ys * jax.nn.sigmoid(_dot(ys.astype(BF16), wg_ref[0]) + bg_ref[0])
    yx = _dot(yh_ref[0], wo_ref[0, 0:dh, :]) + _dot(glu.astype(BF16), wo_ref[0, dh:, :])
    xo = x_ref[0] + gate * _rms(yx, g_ref[0, 1:2, :])
    xo_ref[0] = xo
    hx_ref[0] = (_rms(xo, g_ref[0, 2:3, :]) * (1.0 + scale) + shift).astype(BF16)


def _outproj(yh, ys4, x, mod, mod_row, layer, norm_g, w_glu, b_glu, w_out, tm):
    b, s, d = x.shape
    dh = yh.shape[-1]
    ds5 = ys4.shape[-1]
    tc = tm // S5_CHUNK
    kern = functools.partial(_outproj_kernel, d=d, dh=dh, tm=tm)
    lyr = lambda bi, i: (layer, 0, 0)
    t3 = lambda bi, i: (bi, i, 0)
    return pl.pallas_call(
        kern,
        grid=(b, s // tm),
        in_specs=[pl.BlockSpec((1, tm, dh), t3),
                  pl.BlockSpec((tc, 1, S5_CHUNK, ds5), lambda bi, i: (i, bi, 0, 0)),
                  pl.BlockSpec((1, tm, d), t3),
                  pl.BlockSpec((1, 1, mod.shape[-1]), lambda bi, i: (mod_row(bi), 0, 0)),
                  pl.BlockSpec((1,) + norm_g.shape[1:], lyr),
                  pl.BlockSpec((1, ds5, ds5), lyr), pl.BlockSpec((1, 1, ds5), lyr),
                  pl.BlockSpec((1, dh + ds5, d), lyr)],
        out_specs=[pl.BlockSpec((1, tm, d), t3), pl.BlockSpec((1, tm, d), t3)],
        out_shape=[jax.ShapeDtypeStruct((b, s, d), F32), jax.ShapeDtypeStruct((b, s, d), BF16)],
        compiler_params=_cparams("parallel", "parallel"),
        name="outproj",
    )(yh, ys4, x, mod, norm_g, w_glu, b_glu, w_out)


def _ffn_kernel(hm_ref, hp_ref, hn_ref, x_ref, mod_ref, g_ref, wg_ref, wv_ref, cw_ref, cb_ref, wd_ref,
                o_ref, acc_ref, *, tm, d, wg, vertical):
    i = pl.program_id(1)
    nt = pl.num_programs(1)
    j = pl.program_id(2)
    nj = pl.num_programs(2)
    hm = hm_ref[0]
    cw = cw_ref[0]
    if vertical:
        top = jnp.where(i > 0, 1.0, 0.0).astype(BF16)
        bot = jnp.where(i < nt - 1, 1.0, 0.0).astype(BF16)
        ha = jnp.concatenate([hp_ref[0] * top, hm, hn_ref[0] * bot], axis=0)
    else:
        ha = hm
    g = _dot(ha, wg_ref[0])

    def vcol(dx):
        if not vertical:
            return g * cw[3 + dx:4 + dx]
        return (g[0:tm] * cw[dx:dx + 1] + g[wg:wg + tm] * cw[3 + dx:4 + dx]
                + g[2 * wg:2 * wg + tm] * cw[6 + dx:7 + dx])

    col = lax.broadcasted_iota(jnp.int32, (tm, g.shape[-1]), 0) & (wg - 1)
    conv = (vcol(1) + jnp.where(col > 0, pltpu.roll(vcol(0), 1, axis=0), 0.0)
            + jnp.where(col < wg - 1, pltpu.roll(vcol(2), tm - 1, axis=0), 0.0) + cb_ref[0])
    v = _dot(hm, wv_ref[0])
    hmid = (jax.nn.gelu(conv) * v).astype(BF16)
    part = _dot(hmid, wd_ref[0])

    @pl.when(j == 0)
    def _():
        acc_ref[...] = part

    @pl.when(j > 0)
    def _():
        acc_ref[...] += part

    @pl.when(j == nj - 1)
    def _():
        gate = mod_ref[0, :, 5 * d:6 * d]
        o_ref[0] = x_ref[0] + gate * _rms(acc_ref[...], g_ref[0, 3:4, :])


def _ffn(hx, x, mod, mod_row, layer, norm_g, w_up, conv_w, conv_b, w_down, tm, wg, vertical):
    b, s, d = x.shape
    f = w_down.shape[1]
    fc = f // 2 if (f // 2) % LANES == 0 else f
    nf = f // fc
    nt = s // tm
    r = tm // wg if vertical else 1
    hb = wg if vertical else 16
    nhb = s // hb
    kern = functools.partial(_ffn_kernel, tm=tm, d=d, wg=wg, vertical=vertical)
    t3 = lambda bi, i, j: (bi, i, 0)
    return pl.pallas_call(
        kern,
        grid=(b, nt, nf),
        in_specs=[pl.BlockSpec((1, tm, d), t3),
                  pl.BlockSpec((1, hb, d), lambda bi, i, j: (bi, jnp.maximum(i * r - 1, 0), 0)),
                  pl.BlockSpec((1, hb, d), lambda bi, i, j: (bi, jnp.minimum((i + 1) * r, nhb - 1), 0)),
                  pl.BlockSpec((1, tm, d), t3),
                  pl.BlockSpec((1, 1, mod.shape[-1]), lambda bi, i, j: (mod_row(bi), 0, 0)),
                  pl.BlockSpec((1,) + norm_g.shape[1:], lambda bi, i, j: (layer, 0, 0)),
                  pl.BlockSpec((1, d, fc), lambda bi, i, j: (layer, 0, j)),
                  pl.BlockSpec((1, d, fc), lambda bi, i, j: (layer, 0, nf + j)),
                  pl.BlockSpec((1, 9, fc), lambda bi, i, j: (layer, 0, j)),
                  pl.BlockSpec((1, 1, fc), lambda bi, i, j: (layer, 0, j)),
                  pl.BlockSpec((1, fc, d), lambda bi, i, j: (layer, j, 0))],
        out_specs=pl.BlockSpec((1, tm, d), t3),
        out_shape=jax.ShapeDtypeStruct((b, s, d), F32),
        scratch_shapes=[pltpu.VMEM((tm, d), F32)],
        compiler_params=_cparams("parallel", "parallel", "arbitrary"),
        name="conv_glu_ffn",
    )(hx, hx, hx, x, mod, norm_g, w_up, w_up, conv_w, conv_b, w_down)


def kernel(x, c, ctx, c_ctx, w_ada, b_ada, norm_g, w_in, hy_short_w, hy_short_b,
           filt_w_in, filt_b_in, filt_w_hid, filt_b_hid, filt_freq, filt_w_out, hy_bias,
           s5_lam_re, s5_lam_im, s5_log_step, s5_b_re, s5_b_im, s5_c_re, s5_c_im, s5_d,
           s5_w_glu, s5_b_glu, w_out, ffn_w_up, ffn_conv_w, ffn_conv_b, ffn_w_down):
    depth = w_ada.shape[0]
    bsz, seq, d = x.shape
    lctx = ctx.shape[1]
    dh = hy_bias.shape[-1]
    G, P, H = s5_b_re.shape[2], s5_b_re.shape[3], s5_b_re.shape[4]
    dff = ffn_w_down.shape[1]
    assert bsz % 2 == 0 and bsz <= 4 and seq % GRID_W == 0 and GRID_W & (GRID_W - 1) == 0

    cond = jnp.zeros((8, d), F32).at[:bsz].set(c).at[bsz].set(c_ctx)
    mod = _ada_mod(cond, w_ada, b_ada).reshape(depth * 8, 1, 6 * d)

    filt_args = (filt_w_in, filt_b_in, filt_w_hid, filt_b_hid, filt_freq, filt_w_out)
    long_conv = _LongConv(seq, dh, _hyena_filters(seq, *filt_args, dh))
    k_ctx = _hyena_filters(lctx, *filt_args, dh)
    short_c = tuple(jnp.asarray(a, BF16) for a in _short_consts(lctx))

    w_in_b = w_in.astype(BF16)
    w_glu_b = s5_w_glu.astype(BF16)
    w_out_b = w_out.astype(BF16)
    w_up_b = ffn_w_up.astype(BF16)
    w_down_b = ffn_w_down.astype(BF16)
    sb = hy_short_b.reshape(depth, 1, 3 * dh)
    cw = ffn_conv_w.reshape(depth, 9, dff)
    cb = ffn_conv_b.reshape(depth, 1, dff)
    bg = s5_b_glu.reshape(depth, 1, -1)
    hb = hy_bias.reshape(depth, 1, dh)

    tm = min(seq, ROW_TILE)
    for l in range(depth):
        last = l == depth - 1
        row_x = lambda bi, l=l: 8 * l + bi
        row_c = lambda bi, l=l: 8 * l + bsz
        mats = _s5_matrices(s5_lam_re[l], s5_lam_im[l], s5_log_step[l], s5_b_re[l], s5_b_im[l],
                            s5_c_re[l], s5_c_im[l], s5_d[l])

        x0c, zc, uc = _inproj(ctx, mod, row_c, l, norm_g, w_in_b, hy_short_w, sb, dh, lctx)
        ysc, ctx_state = _s5_mixer(uc, mats, jnp.zeros((G, 2 * bsz, 4 * P), F32), H)

        x0, z, u = _inproj(x, mod, row_x, l, norm_g, w_in_b, hy_short_w, sb, dh, tm)
        ys, _ = _s5_mixer(u, mats, ctx_state, H)
        yh = long_conv(z, x0, hb, l)
        x, hx = _outproj(yh, ys, x, mod, row_x, l, norm_g, w_glu_b, bg, w_out_b, tm)
        x = _ffn(hx, x, mod, row_x, l, norm_g, w_up_b, cw, cb, w_down_b, tm, GRID_W, True)

        if not last:
            yhc = _short_conv(zc, x0c, k_ctx, l, short_c, hb)
            ctx, hc = _outproj(yhc, ysc, ctx, mod, row_c, l, norm_g, w_glu_b, bg, w_out_b, lctx)
            ctx = _ffn(hc, ctx, mod, row_c, l, norm_g, w_up_b, cw, cb, w_down_b, lctx, lctx, False)
    return x
```

```python
import functools
import math

import numpy as np
import jax
import jax.numpy as jnp
from jax import lax
from jax.experimental import pallas as pl
from jax.experimental.pallas import tpu as pltpu

GRID_W = 64
RMS_EPS = 1e-6
DECAY_TARGET = 1e-2
FAST_DECAY_PCT = 0.3
SLOW_DECAY_PCT = 1.5
S5_CHUNK = 16
LANES = 128
HALO = 16
ROW_TILE = 512
SUB_TILES = 4
IN_SUB_TILES = 2
FFN_SUB_TILES = 1
PACK_ROWS = 32
VMEM_LIMIT = 56 * 1024 * 1024

F32 = jnp.float32
BF16 = jnp.bfloat16
HIGHEST = lax.Precision.HIGHEST


def _cparams(*sem):
    return pltpu.CompilerParams(dimension_semantics=sem, vmem_limit_bytes=VMEM_LIMIT)


def _dot(a, b, **kw):
    return jnp.dot(a, b, preferred_element_type=F32, **kw)


def _ada_kernel(cond_ref, w_ref, b_ref, o_ref):
    cv = cond_ref[...]
    s = cv * jax.nn.sigmoid(cv)
    o_ref[0] = _dot(s, w_ref[0], precision=HIGHEST) + b_ref[0]


def _ada_mod(cond, w_ada, b_ada):
    depth, d, n = w_ada.shape
    tn = n // 4
    return pl.pallas_call(
        _ada_kernel,
        grid=(depth, n // tn),
        in_specs=[pl.BlockSpec((8, d), lambda l, j: (0, 0)),
                  pl.BlockSpec((1, d, tn), lambda l, j: (l, 0, j)),
                  pl.BlockSpec((1, 1, tn), lambda l, j: (l, 0, j))],
        out_specs=pl.BlockSpec((1, 8, tn), lambda l, j: (l, 0, j)),
        out_shape=jax.ShapeDtypeStruct((depth, 8, n), F32),
        compiler_params=_cparams("parallel", "parallel"),
        name="ada_mod",
    )(cond, w_ada, b_ada.reshape(depth, 1, n))


def _rms(v, g):
    ms = jnp.mean(v * v, axis=-1, keepdims=True)
    return v * lax.rsqrt(ms + RMS_EPS) * g


def _inproj_kernel(xm_ref, xp_ref, xn_ref, mod_ref, g_ref, w_ref, sw_ref, sb_ref,
                   x0_ref, z_ref, u_ref, p_scr, *, tm, d, dh):
    i = pl.program_id(1)
    nt = pl.num_programs(1)
    shift = mod_ref[0, :, 0:d]
    scale = mod_ref[0, :, d:2 * d]
    xa = jnp.concatenate([xp_ref[0], xm_ref[0], xn_ref[0]], axis=0)
    sw = sw_ref[0]
    ns = IN_SUB_TILES
    sub = tm // ns
    cut = [0] + [HALO + s * sub for s in range(1, ns)] + [tm + 2 * HALO]

    def project(s):
        lo, hi = cut[s], cut[s + 1]
        xn = (_rms(xa[lo:hi], g_ref[0, 0:1, :]) * (1.0 + scale) + shift).astype(BF16)
        p = _dot(xn, w_ref[0])
        p_scr[lo:hi, :] = p[:, :3 * dh]
        if s == 0:
            inside = jnp.where(i > 0, 1.0, 0.0).astype(F32)
            p_scr[HALO - 8:HALO, :] = p[HALO - 8:HALO, :3 * dh] * inside
        if s == ns - 1:
            inside = jnp.where(i < nt - 1, 1.0, 0.0).astype(F32)
            p_scr[HALO + tm:HALO + tm + 8, :] = p[HALO + tm - lo:HALO + tm + 8 - lo, :3 * dh] * inside
        a, b = max(lo, HALO), min(hi, HALO + tm)
        u = p[a - lo:b - lo, 3 * dh:]
        u_ref[(a - HALO) // S5_CHUNK:(b - HALO) // S5_CHUNK, 0, :, :] = u.reshape(
            (b - a) // S5_CHUNK, S5_CHUNK, u.shape[-1])

    def conv_gate(s):
        r0 = HALO + s * sub
        conv = (p_scr[pl.ds(r0 - 1, sub), :] * sw[0:1] + p_scr[pl.ds(r0, sub), :] * sw[1:2]
                + p_scr[pl.ds(r0 + 1, sub), :] * sw[2:3] + sb_ref[0])
        rs = slice(s * sub, (s + 1) * sub)
        x0_ref[0, rs, :] = conv[:, :dh].astype(BF16)
        z_ref[0, rs, :] = (conv[:, dh:2 * dh] * conv[:, 2 * dh:]).astype(BF16)

    project(0)
    for s in range(ns):
        if s + 1 < ns:
            project(s + 1)
        conv_gate(s)


def _inproj(x, mod, mod_row, layer, norm_g, w_in, sw, sb, dh, tm):
    b, s, d = x.shape
    dp = w_in.shape[-1]
    ds5 = dp - 3 * dh
    nt = s // tm
    r = tm // HALO
    nh = s // HALO
    tc = tm // S5_CHUNK
    kern = functools.partial(_inproj_kernel, tm=tm, d=d, dh=dh)
    lyr = lambda bi, i: (layer, 0, 0)
    return pl.pallas_call(
        kern,
        grid=(b, nt),
        in_specs=[pl.BlockSpec((1, tm, d), lambda bi, i: (bi, i, 0)),
                  pl.BlockSpec((1, HALO, d), lambda bi, i: (bi, jnp.maximum(i * r - 1, 0), 0)),
                  pl.BlockSpec((1, HALO, d), lambda bi, i: (bi, jnp.minimum((i + 1) * r, nh - 1), 0)),
                  pl.BlockSpec((1, 1, mod.shape[-1]), lambda bi, i: (mod_row(bi), 0, 0)),
                  pl.BlockSpec((1,) + norm_g.shape[1:], lyr),
                  pl.BlockSpec((1, d, dp), lyr),
                  pl.BlockSpec((1, 3, 3 * dh), lyr),
                  pl.BlockSpec((1, 1, 3 * dh), lyr)],
        out_specs=[pl.BlockSpec((1, tm, dh), lambda bi, i: (bi, i, 0)),
                   pl.BlockSpec((1, tm, dh), lambda bi, i: (bi, i, 0)),
                   pl.BlockSpec((tc, 1, S5_CHUNK, ds5), lambda bi, i: (i, bi, 0, 0))],
        out_shape=[jax.ShapeDtypeStruct((b, s, dh), BF16),
                   jax.ShapeDtypeStruct((b, s, dh), BF16),
                   jax.ShapeDtypeStruct((s // S5_CHUNK, b, S5_CHUNK, ds5), F32)],
        scratch_shapes=[pltpu.VMEM((tm + 2 * HALO, 3 * dh), F32)],
        compiler_params=_cparams("parallel", "arbitrary"),
        name="inproj",
    )(x, x, x, mod, norm_g, w_in, sw, sb)


def _filter_feats(L, emb):
    bands = (emb - 1) // 2
    t = np.linspace(0.0, 1.0, L, dtype=np.float32).astype(np.float64)[:, None]
    w = (2.0 * math.pi / L) * np.arange(L, dtype=np.float64)[:, None]
    f = np.linspace(1e-4, bands - 1, bands, dtype=np.float32).astype(np.float64)[None, :]
    z = np.concatenate([t, np.cos(f * w), -np.sin(f * w)], axis=-1)
    zp = np.zeros((L, LANES), np.float32)
    zp[:, :emb] = z
    return zp


def _filter_kernel(z_ref, win_ref, bin_ref, whid_ref, bhid_ref, fr_ref, wf_ref, wb_ref, dl_ref, o_ref):
    z = z_ref[...]
    fr = fr_ref[0]
    h = jnp.sin(fr * (_dot(z, win_ref[0], precision=HIGHEST) + bin_ref[0]))
    for i in range(whid_ref.shape[1]):
        h = jnp.sin(fr * (_dot(h, whid_ref[0, i], precision=HIGHEST) + bhid_ref[0, i]))
    o_ref[0, 0] = _dot(h, wf_ref[0], precision=HIGHEST) * jnp.exp(-z[:, 0:1] * dl_ref[...])
    hb = _dot(h, wb_ref[0], precision=HIGHEST) * jnp.exp(-z[:, LANES:LANES + 1] * dl_ref[...])
    first = (pl.program_id(1) == 0) & (lax.broadcasted_iota(jnp.int32, (z.shape[0], 1), 0) == 0)
    o_ref[0, 1] = jnp.where(first, 0.0, hb)


def _block_diag2(w):
    zero = jnp.zeros_like(w)
    return jnp.concatenate([jnp.concatenate([w, zero], axis=-1), jnp.concatenate([zero, w], axis=-1)], axis=-2)


def _hyena_filters(L, f_w_in, f_b_in, f_w_hid, f_b_hid, f_freq, f_w_out, dh):
    depth, emb, hid = f_w_in.shape
    n_inner = f_w_hid.shape[1]
    tl = min(L, 1024)
    z1 = _filter_feats(L, emb)
    z = jnp.asarray(np.concatenate([z1, np.concatenate([z1[:1], z1[:0:-1]], axis=0)], axis=1))
    win = _block_diag2(jnp.zeros((depth, LANES, hid), F32).at[:, :emb].set(f_w_in))
    zero = jnp.zeros((depth, hid, dh), F32)
    wf = jnp.concatenate([f_w_out[:, :, :dh], zero], axis=1)
    wb = jnp.concatenate([zero, f_w_out[:, :, dh:]], axis=1)
    twice = lambda v, shp: jnp.tile(v.reshape(shp), (1,) * (len(shp) - 1) + (2,))
    deltas = np.abs(np.linspace(math.log(DECAY_TARGET) / FAST_DECAY_PCT,
                                math.log(DECAY_TARGET) / SLOW_DECAY_PCT, dh, dtype=np.float32))[None, :]
    h2 = 2 * hid
    out = pl.pallas_call(
        _filter_kernel,
        grid=(depth, L // tl),
        in_specs=[pl.BlockSpec((tl, 2 * LANES), lambda l, i: (i, 0)),
                  pl.BlockSpec((1, 2 * LANES, h2), lambda l, i: (l, 0, 0)),
                  pl.BlockSpec((1, 1, h2), lambda l, i: (l, 0, 0)),
                  pl.BlockSpec((1, n_inner, h2, h2), lambda l, i: (l, 0, 0, 0)),
                  pl.BlockSpec((1, n_inner, 1, h2), lambda l, i: (l, 0, 0, 0)),
                  pl.BlockSpec((1, 1, h2), lambda l, i: (l, 0, 0)),
                  pl.BlockSpec((1, h2, dh), lambda l, i: (l, 0, 0)),
                  pl.BlockSpec((1, h2, dh), lambda l, i: (l, 0, 0)),
                  pl.BlockSpec((1, dh), lambda l, i: (0, 0))],
        out_specs=pl.BlockSpec((1, 2, tl, dh), lambda l, i: (l, 0, i, 0)),
        out_shape=jax.ShapeDtypeStruct((depth, 2, L, dh), F32),
        compiler_params=_cparams("parallel", "parallel"),
        name="hyena_filter",
    )(z, win, twice(f_b_in, (depth, 1, hid)), _block_diag2(f_w_hid), twice(f_b_hid, (depth, n_inner, 1, hid)),
      twice(f_freq, (depth, 1, hid)), wf, wb, jnp.asarray(deltas))
    return out.reshape(depth, 2 * L, dh)


def _dft_consts(n1, n2):
    n = n1 * n2
    n2h = n2 // 2
    k2 = np.arange(n2)[:, None]
    a = -2.0 * np.pi * k2 * np.arange(n2)[None, :] / n2
    fr, fi = np.cos(a), np.sin(a)
    f1 = np.block([[fr[:, :n2h], -fi[:, :n2h]], [fi[:, :n2h], fr[:, :n2h]]])
    f1k = np.concatenate([fr, fi], axis=0)
    cr, ci = fr[:n2h] / n, -fi[:n2h] / n
    f3 = np.block([[cr, -ci], [ci, cr]])
    b = -2.0 * np.pi * np.arange(n1)[:, None] * np.arange(n1)[None, :] / n1
    t = -2.0 * np.pi * np.arange(n2)[:, None] * np.arange(n1)[None, :] / n
    return dict(f1=f1.astype(np.float32), f1k=f1k.astype(np.float32), f3=f3.astype(np.float32),
                gr=np.cos(b).astype(np.float32), gi=np.sin(b).astype(np.float32),
                tr=np.cos(t).astype(np.float32)[:, None, :], ti=np.sin(t).astype(np.float32)[:, None, :])


def _tables_kernel(gr_ref, gi_ref, tr_ref, ti_ref, m1_ref, m2_ref, *, kb):
    gr, gi = gr_ref[...], gi_ref[...]
    for q in range(kb):
        tr, ti = tr_ref[q], ti_ref[q]
        re = gr * tr - gi * ti
        im = gr * ti + gi * tr
        m1_ref[q] = jnp.concatenate([jnp.concatenate([re, -im], axis=1),
                                     jnp.concatenate([im, re], axis=1)], axis=0).astype(BF16)
        ret, imt = re.T, im.T
        m2_ref[q] = jnp.concatenate([jnp.concatenate([ret, imt], axis=1),
                                     jnp.concatenate([-imt, ret], axis=1)], axis=0).astype(BF16)


def _dft_tables(c, n1, n2):
    kb = 8
    kern = functools.partial(_tables_kernel, kb=kb)
    shp = jax.ShapeDtypeStruct((n2, 2 * n1, 2 * n1), BF16)
    return pl.pallas_call(
        kern,
        grid=(n2 // kb,),
        in_specs=[pl.BlockSpec((n1, n1), lambda i: (0, 0)),
                  pl.BlockSpec((n1, n1), lambda i: (0, 0)),
                  pl.BlockSpec((kb, 1, n1), lambda i: (i, 0, 0)),
                  pl.BlockSpec((kb, 1, n1), lambda i: (i, 0, 0))],
        out_specs=[pl.BlockSpec((kb, 2 * n1, 2 * n1), lambda i: (i, 0, 0)),
                   pl.BlockSpec((kb, 2 * n1, 2 * n1), lambda i: (i, 0, 0))],
        out_shape=[shp, shp],
        compiler_params=_cparams("parallel"),
        name="dft_tables",
    )(jnp.asarray(c["gr"]), jnp.asarray(c["gi"]), jnp.asarray(c["tr"]), jnp.asarray(c["ti"]))


def _swap_major(v):
    return pltpu.einshape("abc->bac", v)


def _s1_kernel(z_ref, f_ref, o_ref, y_scr, *, n2, nb1):
    zt = [_swap_major(z_ref[s].astype(F32)) for s in range(2)]
    for j in range(nb1):
        x = jnp.concatenate([zt[0][j], zt[1][j]], axis=0).astype(BF16)
        y = _dot(f_ref[...], x)
        y_scr[0, j] = y[:n2]
        y_scr[1, j] = y[n2:]
    for s in range(2):
        o_ref[0, s] = _swap_major(y_scr[s]).astype(BF16)


def _s1k_kernel(k_ref, f_ref, o_ref, y_scr, *, n2, nb1):
    kt = _swap_major(k_ref[0])
    for j in range(nb1):
        y = _dot(f_ref[...], kt[j].astype(BF16))
        y_scr[0, j] = y[:n2]
        y_scr[1, j] = y[n2:]
    for s in range(2):
        o_ref[0, s] = _swap_major(y_scr[s]).astype(BF16)


def _dft_stage1(z4, f1, n2, nb1):
    b, n2h, n1, c = z4.shape
    kern = functools.partial(_s1_kernel, n2=n2, nb1=nb1)
    return pl.pallas_call(
        kern,
        grid=(b // 2, n1 // nb1),
        in_specs=[pl.BlockSpec((2, n2h, nb1, c), lambda p, j: (p, 0, j, 0)),
                  pl.BlockSpec((2 * n2, 2 * n2h), lambda p, j: (0, 0))],
        out_specs=pl.BlockSpec((1, 2, n2, nb1, c), lambda p, j: (p, 0, 0, j, 0)),
        out_shape=jax.ShapeDtypeStruct((b // 2, 2, n2, n1, c), BF16),
        scratch_shapes=[pltpu.VMEM((2, nb1, n2, c), F32)],
        compiler_params=_cparams("parallel", "parallel"),
        name="dft_stage1",
    )(z4, f1)


def _dft_stage1_filter(k4, f1k, nb1):
    depth, n2, n1, c = k4.shape
    kern = functools.partial(_s1k_kernel, n2=n2, nb1=nb1)
    return pl.pallas_call(
        kern,
        grid=(depth, n1 // nb1),
        in_specs=[pl.BlockSpec((1, n2, nb1, c), lambda l, j: (l, 0, j, 0)),
                  pl.BlockSpec((2 * n2, n2), lambda l, j: (0, 0))],
        out_specs=pl.BlockSpec((1, 2, n2, nb1, c), lambda l, j: (l, 0, 0, j, 0)),
        out_shape=jax.ShapeDtypeStruct((depth, 2, n2, n1, c), BF16),
        scratch_shapes=[pltpu.VMEM((2, nb1, n2, c), F32)],
        compiler_params=_cparams("parallel", "parallel"),
        name="dft_stage1_filter",
    )(k4, f1k)


def _s2k_kernel(b_ref, m1_ref, o_ref, *, kb, n1):
    for q in range(kb):
        rows = slice(q * n1, (q + 1) * n1)
        xin = jnp.concatenate([b_ref[0, 0, rows, :], b_ref[0, 1, rows, :]], axis=0).astype(BF16)
        xf = _dot(m1_ref[q], xin)
        o_ref[0, 0, rows, :] = xf[:n1]
        o_ref[0, 1, rows, :] = xf[n1:]


def _dft_stage2_filter(bv, m1, n1, kb):
    depth, _, n, c = bv.shape
    kern = functools.partial(_s2k_kernel, kb=kb, n1=n1)
    return pl.pallas_call(
        kern,
        grid=(n // (kb * n1), depth),
        in_specs=[pl.BlockSpec((1, 2, kb * n1, c), lambda k, l: (l, 0, k, 0)),
                  pl.BlockSpec((kb, 2 * n1, 2 * n1), lambda k, l: (k, 0, 0))],
        out_specs=pl.BlockSpec((1, 2, kb * n1, c), lambda k, l: (l, 0, k, 0)),
        out_shape=jax.ShapeDtypeStruct((depth, 2, n, c), F32),
        compiler_params=_cparams("parallel", "parallel"),
        name="dft_stage2_filter",
    )(bv, m1)


def _s2_kernel(b_ref, m1_ref, kf_ref, m2_ref, o_ref, *, kb, n1):
    for q in range(kb):
        rows = slice(q * n1, (q + 1) * n1)
        xin = jnp.concatenate([b_ref[0, 0, rows, :], b_ref[0, 1, rows, :]], axis=0).astype(BF16)
        xf = _dot(m1_ref[q], xin)
        xr, xi = xf[:n1], xf[n1:]
        kr, ki = kf_ref[0, 0, rows, :], kf_ref[0, 1, rows, :]
        yin = jnp.concatenate([xr * kr - xi * ki, xr * ki + xi * kr], axis=0).astype(BF16)
        g = _dot(m2_ref[q], yin)
        o_ref[0, 0, rows, :] = g[:n1].astype(BF16)
        o_ref[0, 1, rows, :] = g[n1:].astype(BF16)


def _dft_stage2(bv, m1, kf, layer, m2, n1, kb):
    p, _, n, c = bv.shape
    kern = functools.partial(_s2_kernel, kb=kb, n1=n1)
    return pl.pallas_call(
        kern,
        grid=(n // (kb * n1), p),
        in_specs=[pl.BlockSpec((1, 2, kb * n1, c), lambda k, q: (q, 0, k, 0)),
                  pl.BlockSpec((kb, 2 * n1, 2 * n1), lambda k, q: (k, 0, 0)),
                  pl.BlockSpec((1, 2, kb * n1, c), lambda k, q: (layer, 0, k, 0)),
                  pl.BlockSpec((kb, 2 * n1, 2 * n1), lambda k, q: (k, 0, 0))],
        out_specs=pl.BlockSpec((1, 2, kb * n1, c), lambda k, q: (q, 0, k, 0)),
        out_shape=jax.ShapeDtypeStruct((p, 2, n, c), BF16),
        compiler_params=_cparams("parallel", "arbitrary"),
        name="dft_stage2",
    )(bv, m1, kf, m2)


def _s3_kernel(g_ref, f_ref, z_ref, x0_ref, bias_ref, o_ref, y_scr, *, n2h, nb1):
    gt = [_swap_major(g_ref[0, s].astype(F32)) for s in range(2)]
    for j in range(nb1):
        gin = jnp.concatenate([gt[0][j], gt[1][j]], axis=0).astype(BF16)
        y = _dot(f_ref[...], gin)
        y_scr[0, j] = y[:n2h]
        y_scr[1, j] = y[n2h:]
    for s in range(2):
        yh = _swap_major(y_scr[s]) + z_ref[s].astype(F32) * bias_ref[0]
        o_ref[s] = (x0_ref[s].astype(F32) * yh).astype(BF16)


def _dft_stage3(g5, f3, z4, x04, bias, layer, nb1):
    p, _, n2, n1, c = g5.shape
    n2h = n2 // 2
    kern = functools.partial(_s3_kernel, n2h=n2h, nb1=nb1)
    return pl.pallas_call(
        kern,
        grid=(p, n1 // nb1),
        in_specs=[pl.BlockSpec((1, 2, n2, nb1, c), lambda q, j: (q, 0, 0, j, 0)),
                  pl.BlockSpec((2 * n2h, 2 * n2), lambda q, j: (0, 0)),
                  pl.BlockSpec((2, n2h, nb1, c), lambda q, j: (q, 0, j, 0)),
                  pl.BlockSpec((2, n2h, nb1, c), lambda q, j: (q, 0, j, 0)),
                  pl.BlockSpec((1, 1, c), lambda q, j: (layer, 0, 0))],
        out_specs=pl.BlockSpec((2, n2h, nb1, c), lambda q, j: (q, 0, j, 0)),
        out_shape=jax.ShapeDtypeStruct((2 * p, n2h, n1, c), BF16),
        scratch_shapes=[pltpu.VMEM((2, nb1, n2h, c), F32)],
        compiler_params=_cparams("parallel", "parallel"),
        name="dft_stage3",
    )(g5, f3, z4, x04, bias)


class _LongConv:
    def __init__(self, L, c, k2s):
        n = 2 * L
        n1 = 1 << (int(math.log2(n)) // 2)
        n2 = n // n1
        assert n1 * n2 == n and n1 == n2, "long-convolution path needs 2L to be a square power of two"
        self.L, self.c, self.n1, self.n2 = L, c, n1, n2
        self.nb1 = min(n1, 16)
        self.kb = 4
        cst = _dft_consts(n1, n2)
        self.f1 = jnp.asarray(cst["f1"], BF16)
        self.f3 = jnp.asarray(cst["f3"], BF16)
        self.m1, self.m2 = _dft_tables(cst, n1, n2)
        depth = k2s.shape[0]
        bk = _dft_stage1_filter(k2s.reshape(depth, n2, n1, c), jnp.asarray(cst["f1k"], BF16), self.nb1)
        self.kf = _dft_stage2_filter(bk.reshape(depth, 2, n, c), self.m1, n1, self.kb)

    def __call__(self, z, x0, bias, layer):
        b, L, c = z.shape
        n1, n2 = self.n1, self.n2
        z4 = z.reshape(b, n2 // 2, n1, c)
        x04 = x0.reshape(b, n2 // 2, n1, c)
        b5 = _dft_stage1(z4, self.f1, n2, self.nb1)
        gv = _dft_stage2(b5.reshape(b // 2, 2, n1 * n2, c), self.m1, self.kf, layer, self.m2, n1, self.kb)
        y = _dft_stage3(gv.reshape(b // 2, 2, n2, n1, c), self.f3, z4, x04, bias, layer, self.nb1)
        return y.reshape(b, L, c)


def _short_consts(L):
    n = 2 * L
    a = -2.0 * np.pi * np.arange(n)[:, None] * np.arange(n)[None, :] / n
    fr, fi = np.cos(a), np.sin(a)
    ff = np.block([[fr[:, :L], -fi[:, :L]], [fi[:, :L], fr[:, :L]]])
    fk = np.concatenate([fr, fi], axis=0)
    cr, ci = fr[:L] / n, -fi[:L] / n
    finv = np.block([[cr, -ci], [ci, cr]])
    return ff.astype(np.float32), fk.astype(np.float32), finv.astype(np.float32)


def _short_conv_kernel(z_ref, x0_ref, k_ref, ff_ref, fk_ref, fi_ref, bias_ref, o_ref, *, L):
    n = 2 * L
    x = jnp.concatenate([z_ref[0], z_ref[1]], axis=0)
    xf = _dot(ff_ref[...], x)
    kf = _dot(fk_ref[...], k_ref[0].astype(BF16))
    xr, xi, kr, ki = xf[:n], xf[n:], kf[:n], kf[n:]
    yin = jnp.concatenate([xr * kr - xi * ki, xr * ki + xi * kr], axis=0).astype(BF16)
    y = _dot(fi_ref[...], yin)
    for s in range(2):
        yh = y[s * L:(s + 1) * L] + z_ref[s].astype(F32) * bias_ref[0]
        o_ref[s] = (x0_ref[s].astype(F32) * yh).astype(BF16)


def _short_conv(z, x0, k2s, layer, consts, bias):
    b, L, c = z.shape
    n = 2 * L
    cb = min(c, 256)
    ff, fk, finv = consts
    kern = functools.partial(_short_conv_kernel, L=L)
    return pl.pallas_call(
        kern,
        grid=(b // 2, c // cb),
        in_specs=[pl.BlockSpec((2, L, cb), lambda p, j: (p, 0, j)),
                  pl.BlockSpec((2, L, cb), lambda p, j: (p, 0, j)),
                  pl.BlockSpec((1, n, cb), lambda p, j: (layer, 0, j)),
                  pl.BlockSpec((2 * n, 2 * L), lambda p, j: (0, 0)),
                  pl.BlockSpec((2 * n, n), lambda p, j: (0, 0)),
                  pl.BlockSpec((2 * L, 2 * n), lambda p, j: (0, 0)),
                  pl.BlockSpec((1, 1, cb), lambda p, j: (layer, 0, j))],
        out_specs=pl.BlockSpec((2, L, cb), lambda p, j: (p, 0, j)),
        out_shape=jax.ShapeDtypeStruct((b, L, c), BF16),
        compiler_params=_cparams("parallel", "parallel"),
        name="short_conv",
    )(z, x0, k2s, ff, fk, finv, bias)


def _s5_matrices(lam_re, lam_im, log_step, b_re, b_im, c_re, c_im, d):
    f32 = F32
    T = S5_CHUNK
    lr, li = lam_re.astype(f32), lam_im.astype(f32)
    dt = jnp.exp(log_step.astype(f32))[..., None]
    mag = jnp.exp(lr * dt)
    a_r, a_i = mag * jnp.cos(li * dt), mag * jnp.sin(li * dt)
    den = lr * lr + li * li
    q_r = ((a_r - 1.0) * lr + a_i * li) / den
    q_i = (a_i * lr - (a_r - 1.0) * li) / den
    br, bi = b_re.astype(f32), b_im.astype(f32)
    bb_r = q_r[..., None] * br - q_i[..., None] * bi
    bb_i = q_r[..., None] * bi + q_i[..., None] * br
    pr, pi = [jnp.ones_like(a_r)], [jnp.zeros_like(a_i)]
    for _ in range(T):
        pr.append(pr[-1] * a_r - pi[-1] * a_i)
        pi.append(pr[-2] * a_i + pi[-1] * a_r)
    pw_r, pw_i = jnp.stack(pr), jnp.stack(pi)
    cr, ci = c_re.astype(f32), c_im.astype(f32)
    ca_r = cr[None] * pw_r[:, :, :, None, :] - ci[None] * pw_i[:, :, :, None, :]
    ca_i = cr[None] * pw_i[:, :, :, None, :] + ci[None] * pw_r[:, :, :, None, :]
    kk = (jnp.einsum('mdgop,dgph->mdgoh', ca_r[:T], bb_r, precision=HIGHEST)
          - jnp.einsum('mdgop,dgph->mdgoh', ca_i[:T], bb_i, precision=HIGHEST))
    kf, kb = kk[:, 0], kk[:, 1]
    G, H, P = cr.shape[1], cr.shape[2], cr.shape[3]
    lagk = jnp.concatenate([jnp.flip(kb[1:], 0), kf[:1] + kb[:1], kf[1:]], axis=0)
    idx = (np.arange(T)[None, :] - np.arange(T)[:, None]) + (T - 1)
    toe = lagk[idx]
    m_mat = jnp.transpose(toe, (2, 0, 4, 1, 3)).reshape(G, T * H, T * H)
    def cmul_pb(pwr, pwi, d_):
        er = pwr[..., None] * bb_r[d_][None] - pwi[..., None] * bb_i[d_][None]
        ei = pwr[..., None] * bb_i[d_][None] + pwi[..., None] * bb_r[d_][None]
        tr = lambda v: jnp.transpose(v, (1, 0, 3, 2)).reshape(G, T * H, P)
        return tr(er), tr(ei)
    ef_r, ef_i = cmul_pb(jnp.flip(pw_r[:T, 0], 0), jnp.flip(pw_i[:T, 0], 0), 0)
    eb_r, eb_i = cmul_pb(pw_r[:T, 1], pw_i[:T, 1], 1)
    e_mat = jnp.concatenate([ef_r, eb_r, ef_i, eb_i], axis=-1)
    def carry(car, cai):
        rr = jnp.transpose(car, (1, 3, 0, 2)).reshape(G, P, T * H)
        ri = jnp.transpose(-cai, (1, 3, 0, 2)).reshape(G, P, T * H)
        return rr, ri
    cf_r, cf_i = carry(ca_r[1:T + 1, 0], ca_i[1:T + 1, 0])
    cb_r, cb_i = carry(jnp.flip(ca_r[1:T + 1, 1], 0), jnp.flip(ca_i[1:T + 1, 1], 0))
    zp = jnp.zeros_like(cf_r)
    wf = jnp.concatenate([cf_r, zp, cf_i, zp], axis=1)
    wb = jnp.concatenate([zp, cb_r, zp, cb_i], axis=1)
    w_out = jnp.concatenate([m_mat, wf, wb], axis=1)
    at_r = jnp.concatenate([pw_r[T, 0], pw_r[T, 1]], axis=-1)[:, None, :]
    at_i = jnp.concatenate([pw_i[T, 0], pw_i[T, 1]], axis=-1)[:, None, :]
    d_t = jnp.tile(d.astype(f32).reshape(G, 1, H), (1, 1, T))
    return e_mat.astype(BF16), w_out.astype(BF16), at_r, at_i, d_t


def _s5_kernel(x_ref, e_ref, w_ref, ar_ref, ai_ref, d_ref, init_ref, y_ref, fin_ref,
               e_scr, sa_scr, sb_scr, *, gb, nb, nc, p2):
    nt = nc // 2
    lane = lax.broadcasted_iota(jnp.int32, (2 * nb, p2), 1)
    is_fwd = lane < (p2 // 2)
    first = lax.broadcasted_iota(jnp.int32, (2 * nb, p2), 0) < nb
    swap = lambda v: pltpu.roll(v, nb, axis=0)
    for g in range(gb):
        e_scr[g] = _dot(x_ref[g], e_ref[g])

    def step(k, carry):
        out = []
        rf = pl.ds(pl.multiple_of(k * 2 * nb, 2 * nb), 2 * nb)
        rb = pl.ds(pl.multiple_of((nt - 1 - k) * 2 * nb, 2 * nb), 2 * nb)
        for g in range(gb):
            cr, ci = carry[2 * g], carry[2 * g + 1]
            ar, ai = ar_ref[g], ai_ref[g]
            er = jnp.where(is_fwd, e_scr[g, rf, 0:p2], swap(e_scr[g, rb, 0:p2]))
            ei = jnp.where(is_fwd, e_scr[g, rf, p2:2 * p2], swap(e_scr[g, rb, p2:2 * p2]))
            ur = ar * cr - ai * ci + er
            ui = ar * ci + ai * cr + ei
            ur4, ui4 = swap(ur), swap(ui)
            sr = jnp.where(first, cr, ur4)
            si = jnp.where(first, ci, ui4)
            sa_scr[g, rf, 0:p2] = sr
            sa_scr[g, rf, p2:2 * p2] = si
            sb_scr[g, rb, 0:p2] = swap(sr)
            sb_scr[g, rb, p2:2 * p2] = swap(si)
            xr = jnp.where(first, ur, ur4)
            xi = jnp.where(first, ui, ui4)
            zr = ar * xr - ai * xi + er
            zi = ar * xi + ai * xr + ei
            out.append(jnp.where(first, swap(zr), zr))
            out.append(jnp.where(first, swap(zi), zi))
        return tuple(out)

    init = []
    for g in range(gb):
        init += [init_ref[g, :, 0:p2], init_ref[g, :, p2:2 * p2]]
    fin = lax.fori_loop(0, nt, step, tuple(init))
    for g in range(gb):
        fin_ref[g] = jnp.concatenate([fin[2 * g], fin[2 * g + 1]], axis=1)
        x = x_ref[g]
        lhs = jnp.concatenate([x, sa_scr[g].astype(BF16), sb_scr[g].astype(BF16)], axis=1)
        y = _dot(lhs, w_ref[g]) + x.astype(F32) * d_ref[g]
        y_ref[g] = jax.nn.gelu(y).astype(BF16)


def _s5_scan(xg, mats, init, nb):
    e_mat, w_out, at_r, at_i, d_t = mats
    G, R, th = xg.shape
    p4 = e_mat.shape[-1]
    p2 = p4 // 2
    gb = 2
    nc = R // nb
    assert 2 * nb == 8 and nc % 2 == 0, "two chunks of batch rows must fill one 8-sublane tile"
    kern = functools.partial(_s5_kernel, gb=gb, nb=nb, nc=nc, p2=p2)
    g3 = lambda i: (i, 0, 0)
    return pl.pallas_call(
        kern,
        grid=(G // gb,),
        in_specs=[pl.BlockSpec((gb, R, th), g3),
                  pl.BlockSpec((gb, th, p4), g3),
                  pl.BlockSpec((gb, th + 2 * p4, th), g3),
                  pl.BlockSpec((gb, 1, p2), g3),
                  pl.BlockSpec((gb, 1, p2), g3),
                  pl.BlockSpec((gb, 1, th), g3),
                  pl.BlockSpec((gb, 2 * nb, p4), g3)],
        out_specs=[pl.BlockSpec((gb, R, th), g3),
                   pl.BlockSpec((gb, 2 * nb, p4), g3)],
        out_shape=[jax.ShapeDtypeStruct((G, R, th), BF16),
                   jax.ShapeDtypeStruct((G, 2 * nb, p4), F32)],
        scratch_shapes=[pltpu.VMEM((gb, R, p4), F32),
                        pltpu.VMEM((gb, R, p4), F32),
                        pltpu.VMEM((gb, R, p4), F32)],
        compiler_params=_cparams("parallel"),
        name="s5_scan",
    )(xg, e_mat, w_out, at_r, at_i, d_t, init)


def _lane_group(rows, h):
    return lax.broadcasted_iota(jnp.int32, (rows, LANES), 1) // h


def _s5_pack_kernel(u_ref, o_ref, t_scr, *, h, rows):
    gl = LANES // h
    t_scr[...] = _swap_major(u_ref[...])
    pr = min(rows, PACK_ROWS)
    grp = _lane_group(pr, h)
    for rb in range(rows // pr):
        rs = slice(rb * pr, (rb + 1) * pr)
        for half in range(S5_CHUNK // gl):
            acc = [None] * gl
            for jj in range(gl):
                v = t_scr[half * gl + jj, rs, :]
                for g in range(gl):
                    r = pltpu.roll(v, ((jj - g) * h) % LANES, axis=1) if jj != g else v
                    acc[g] = r if acc[g] is None else jnp.where(grp == jj, r, acc[g])
            for g in range(gl):
                o_ref[g, rs, half * LANES:(half + 1) * LANES] = acc[g].astype(BF16)


def _s5_unpack_kernel(y_ref, o_ref, t_scr, *, h, rows):
    gl = LANES // h
    pr = min(rows, PACK_ROWS)
    grp = _lane_group(pr, h)
    for rb in range(rows // pr):
        rs = slice(rb * pr, (rb + 1) * pr)
        for half in range(S5_CHUNK // gl):
            ys =[y_ref[g, rs, half * LANES:(half + 1) * LANES].astype(F32) for g in range(gl)]
            for tt in range(gl):
                acc = None
                for g in range(gl):
                    r = pltpu.roll(ys[g], ((g - tt) * h) % LANES, axis=1) if g != tt else ys[g]
                    acc = r if acc is None else jnp.where(grp == g, r, acc)
                t_scr[half * gl + tt, rs, :] = acc
    o_ref[...] = _swap_major(t_scr[...])


def _s5_pack(u3, h):
    R, t, w = u3.shape
    gl = LANES // h
    rows = min(R, 256)
    kern = functools.partial(_s5_pack_kernel, h=h, rows=rows)
    return pl.pallas_call(
        kern,
        grid=(w // LANES, R // rows),
        in_specs=[pl.BlockSpec((rows, t, LANES), lambda l, i: (i, 0, l))],
        out_specs=pl.BlockSpec((gl, rows, t * h), lambda l, i: (l, i, 0)),
        out_shape=jax.ShapeDtypeStruct((w // h, R, t * h), BF16),
        scratch_shapes=[pltpu.VMEM((t, rows, LANES), F32)],
        compiler_params=_cparams("parallel", "parallel"),
        name="s5_pack",
    )(u3)


def _s5_unpack(yg, h):
    G, R, th = yg.shape
    t = th // h
    gl = LANES // h
    rows = min(R, 256)
    kern = functools.partial(_s5_unpack_kernel, h=h, rows=rows)
    return pl.pallas_call(
        kern,
        grid=(G // gl, R // rows),
        in_specs=[pl.BlockSpec((gl, rows, th), lambda l, i: (l, i, 0))],
        out_specs=pl.BlockSpec((rows, t, LANES), lambda l, i: (i, 0, l)),
        out_shape=jax.ShapeDtypeStruct((R, t, G * h), F32),
        scratch_shapes=[pltpu.VMEM((t, rows, LANES), F32)],
        compiler_params=_cparams("parallel", "parallel"),
        name="s5_unpack",
    )(yg)


def _s5_mixer(u4, mats, init, h):
    lc, nb, t, w = u4.shape
    yg, fin = _s5_scan(_s5_pack(u4.reshape(lc * nb, t, w), h), mats, init, nb)
    return _s5_unpack(yg, h).reshape(lc, nb, t, w), fin


def _outproj_kernel(yh_ref, ys_ref, x_ref, mod_ref, g_ref, wg_ref, bg_ref, wo_ref,
                    xo_ref, hx_ref, *, d, dh, tm):
    gate = mod_ref[0, :, 2 * d:3 * d]
    shift = mod_ref[0, :, 3 * d:4 * d]
    scale = mod_ref[0, :, 4 * d:5 * d]
    sub = tm // SUB_TILES
    rows = [slice(s * sub, (s + 1) * sub) for s in range(SUB_TILES)]
    ys = [ys_ref[s * sub // S5_CHUNK:(s + 1) * sub // S5_CHUNK, 0, :, :].reshape(sub, ys_ref.shape[-1])
          for s in range(SUB_TILES)]
    pre = [_dot(y.astype(BF16), wg_ref[0]) for y in ys]
    glu = [(y * jax.nn.sigmoid(a + bg_ref[0])).astype(BF16) for y, a in zip(ys, pre)]
    yx = [_dot(yh_ref[0, rs, :], wo_ref[0, 0:dh, :]) + _dot(gl, wo_ref[0, dh:, :]) for rs, gl in zip(rows, glu)]
    for rs, v in zip(rows, yx):
        xo = x_ref[0, rs, :] + gate * _rms(v, g_ref[0, 1:2, :])
        xo_ref[0, rs, :] = xo
        hx_ref[0, rs, :] = (_rms(xo, g_ref[0, 2:3, :]) * (1.0 + scale) + shift).astype(BF16)


def _outproj(yh, ys4, x, mod, mod_row, layer, norm_g, w_glu, b_glu, w_out, tm):
    b, s, d = x.shape
    dh = yh.shape[-1]
    ds5 = ys4.shape[-1]
    tc = tm // S5_CHUNK
    kern = functools.partial(_outproj_kernel, d=d, dh=dh, tm=tm)
    lyr = lambda bi, i: (layer, 0, 0)
    t3 = lambda bi, i: (bi, i, 0)
    return pl.pallas_call(
        kern,
        grid=(b, s // tm),
        in_specs=[pl.BlockSpec((1, tm, dh), t3),
                  pl.BlockSpec((tc, 1, S5_CHUNK, ds5), lambda bi, i: (i, bi, 0, 0)),
                  pl.BlockSpec((1, tm, d), t3),
                  pl.BlockSpec((1, 1, mod.shape[-1]), lambda bi, i: (mod_row(bi), 0, 0)),
                  pl.BlockSpec((1,) + norm_g.shape[1:], lyr),
                  pl.BlockSpec((1, ds5, ds5), lyr), pl.BlockSpec((1, 1, ds5), lyr),
                  pl.BlockSpec((1, dh + ds5, d), lyr)],
        out_specs=[pl.BlockSpec((1, tm, d), t3), pl.BlockSpec((1, tm, d), t3)],
        out_shape=[jax.ShapeDtypeStruct((b, s, d), F32), jax.ShapeDtypeStruct((b, s, d), BF16)],
        compiler_params=_cparams("parallel", "parallel"),
        name="outproj",
    )(yh, ys4, x, mod, norm_g, w_glu, b_glu, w_out)


def _ffn_kernel(hm_ref, hp_ref, hn_ref, x_ref, mod_ref, g_ref, wg_ref, wv_ref, cw_ref, cb_ref, wd_ref,
                o_ref, acc_ref, *, tm, d, wg, vertical):
    i = pl.program_id(1)
    nt = pl.num_programs(1)
    j = pl.program_id(2)
    nj = pl.num_programs(2)
    hm = hm_ref[0]
    cw = cw_ref[0]
    if vertical:
        top = jnp.where(i > 0, 1.0, 0.0).astype(BF16)
        bot = jnp.where(i < nt - 1, 1.0, 0.0).astype(BF16)
        ha = jnp.concatenate([hp_ref[0] * top, hm, hn_ref[0] * bot], axis=0)
    else:
        ha = hm
    g = _dot(ha, wg_ref[0])

    ns = FFN_SUB_TILES if tm % (FFN_SUB_TILES * wg) == 0 else 1
    sub = tm // ns

    def vcol(dx, r0):
        if not vertical:
            return g[r0:r0 + sub] * cw[3 + dx:4 + dx]
        return (g[r0:r0 + sub] * cw[dx:dx + 1] + g[r0 + wg:r0 + wg + sub] * cw[3 + dx:4 + dx]
                + g[r0 + 2 * wg:r0 + 2 * wg + sub] * cw[6 + dx:7 + dx])

    col = lax.broadcasted_iota(jnp.int32, (sub, g.shape[-1]), 0) & (wg - 1)
    vs = [_dot(hm[s * sub:(s + 1) * sub], wv_ref[0]) for s in range(ns)]
    hmid = []
    for s in range(ns):
        r0 = s * sub
        conv = (vcol(1, r0) + jnp.where(col > 0, pltpu.roll(vcol(0, r0), 1, axis=0), 0.0)
                + jnp.where(col < wg - 1, pltpu.roll(vcol(2, r0), sub - 1, axis=0), 0.0) + cb_ref[0])
        hmid.append((jax.nn.gelu(conv) * vs[s]).astype(BF16))
    parts = [_dot(h, wd_ref[0]) for h in hmid]

    @pl.when(j == 0)
    def _():
        for s in range(ns):
            acc_ref[s * sub:(s + 1) * sub, :] = parts[s]

    @pl.when(j > 0)
    def _():
        for s in range(ns):
            acc_ref[s * sub:(s + 1) * sub, :] += parts[s]

    @pl.when(j == nj - 1)
    def _():
        gate = mod_ref[0, :, 5 * d:6 * d]
        o_ref[0] = x_ref[0] + gate * _rms(acc_ref[...], g_ref[0, 3:4, :])


def _ffn(hx, x, mod, mod_row, layer, norm_g, w_up, conv_w, conv_b, w_down, tm, wg, vertical):
    b, s, d = x.shape
    f = w_down.shape[1]
    fc = f // 2 if (f // 2) % LANES == 0 else f
    nf = f // fc
    nt = s // tm
    r = tm // wg if vertical else 1
    hb = wg if vertical else 16
    nhb = s // hb
    kern = functools.partial(_ffn_kernel, tm=tm, d=d, wg=wg, vertical=vertical)
    t3 = lambda bi, i, j: (bi, i, 0)
    return pl.pallas_call(
        kern,
        grid=(b, nt, nf),
        in_specs=[pl.BlockSpec((1, tm, d), t3),
                  pl.BlockSpec((1, hb, d), lambda bi, i, j: (bi, jnp.maximum(i * r - 1, 0), 0)),
                  pl.BlockSpec((1, hb, d), lambda bi, i, j: (bi, jnp.minimum((i + 1) * r, nhb - 1), 0)),
                  pl.BlockSpec((1, tm, d), t3),
                  pl.BlockSpec((1, 1, mod.shape[-1]), lambda bi, i, j: (mod_row(bi), 0, 0)),
                  pl.BlockSpec((1,) + norm_g.shape[1:], lambda bi, i, j: (layer, 0, 0)),
                  pl.BlockSpec((1, d, fc), lambda bi, i, j: (layer, 0, j)),
                  pl.BlockSpec((1, d, fc), lambda bi, i, j: (layer, 0, nf + j)),
                  pl.BlockSpec((1, 9, fc), lambda bi, i, j: (layer, 0, j)),
                  pl.BlockSpec((1, 1, fc), lambda bi, i, j: (layer, 0, j)),
                  pl.BlockSpec((1, fc, d), lambda bi, i, j: (layer, j, 0))],
        out_specs=pl.BlockSpec((1, tm, d), t3),
        out_shape=jax.ShapeDtypeStruct((b, s, d), F32),
        scratch_shapes=[pltpu.VMEM((tm, d), F32)],
        compiler_params=_cparams("parallel", "parallel", "arbitrary"),
        name="conv_glu_ffn",
    )(hx, hx, hx, x, mod, norm_g, w_up, w_up, conv_w, conv_b, w_down)


def kernel(x, c, ctx, c_ctx, w_ada, b_ada, norm_g, w_in, hy_short_w, hy_short_b,
           filt_w_in, filt_b_in, filt_w_hid, filt_b_hid, filt_freq, filt_w_out, hy_bias,
           s5_lam_re, s5_lam_im, s5_log_step, s5_b_re, s5_b_im, s5_c_re, s5_c_im, s5_d,
           s5_w_glu, s5_b_glu, w_out, ffn_w_up, ffn_conv_w, ffn_conv_b, ffn_w_down):
    depth = w_ada.shape[0]
    bsz, seq, d = x.shape
    lctx = ctx.shape[1]
    dh = hy_bias.shape[-1]
    G, P, H = s5_b_re.shape[2], s5_b_re.shape[3], s5_b_re.shape[4]
    dff = ffn_w_down.shape[1]
    assert bsz % 2 == 0 and bsz <= 4 and seq % GRID_W == 0 and GRID_W & (GRID_W - 1) == 0

    cond = jnp.zeros((8, d), F32).at[:bsz].set(c).at[bsz].set(c_ctx)
    mod = _ada_mod(cond, w_ada, b_ada).reshape(depth * 8, 1, 6 * d)

    filt_args = (filt_w_in, filt_b_in, filt_w_hid, filt_b_hid, filt_freq, filt_w_out)
    long_conv = _LongConv(seq, dh, _hyena_filters(seq, *filt_args, dh))
    k_ctx = _hyena_filters(lctx, *filt_args, dh)
    short_c = tuple(jnp.asarray(a, BF16) for a in _short_consts(lctx))

    w_in_b = w_in.astype(BF16)
    w_glu_b = s5_w_glu.astype(BF16)
    w_out_b = w_out.astype(BF16)
    w_up_b = ffn_w_up.astype(BF16)
    w_down_b = ffn_w_down.astype(BF16)
    sb = hy_short_b.reshape(depth, 1, 3 * dh)
    cw = ffn_conv_w.reshape(depth, 9, dff)
    cb = ffn_conv_b.reshape(depth, 1, dff)
    bg = s5_b_glu.reshape(depth, 1, -1)
    hb = hy_bias.reshape(depth, 1, dh)

    tm = min(seq, ROW_TILE)
    for l in range(depth):
        last = l == depth - 1
        row_x = lambda bi, l=l: 8 * l + bi
        row_c = lambda bi, l=l: 8 * l + bsz
        mats = _s5_matrices(s5_lam_re[l], s5_lam_im[l], s5_log_step[l], s5_b_re[l], s5_b_im[l],
                            s5_c_re[l], s5_c_im[l], s5_d[l])

        x0c, zc, uc = _inproj(ctx, mod, row_c, l, norm_g, w_in_b, hy_short_w, sb, dh, lctx)
        ysc, ctx_state = _s5_mixer(uc, mats, jnp.zeros((G, 2 * bsz, 4 * P), F32), H)

        x0, z, u = _inproj(x, mod, row_x, l, norm_g, w_in_b, hy_short_w, sb, dh, tm)
        ys, _ = _s5_mixer(u, mats, ctx_state, H)
        yh = long_conv(z, x0, hb, l)
        x, hx = _outproj(yh, ys, x, mod, row_x, l, norm_g, w_glu_b, bg, w_out_b, tm)
        x = _ffn(hx, x, mod, row_x, l, norm_g, w_up_b, cw, cb, w_down_b, tm, GRID_W, True)

        if not last:
            yhc = _short_conv(zc, x0c, k_ctx, l, short_c, hb)
            ctx, hc = _outproj(yhc, ysc, ctx, mod, row_c, l, norm_g, w_glu_b, bg, w_out_b, lctx)
            ctx = _ffn(hc, ctx, mod, row_c, l, norm_g, w_up_b, cw, cb, w_down_b, lctx, lctx, False)
    return x
```

```python
import functools
import math

import numpy as np
import jax
import jax.numpy as jnp
from jax import lax
from jax.experimental import pallas as pl
from jax.experimental.pallas import tpu as pltpu

GRID_W = 64
RMS_EPS = 1e-6
DECAY_TARGET = 1e-2
FAST_DECAY_PCT = 0.3
SLOW_DECAY_PCT = 1.5
S5_CHUNK = 16
LANES = 128
HALO = 16
ROW_TILE = 512
PROJ_ROW_TILE = 1024
SUB_TILES = 4
IN_SUB_TILES = 2
FFN_SUB_TILES = 1
PACK_ROWS = 32
VMEM_LIMIT = 56 * 1024 * 1024

F32 = jnp.float32
BF16 = jnp.bfloat16
HIGHEST = lax.Precision.HIGHEST


def _cparams(*sem):
    return pltpu.CompilerParams(dimension_semantics=sem, vmem_limit_bytes=VMEM_LIMIT)


def _dot(a, b, **kw):
    return jnp.dot(a, b, preferred_element_type=F32, **kw)


def _ada_kernel(cond_ref, w_ref, b_ref, o_ref):
    cv = cond_ref[...]
    s = cv * jax.nn.sigmoid(cv)
    o_ref[0] = _dot(s, w_ref[0], precision=HIGHEST) + b_ref[0]


def _ada_mod(cond, w_ada, b_ada):
    depth, d, n = w_ada.shape
    tn = n // 4
    return pl.pallas_call(
        _ada_kernel,
        grid=(depth, n // tn),
        in_specs=[pl.BlockSpec((8, d), lambda l, j: (0, 0)),
                  pl.BlockSpec((1, d, tn), lambda l, j: (l, 0, j)),
                  pl.BlockSpec((1, 1, tn), lambda l, j: (l, 0, j))],
        out_specs=pl.BlockSpec((1, 8, tn), lambda l, j: (l, 0, j)),
        out_shape=jax.ShapeDtypeStruct((depth, 8, n), F32),
        compiler_params=_cparams("parallel", "parallel"),
        name="ada_mod",
    )(cond, w_ada, b_ada.reshape(depth, 1, n))


def _rms(v, g):
    ms = jnp.mean(v * v, axis=-1, keepdims=True)
    return v * lax.rsqrt(ms + RMS_EPS) * g


def _inproj_kernel(xm_ref, xp_ref, xn_ref, mod_ref, g_ref, w_ref, sw_ref, sb_ref,
                   x0_ref, z_ref, u_ref, p_scr, *, tm, d, dh):
    i = pl.program_id(1)
    nt = pl.num_programs(1)
    shift = mod_ref[0, :, 0:d]
    scale = mod_ref[0, :, d:2 * d]
    xa = jnp.concatenate([xp_ref[0], xm_ref[0], xn_ref[0]], axis=0)
    sw = sw_ref[0]
    ns = IN_SUB_TILES
    sub = tm // ns
    cut = [0] + [HALO + s * sub for s in range(1, ns)] + [tm + 2 * HALO]

    def project(s):
        lo, hi = cut[s], cut[s + 1]
        xn = (_rms(xa[lo:hi], g_ref[0, 0:1, :]) * (1.0 + scale) + shift).astype(BF16)
        p = _dot(xn, w_ref[0])
        p_scr[lo:hi, :] = p[:, :3 * dh]
        if s == 0:
            inside = jnp.where(i > 0, 1.0, 0.0).astype(F32)
            p_scr[HALO - 8:HALO, :] = p[HALO - 8:HALO, :3 * dh] * inside
        if s == ns - 1:
            inside = jnp.where(i < nt - 1, 1.0, 0.0).astype(F32)
            p_scr[HALO + tm:HALO + tm + 8, :] = p[HALO + tm - lo:HALO + tm + 8 - lo, :3 * dh] * inside
        a, b = max(lo, HALO), min(hi, HALO + tm)
        u = p[a - lo:b - lo, 3 * dh:]
        u_ref[(a - HALO) // S5_CHUNK:(b - HALO) // S5_CHUNK, 0, :, :] = u.reshape(
            (b - a) // S5_CHUNK, S5_CHUNK, u.shape[-1])

    def conv_gate(s):
        r0 = HALO + s * sub
        conv = (p_scr[pl.ds(r0 - 1, sub), :] * sw[0:1] + p_scr[pl.ds(r0, sub), :] * sw[1:2]
                + p_scr[pl.ds(r0 + 1, sub), :] * sw[2:3] + sb_ref[0])
        rs = slice(s * sub, (s + 1) * sub)
        x0_ref[0, rs, :] = conv[:, :dh].astype(BF16)
        z_ref[0, rs, :] = (conv[:, dh:2 * dh] * conv[:, 2 * dh:]).astype(BF16)

    project(0)
    for s in range(ns):
        if s + 1 < ns:
            project(s + 1)
        conv_gate(s)


def _inproj(x, mod, mod_row, layer, norm_g, w_in, sw, sb, dh, tm):
    b, s, d = x.shape
    dp = w_in.shape[-1]
    ds5 = dp - 3 * dh
    nt = s // tm
    r = tm // HALO
    nh = s // HALO
    tc = tm // S5_CHUNK
    kern = functools.partial(_inproj_kernel, tm=tm, d=d, dh=dh)
    lyr = lambda bi, i: (layer, 0, 0)
    return pl.pallas_call(
        kern,
        grid=(b, nt),
        in_specs=[pl.BlockSpec((1, tm, d), lambda bi, i: (bi, i, 0)),
                  pl.BlockSpec((1, HALO, d), lambda bi, i: (bi, jnp.maximum(i * r - 1, 0), 0)),
                  pl.BlockSpec((1, HALO, d), lambda bi, i: (bi, jnp.minimum((i + 1) * r, nh - 1), 0)),
                  pl.BlockSpec((1, 1, mod.shape[-1]), lambda bi, i: (mod_row(bi), 0, 0)),
                  pl.BlockSpec((1,) + norm_g.shape[1:], lyr),
                  pl.BlockSpec((1, d, dp), lyr),
                  pl.BlockSpec((1, 3, 3 * dh), lyr),
                  pl.BlockSpec((1, 1, 3 * dh), lyr)],
        out_specs=[pl.BlockSpec((1, tm, dh), lambda bi, i: (bi, i, 0)),
                   pl.BlockSpec((1, tm, dh), lambda bi, i: (bi, i, 0)),
                   pl.BlockSpec((tc, 1, S5_CHUNK, ds5), lambda bi, i: (i, bi, 0, 0))],
        out_shape=[jax.ShapeDtypeStruct((b, s, dh), BF16),
                   jax.ShapeDtypeStruct((b, s, dh), BF16),
                   jax.ShapeDtypeStruct((s // S5_CHUNK, b, S5_CHUNK, ds5), F32)],
        scratch_shapes=[pltpu.VMEM((tm + 2 * HALO, 3 * dh), F32)],
        compiler_params=_cparams("parallel", "arbitrary"),
        name="inproj",
    )(x, x, x, mod, norm_g, w_in, sw, sb)


def _filter_feats(L, emb):
    bands = (emb - 1) // 2
    t = np.linspace(0.0, 1.0, L, dtype=np.float32).astype(np.float64)[:, None]
    w = (2.0 * math.pi / L) * np.arange(L, dtype=np.float64)[:, None]
    f = np.linspace(1e-4, bands - 1, bands, dtype=np.float32).astype(np.float64)[None, :]
    z = np.concatenate([t, np.cos(f * w), -np.sin(f * w)], axis=-1)
    zp = np.zeros((L, LANES), np.float32)
    zp[:, :emb] = z
    return zp


def _filter_kernel(z_ref, win_ref, bin_ref, whid_ref, bhid_ref, fr_ref, wf_ref, wb_ref, dl_ref, o_ref):
    z = z_ref[...]
    fr = fr_ref[0]
    h = jnp.sin(fr * (_dot(z, win_ref[0], precision=HIGHEST) + bin_ref[0]))
    for i in range(whid_ref.shape[1]):
        h = jnp.sin(fr * (_dot(h, whid_ref[0, i], precision=HIGHEST) + bhid_ref[0, i]))
    o_ref[0, 0] = _dot(h, wf_ref[0], precision=HIGHEST) * jnp.exp(-z[:, 0:1] * dl_ref[...])
    hb = _dot(h, wb_ref[0], precision=HIGHEST) * jnp.exp(-z[:, LANES:LANES + 1] * dl_ref[...])
    first = (pl.program_id(1) == 0) & (lax.broadcasted_iota(jnp.int32, (z.shape[0], 1), 0) == 0)
    o_ref[0, 1] = jnp.where(first, 0.0, hb)


def _block_diag2(w):
    zero = jnp.zeros_like(w)
    return jnp.concatenate([jnp.concatenate([w, zero], axis=-1), jnp.concatenate([zero, w], axis=-1)], axis=-2)


def _hyena_filters(L, f_w_in, f_b_in, f_w_hid, f_b_hid, f_freq, f_w_out, dh):
    depth, emb, hid = f_w_in.shape
    n_inner = f_w_hid.shape[1]
    tl = min(L, 1024)
    z1 = _filter_feats(L, emb)
    z = jnp.asarray(np.concatenate([z1, np.concatenate([z1[:1], z1[:0:-1]], axis=0)], axis=1))
    win = _block_diag2(jnp.zeros((depth, LANES, hid), F32).at[:, :emb].set(f_w_in))
    zero = jnp.zeros((depth, hid, dh), F32)
    wf = jnp.concatenate([f_w_out[:, :, :dh], zero], axis=1)
    wb = jnp.concatenate([zero, f_w_out[:, :, dh:]], axis=1)
    twice = lambda v, shp: jnp.tile(v.reshape(shp), (1,) * (len(shp) - 1) + (2,))
    deltas = np.abs(np.linspace(math.log(DECAY_TARGET) / FAST_DECAY_PCT,
                                math.log(DECAY_TARGET) / SLOW_DECAY_PCT, dh, dtype=np.float32))[None, :]
    h2 = 2 * hid
    out = pl.pallas_call(
        _filter_kernel,
        grid=(depth, L // tl),
        in_specs=[pl.BlockSpec((tl, 2 * LANES), lambda l, i: (i, 0)),
                  pl.BlockSpec((1, 2 * LANES, h2), lambda l, i: (l, 0, 0)),
                  pl.BlockSpec((1, 1, h2), lambda l, i: (l, 0, 0)),
                  pl.BlockSpec((1, n_inner, h2, h2), lambda l, i: (l, 0, 0, 0)),
                  pl.BlockSpec((1, n_inner, 1, h2), lambda l, i: (l, 0, 0, 0)),
                  pl.BlockSpec((1, 1, h2), lambda l, i: (l, 0, 0)),
                  pl.BlockSpec((1, h2, dh), lambda l, i: (l, 0, 0)),
                  pl.BlockSpec((1, h2, dh), lambda l, i: (l, 0, 0)),
                  pl.BlockSpec((1, dh), lambda l, i: (0, 0))],
        out_specs=pl.BlockSpec((1, 2, tl, dh), lambda l, i: (l, 0, i, 0)),
        out_shape=jax.ShapeDtypeStruct((depth, 2, L, dh), F32),
        compiler_params=_cparams("parallel", "parallel"),
        name="hyena_filter",
    )(z, win, twice(f_b_in, (depth, 1, hid)), _block_diag2(f_w_hid), twice(f_b_hid, (depth, n_inner, 1, hid)),
      twice(f_freq, (depth, 1, hid)), wf, wb, jnp.asarray(deltas))
    return out.reshape(depth, 2 * L, dh)


def _dft_consts(n1, n2):
    n = n1 * n2
    n2h = n2 // 2
    k2 = np.arange(n2)[:, None]
    a = -2.0 * np.pi * k2 * np.arange(n2)[None, :] / n2
    fr, fi = np.cos(a), np.sin(a)
    f1 = np.block([[fr[:, :n2h], -fi[:, :n2h]], [fi[:, :n2h], fr[:, :n2h]]])
    f1k = np.concatenate([fr, fi], axis=0)
    cr, ci = fr[:n2h] / n, -fi[:n2h] / n
    f3 = np.block([[cr, -ci], [ci, cr]])
    b = -2.0 * np.pi * np.arange(n1)[:, None] * np.arange(n1)[None, :] / n1
    t = -2.0 * np.pi * np.arange(n2)[:, None] * np.arange(n1)[None, :] / n
    return dict(f1=f1.astype(np.float32), f1k=f1k.astype(np.float32), f3=f3.astype(np.float32),
                gr=np.cos(b).astype(np.float32), gi=np.sin(b).astype(np.float32),
                tr=np.cos(t).astype(np.float32)[:, None, :], ti=np.sin(t).astype(np.float32)[:, None, :])


def _tables_kernel(gr_ref, gi_ref, tr_ref, ti_ref, m1_ref, m2_ref, *, kb):
    gr, gi = gr_ref[...], gi_ref[...]
    for q in range(kb):
        tr, ti = tr_ref[q], ti_ref[q]
        re = gr * tr - gi * ti
        im = gr * ti + gi * tr
        m1_ref[q] = jnp.concatenate([jnp.concatenate([re, -im], axis=1),
                                     jnp.concatenate([im, re], axis=1)], axis=0).astype(BF16)
        ret, imt = re.T, im.T
        m2_ref[q] = jnp.concatenate([jnp.concatenate([ret, imt], axis=1),
                                     jnp.concatenate([-imt, ret], axis=1)], axis=0).astype(BF16)


def _dft_tables(c, n1, n2):
    kb = 8
    kern = functools.partial(_tables_kernel, kb=kb)
    shp = jax.ShapeDtypeStruct((n2, 2 * n1, 2 * n1), BF16)
    return pl.pallas_call(
        kern,
        grid=(n2 // kb,),
        in_specs=[pl.BlockSpec((n1, n1), lambda i: (0, 0)),
                  pl.BlockSpec((n1, n1), lambda i: (0, 0)),
                  pl.BlockSpec((kb, 1, n1), lambda i: (i, 0, 0)),
                  pl.BlockSpec((kb, 1, n1), lambda i: (i, 0, 0))],
        out_specs=[pl.BlockSpec((kb, 2 * n1, 2 * n1), lambda i: (i, 0, 0)),
                   pl.BlockSpec((kb, 2 * n1, 2 * n1), lambda i: (i, 0, 0))],
        out_shape=[shp, shp],
        compiler_params=_cparams("parallel"),
        name="dft_tables",
    )(jnp.asarray(c["gr"]), jnp.asarray(c["gi"]), jnp.asarray(c["tr"]), jnp.asarray(c["ti"]))


def _swap_major(v):
    return pltpu.einshape("abc->bac", v)


def _s1_kernel(z_ref, f_ref, o_ref, y_scr, *, n2, nb1):
    zt = [_swap_major(z_ref[s].astype(F32)) for s in range(2)]
    for j in range(nb1):
        x = jnp.concatenate([zt[0][j], zt[1][j]], axis=0).astype(BF16)
        y = _dot(f_ref[...], x)
        y_scr[0, j] = y[:n2]
        y_scr[1, j] = y[n2:]
    for s in range(2):
        o_ref[0, s] = _swap_major(y_scr[s]).astype(BF16)


def _s1k_kernel(k_ref, f_ref, o_ref, y_scr, *, n2, nb1):
    kt = _swap_major(k_ref[0])
    for j in range(nb1):
        y = _dot(f_ref[...], kt[j].astype(BF16))
        y_scr[0, j] = y[:n2]
        y_scr[1, j] = y[n2:]
    for s in range(2):
        o_ref[0, s] = _swap_major(y_scr[s]).astype(BF16)


def _dft_stage1(z4, f1, n2, nb1):
    b, n2h, n1, c = z4.shape
    kern = functools.partial(_s1_kernel, n2=n2, nb1=nb1)
    return pl.pallas_call(
        kern,
        grid=(b // 2, n1 // nb1),
        in_specs=[pl.BlockSpec((2, n2h, nb1, c), lambda p, j: (p, 0, j, 0)),
                  pl.BlockSpec((2 * n2, 2 * n2h), lambda p, j: (0, 0))],
        out_specs=pl.BlockSpec((1, 2, n2, nb1, c), lambda p, j: (p, 0, 0, j, 0)),
        out_shape=jax.ShapeDtypeStruct((b // 2, 2, n2, n1, c), BF16),
        scratch_shapes=[pltpu.VMEM((2, nb1, n2, c), F32)],
        compiler_params=_cparams("parallel", "parallel"),
        name="dft_stage1",
    )(z4, f1)


def _dft_stage1_filter(k4, f1k, nb1):
    depth, n2, n1, c = k4.shape
    kern = functools.partial(_s1k_kernel, n2=n2, nb1=nb1)
    return pl.pallas_call(
        kern,
        grid=(depth, n1 // nb1),
        in_specs=[pl.BlockSpec((1, n2, nb1, c), lambda l, j: (l, 0, j, 0)),
                  pl.BlockSpec((2 * n2, n2), lambda l, j: (0, 0))],
        out_specs=pl.BlockSpec((1, 2, n2, nb1, c), lambda l, j: (l, 0, 0, j, 0)),
        out_shape=jax.ShapeDtypeStruct((depth, 2, n2, n1, c), BF16),
        scratch_shapes=[pltpu.VMEM((2, nb1, n2, c), F32)],
        compiler_params=_cparams("parallel", "parallel"),
        name="dft_stage1_filter",
    )(k4, f1k)


def _s2k_kernel(b_ref, m1_ref, o_ref, *, kb, n1):
    for q in range(kb):
        rows = slice(q * n1, (q + 1) * n1)
        xin = jnp.concatenate([b_ref[0, 0, rows, :], b_ref[0, 1, rows, :]], axis=0).astype(BF16)
        xf = _dot(m1_ref[q], xin)
        o_ref[0, 0, rows, :] = xf[:n1]
        o_ref[0, 1, rows, :] = xf[n1:]


def _dft_stage2_filter(bv, m1, n1, kb):
    depth, _, n, c = bv.shape
    kern = functools.partial(_s2k_kernel, kb=kb, n1=n1)
    return pl.pallas_call(
        kern,
        grid=(n // (kb * n1), depth),
        in_specs=[pl.BlockSpec((1, 2, kb * n1, c), lambda k, l: (l, 0, k, 0)),
                  pl.BlockSpec((kb, 2 * n1, 2 * n1), lambda k, l: (k, 0, 0))],
        out_specs=pl.BlockSpec((1, 2, kb * n1, c), lambda k, l: (l, 0, k, 0)),
        out_shape=jax.ShapeDtypeStruct((depth, 2, n, c), F32),
        compiler_params=_cparams("parallel", "parallel"),
        name="dft_stage2_filter",
    )(bv, m1)


def _s2_kernel(b_ref, m1_ref, kf_ref, m2_ref, o_ref, *, kb, n1):
    for q in range(kb):
        rows = slice(q * n1, (q + 1) * n1)
        xin = jnp.concatenate([b_ref[0, 0, rows, :], b_ref[0, 1, rows, :]], axis=0).astype(BF16)
        xf = _dot(m1_ref[q], xin)
        xr, xi = xf[:n1], xf[n1:]
        kr, ki = kf_ref[0, 0, rows, :], kf_ref[0, 1, rows, :]
        yin = jnp.concatenate([xr * kr - xi * ki, xr * ki + xi * kr], axis=0).astype(BF16)
        g = _dot(m2_ref[q], yin)
        o_ref[0, 0, rows, :] = g[:n1].astype(BF16)
        o_ref[0, 1, rows, :] = g[n1:].astype(BF16)


def _dft_stage2(bv, m1, kf, layer, m2, n1, kb):
    p, _, n, c = bv.shape
    kern = functools.partial(_s2_kernel, kb=kb, n1=n1)
    return pl.pallas_call(
        kern,
        grid=(n // (kb * n1), p),
        in_specs=[pl.BlockSpec((1, 2, kb * n1, c), lambda k, q: (q, 0, k, 0)),
                  pl.BlockSpec((kb, 2 * n1, 2 * n1), lambda k, q: (k, 0, 0)),
                  pl.BlockSpec((1, 2, kb * n1, c), lambda k, q: (layer, 0, k, 0)),
                  pl.BlockSpec((kb, 2 * n1, 2 * n1), lambda k, q: (k, 0, 0))],
        out_specs=pl.BlockSpec((1, 2, kb * n1, c), lambda k, q: (q, 0, k, 0)),
        out_shape=jax.ShapeDtypeStruct((p, 2, n, c), BF16),
        compiler_params=_cparams("parallel", "arbitrary"),
        name="dft_stage2",
    )(bv, m1, kf, m2)


def _s3_kernel(g_ref, f_ref, z_ref, x0_ref, bias_ref, o_ref, y_scr, *, n2h, nb1):
    gt = [_swap_major(g_ref[0, s].astype(F32)) for s in range(2)]
    for j in range(nb1):
        gin = jnp.concatenate([gt[0][j], gt[1][j]], axis=0).astype(BF16)
        y = _dot(f_ref[...], gin)
        y_scr[0, j] = y[:n2h]
        y_scr[1, j] = y[n2h:]
    for s in range(2):
        yh = _swap_major(y_scr[s]) + z_ref[s].astype(F32) * bias_ref[0]
        o_ref[s] = (x0_ref[s].astype(F32) * yh).astype(BF16)


def _dft_stage3(g5, f3, z4, x04, bias, layer, nb1):
    p, _, n2, n1, c = g5.shape
    n2h = n2 // 2
    kern = functools.partial(_s3_kernel, n2h=n2h, nb1=nb1)
    return pl.pallas_call(
        kern,
        grid=(p, n1 // nb1),
        in_specs=[pl.BlockSpec((1, 2, n2, nb1, c), lambda q, j: (q, 0, 0, j, 0)),
                  pl.BlockSpec((2 * n2h, 2 * n2), lambda q, j: (0, 0)),
                  pl.BlockSpec((2, n2h, nb1, c), lambda q, j: (q, 0, j, 0)),
                  pl.BlockSpec((2, n2h, nb1, c), lambda q, j: (q, 0, j, 0)),
                  pl.BlockSpec((1, 1, c), lambda q, j: (layer, 0, 0))],
        out_specs=pl.BlockSpec((2, n2h, nb1, c), lambda q, j: (q, 0, j, 0)),
        out_shape=jax.ShapeDtypeStruct((2 * p, n2h, n1, c), BF16),
        scratch_shapes=[pltpu.VMEM((2, nb1, n2h, c), F32)],
        compiler_params=_cparams("parallel", "parallel"),
        name="dft_stage3",
    )(g5, f3, z4, x04, bias)


class _LongConv:
    def __init__(self, L, c, k2s):
        n = 2 * L
        n1 = 1 << (int(math.log2(n)) // 2)
        n2 = n // n1
        assert n1 * n2 == n and n1 == n2, "long-convolution path needs 2L to be a square power of two"
        self.L, self.c, self.n1, self.n2 = L, c, n1, n2
        self.nb1 = min(n1, 16)
        self.kb = 4
        cst = _dft_consts(n1, n2)
        self.f1 = jnp.asarray(cst["f1"], BF16)
        self.f3 = jnp.asarray(cst["f3"], BF16)
        self.m1, self.m2 = _dft_tables(cst, n1, n2)
        depth = k2s.shape[0]
        bk = _dft_stage1_filter(k2s.reshape(depth, n2, n1, c), jnp.asarray(cst["f1k"], BF16), self.nb1)
        self.kf = _dft_stage2_filter(bk.reshape(depth, 2, n, c), self.m1, n1, self.kb)

    def __call__(self, z, x0, bias, layer):
        b, L, c = z.shape
        n1, n2 = self.n1, self.n2
        z4 = z.reshape(b, n2 // 2, n1, c)
        x04 = x0.reshape(b, n2 // 2, n1, c)
        b5 = _dft_stage1(z4, self.f1, n2, self.nb1)
        gv = _dft_stage2(b5.reshape(b // 2, 2, n1 * n2, c), self.m1, self.kf, layer, self.m2, n1, self.kb)
        y = _dft_stage3(gv.reshape(b // 2, 2, n2, n1, c), self.f3, z4, x04, bias, layer, self.nb1)
        return y.reshape(b, L, c)


def _short_consts(L):
    n = 2 * L
    a = -2.0 * np.pi * np.arange(n)[:, None] * np.arange(n)[None, :] / n
    fr, fi = np.cos(a), np.sin(a)
    ff = np.block([[fr[:, :L], -fi[:, :L]], [fi[:, :L], fr[:, :L]]])
    fk = np.concatenate([fr, fi], axis=0)
    cr, ci = fr[:L] / n, -fi[:L] / n
    finv = np.block([[cr, -ci], [ci, cr]])
    return ff.astype(np.float32), fk.astype(np.float32), finv.astype(np.float32)


def _short_conv_kernel(z_ref, x0_ref, k_ref, ff_ref, fk_ref, fi_ref, bias_ref, o_ref, *, L):
    n = 2 * L
    x = jnp.concatenate([z_ref[0], z_ref[1]], axis=0)
    xf = _dot(ff_ref[...], x)
    kf = _dot(fk_ref[...], k_ref[0].astype(BF16))
    xr, xi, kr, ki = xf[:n], xf[n:], kf[:n], kf[n:]
    yin = jnp.concatenate([xr * kr - xi * ki, xr * ki + xi * kr], axis=0).astype(BF16)
    y = _dot(fi_ref[...], yin)
    for s in range(2):
        yh = y[s * L:(s + 1) * L] + z_ref[s].astype(F32) * bias_ref[0]
        o_ref[s] = (x0_ref[s].astype(F32) * yh).astype(BF16)


def _short_conv(z, x0, k2s, layer, consts, bias):
    b, L, c = z.shape
    n = 2 * L
    cb = min(c, 256)
    ff, fk, finv = consts
    kern = functools.partial(_short_conv_kernel, L=L)
    return pl.pallas_call(
        kern,
        grid=(b // 2, c // cb),
        in_specs=[pl.BlockSpec((2, L, cb), lambda p, j: (p, 0, j)),
                  pl.BlockSpec((2, L, cb), lambda p, j: (p, 0, j)),
                  pl.BlockSpec((1, n, cb), lambda p, j: (layer, 0, j)),
                  pl.BlockSpec((2 * n, 2 * L), lambda p, j: (0, 0)),
                  pl.BlockSpec((2 * n, n), lambda p, j: (0, 0)),
                  pl.BlockSpec((2 * L, 2 * n), lambda p, j: (0, 0)),
                  pl.BlockSpec((1, 1, cb), lambda p, j: (layer, 0, j))],
        out_specs=pl.BlockSpec((2, L, cb), lambda p, j: (p, 0, j)),
        out_shape=jax.ShapeDtypeStruct((b, L, c), BF16),
        compiler_params=_cparams("parallel", "parallel"),
        name="short_conv",
    )(z, x0, k2s, ff, fk, finv, bias)


def _s5_matrices(lam_re, lam_im, log_step, b_re, b_im, c_re, c_im, d):
    T = S5_CHUNK
    lr, li = lam_re.astype(F32), lam_im.astype(F32)
    dt = jnp.exp(log_step.astype(F32))[..., None]
    mag = jnp.exp(lr * dt)
    a_r, a_i = mag * jnp.cos(li * dt), mag * jnp.sin(li * dt)
    den = lr * lr + li * li
    q_r = ((a_r - 1.0) * lr + a_i * li) / den
    q_i = (a_i * lr - (a_r - 1.0) * li) / den
    br, bi = b_re.astype(F32), b_im.astype(F32)
    bb_r = q_r[..., None] * br - q_i[..., None] * bi
    bb_i = q_r[..., None] * bi + q_i[..., None] * br
    D, _, G, P, H = bb_r.shape
    pr, pi = [jnp.ones_like(a_r)], [jnp.zeros_like(a_i)]
    for _ in range(T):
        pr.append(pr[-1] * a_r - pi[-1] * a_i)
        pi.append(pr[-2] * a_i + pi[-1] * a_r)
    pw_r, pw_i = jnp.stack(pr, axis=-1), jnp.stack(pi, axis=-1)
    ct_r = jnp.swapaxes(c_re.astype(F32), -1, -2)
    ct_i = jnp.swapaxes(c_im.astype(F32), -1, -2)
    cat_r = (pw_r[..., :, None] * ct_r[..., None, :]
             - pw_i[..., :, None] * ct_i[..., None, :]).reshape(D, 2, G, P, (T + 1) * H)
    cat_i = (pw_i[..., :, None] * ct_r[..., None, :]
             + pw_r[..., :, None] * ct_i[..., None, :]).reshape(D, 2, G, P, (T + 1) * H)
    kall = (jnp.einsum('dkgph,dkgpn->dkghn', bb_r, cat_r[..., :T * H], precision=HIGHEST)
            - jnp.einsum('dkgph,dkgpn->dkghn', bb_i, cat_i[..., :T * H], precision=HIGHEST))
    kf = kall[:, 0]
    kb_rev = jnp.flip(kall[:, 1].reshape(D, G, H, T, H), axis=-2).reshape(D, G, H, T * H)
    zpad = jnp.zeros((D, G, H, (T - 1) * H), F32)
    kf_pad = jnp.concatenate([zpad, kf], axis=-1)
    kb_pad = jnp.concatenate([kb_rev, zpad], axis=-1)
    rows = [kf_pad[..., (T - 1 - j) * H:(2 * T - 1 - j) * H] + kb_pad[..., (T - 1 - j) * H:(2 * T - 1 - j) * H]
            for j in range(T)]
    m_mat = jnp.stack(rows, axis=2).reshape(D, G, T * H, T * H)
    bt_r, bt_i = jnp.swapaxes(bb_r, -1, -2), jnp.swapaxes(bb_i, -1, -2)
    pj_r = jnp.moveaxis(pw_r[..., :T], -1, -2)
    pj_i = jnp.moveaxis(pw_i[..., :T], -1, -2)

    def end_state(k, flip):
        qr, qi = pj_r[:, k], pj_i[:, k]
        if flip:
            qr, qi = jnp.flip(qr, axis=-2), jnp.flip(qi, axis=-2)
        er = qr[..., :, None, :] * bt_r[:, k][..., None, :, :] - qi[..., :, None, :] * bt_i[:, k][..., None, :, :]
        ei = qr[..., :, None, :] * bt_i[:, k][..., None, :, :] + qi[..., :, None, :] * bt_r[:, k][..., None, :, :]
        return er.reshape(D, G, T * H, P), ei.reshape(D, G, T * H, P)

    ef_r, ef_i = end_state(0, True)
    eb_r, eb_i = end_state(1, False)
    e_mat = jnp.concatenate([ef_r, eb_r, ef_i, eb_i], axis=-1)
    cf_r, cf_i = cat_r[:, 0][..., H:], -cat_i[:, 0][..., H:]
    rev = lambda v: jnp.flip(v.reshape(D, G, P, T, H), axis=-2).reshape(D, G, P, T * H)
    cb_r, cb_i = rev(cat_r[:, 1][..., H:]), rev(-cat_i[:, 1][..., H:])
    zp = jnp.zeros_like(cf_r)
    w_out = jnp.concatenate([m_mat, cf_r, zp, cf_i, zp, zp, cb_r, zp, cb_i], axis=-2)
    at_r = jnp.concatenate([pw_r[:, 0, :, :, T], pw_r[:, 1, :, :, T]], axis=-1)[:, :, None, :]
    at_i = jnp.concatenate([pw_i[:, 0, :, :, T], pw_i[:, 1, :, :, T]], axis=-1)[:, :, None, :]
    d_t = jnp.tile(d.astype(F32).reshape(D, G, 1, H), (1, 1, 1, T))
    return e_mat.astype(BF16), w_out.astype(BF16), at_r, at_i, d_t


def _s5_kernel(x_ref, e_ref, w_ref, ar_ref, ai_ref, d_ref, init_ref, y_ref, fin_ref,
               e_scr, sa_scr, sb_scr, *, gb, nb, nc, p2):
    nt = nc // 2
    lane = lax.broadcasted_iota(jnp.int32, (2 * nb, p2), 1)
    is_fwd = lane < (p2 // 2)
    first = lax.broadcasted_iota(jnp.int32, (2 * nb, p2), 0) < nb
    swap = lambda v: pltpu.roll(v, nb, axis=0)
    for g in range(gb):
        e_scr[g] = _dot(x_ref[g], e_ref[g])

    def step(k, carry):
        out = []
        rf = pl.ds(pl.multiple_of(k * 2 * nb, 2 * nb), 2 * nb)
        rb = pl.ds(pl.multiple_of((nt - 1 - k) * 2 * nb, 2 * nb), 2 * nb)
        for g in range(gb):
            cr, ci = carry[2 * g], carry[2 * g + 1]
            ar, ai = ar_ref[g], ai_ref[g]
            er = jnp.where(is_fwd, e_scr[g, rf, 0:p2], swap(e_scr[g, rb, 0:p2]))
            ei = jnp.where(is_fwd, e_scr[g, rf, p2:2 * p2], swap(e_scr[g, rb, p2:2 * p2]))
            ur = ar * cr - ai * ci + er
            ui = ar * ci + ai * cr + ei
            ur4, ui4 = swap(ur), swap(ui)
            sr = jnp.where(first, cr, ur4)
            si = jnp.where(first, ci, ui4)
            sa_scr[g, rf, 0:p2] = sr
            sa_scr[g, rf, p2:2 * p2] = si
            sb_scr[g, rb, 0:p2] = swap(sr)
            sb_scr[g, rb, p2:2 * p2] = swap(si)
            xr = jnp.where(first, ur, ur4)
            xi = jnp.where(first, ui, ui4)
            zr = ar * xr - ai * xi + er
            zi = ar * xi + ai * xr + ei
            out.append(jnp.where(first, swap(zr), zr))
            out.append(jnp.where(first, swap(zi), zi))
        return tuple(out)

    init = []
    for g in range(gb):
        init += [init_ref[g, :, 0:p2], init_ref[g, :, p2:2 * p2]]
    fin = lax.fori_loop(0, nt, step, tuple(init))
    for g in range(gb):
        fin_ref[g] = jnp.concatenate([fin[2 * g], fin[2 * g + 1]], axis=1)
        x = x_ref[g]
        lhs = jnp.concatenate([x, sa_scr[g].astype(BF16), sb_scr[g].astype(BF16)], axis=1)
        y = _dot(lhs, w_ref[g]) + x.astype(F32) * d_ref[g]
        y_ref[g] = jax.nn.gelu(y).astype(BF16)


def _s5_scan(xg, mats, layer, init, nb):
    e_mat, w_out, at_r, at_i, d_t = mats
    G, R, th = xg.shape
    p4 = e_mat.shape[-1]
    p2 = p4 // 2
    gb = 2
    nc = R // nb
    assert 2 * nb == 8 and nc % 2 == 0, "two chunks of batch rows must fill one 8-sublane tile"
    kern = functools.partial(_s5_kernel, gb=gb, nb=nb, nc=nc, p2=p2)
    g3 = lambda i: (i, 0, 0)
    l4 = lambda i: (layer, i, 0, 0)
    return pl.pallas_call(
        kern,
        grid=(G // gb,),
        in_specs=[pl.BlockSpec((gb, R, th), g3),
                  pl.BlockSpec((None, gb, th, p4), l4),
                  pl.BlockSpec((None, gb, th + 2 * p4, th), l4),
                  pl.BlockSpec((None, gb, 1, p2), l4),
                  pl.BlockSpec((None, gb, 1, p2), l4),
                  pl.BlockSpec((None, gb, 1, th), l4),
                  pl.BlockSpec((gb, 2 * nb, p4), g3)],
        out_specs=[pl.BlockSpec((gb, R, th), g3),
                   pl.BlockSpec((gb, 2 * nb, p4), g3)],
        out_shape=[jax.ShapeDtypeStruct((G, R, th), BF16),
                   jax.ShapeDtypeStruct((G, 2 * nb, p4), F32)],
        scratch_shapes=[pltpu.VMEM((gb, R, p4), F32),
                        pltpu.VMEM((gb, R, p4), F32),
                        pltpu.VMEM((gb, R, p4), F32)],
        compiler_params=_cparams("parallel"),
        name="s5_scan",
    )(xg, e_mat, w_out, at_r, at_i, d_t, init)


def _lane_group(rows, h):
    return lax.broadcasted_iota(jnp.int32, (rows, LANES), 1) // h


def _s5_pack_kernel(u_ref, o_ref, t_scr, *, h, rows):
    gl = LANES // h
    t_scr[...] = _swap_major(u_ref[...])
    pr = min(rows, PACK_ROWS)
    grp = _lane_group(pr, h)
    for rb in range(rows // pr):
        rs = slice(rb * pr, (rb + 1) * pr)
        for half in range(S5_CHUNK // gl):
            acc = [None] * gl
            for jj in range(gl):
                v = t_scr[half * gl + jj, rs, :]
                for g in range(gl):
                    r = pltpu.roll(v, ((jj - g) * h) % LANES, axis=1) if jj != g else v
                    acc[g] = r if acc[g] is None else jnp.where(grp == jj, r, acc[g])
            for g in range(gl):
                o_ref[g, rs, half * LANES:(half + 1) * LANES] = acc[g].astype(BF16)


def _s5_unpack_kernel(y_ref, o_ref, t_scr, *, h, rows):
    gl = LANES // h
    pr = min(rows, PACK_ROWS)
    grp = _lane_group(pr, h)
    for rb in range(rows // pr):
        rs = slice(rb * pr, (rb + 1) * pr)
        for half in range(S5_CHUNK // gl):
            ys =[y_ref[g, rs, half * LANES:(half + 1) * LANES].astype(F32) for g in range(gl)]
            for tt in range(gl):
                acc = None
                for g in range(gl):
                    r = pltpu.roll(ys[g], ((g - tt) * h) % LANES, axis=1) if g != tt else ys[g]
                    acc = r if acc is None else jnp.where(grp == g, r, acc)
                t_scr[half * gl + tt, rs, :] = acc
    o_ref[...] = _swap_major(t_scr[...])


def _s5_pack(u3, h):
    R, t, w = u3.shape
    gl = LANES // h
    rows = min(R, 256)
    kern = functools.partial(_s5_pack_kernel, h=h, rows=rows)
    return pl.pallas_call(
        kern,
        grid=(w // LANES, R // rows),
        in_specs=[pl.BlockSpec((rows, t, LANES), lambda l, i: (i, 0, l))],
        out_specs=pl.BlockSpec((gl, rows, t * h), lambda l, i: (l, i, 0)),
        out_shape=jax.ShapeDtypeStruct((w // h, R, t * h), BF16),
        scratch_shapes=[pltpu.VMEM((t, rows, LANES), F32)],
        compiler_params=_cparams("parallel", "parallel"),
        name="s5_pack",
    )(u3)


def _s5_unpack(yg, h):
    G, R, th = yg.shape
    t = th // h
    gl = LANES // h
    rows = min(R, 256)
    kern = functools.partial(_s5_unpack_kernel, h=h, rows=rows)
    return pl.pallas_call(
        kern,
        grid=(G // gl, R // rows),
        in_specs=[pl.BlockSpec((gl, rows, th), lambda l, i: (l, i, 0))],
        out_specs=pl.BlockSpec((rows, t, LANES), lambda l, i: (i, 0, l)),
        out_shape=jax.ShapeDtypeStruct((R, t, G * h), F32),
        scratch_shapes=[pltpu.VMEM((t, rows, LANES), F32)],
        compiler_params=_cparams("parallel", "parallel"),
        name="s5_unpack",
    )(yg)


def _s5_mixer(u4, mats, layer, init, h):
    lc, nb, t, w = u4.shape
    yg, fin = _s5_scan(_s5_pack(u4.reshape(lc * nb, t, w), h), mats, layer, init, nb)
    return _s5_unpack(yg, h).reshape(lc, nb, t, w), fin


def _outproj_kernel(yh_ref, ys_ref, x_ref, mod_ref, g_ref, wg_ref, bg_ref, wo_ref,
                    xo_ref, hx_ref, *, d, dh, tm):
    gate = mod_ref[0, :, 2 * d:3 * d]
    shift = mod_ref[0, :, 3 * d:4 * d]
    scale = mod_ref[0, :, 4 * d:5 * d]
    sub = tm // SUB_TILES
    rows = [slice(s * sub, (s + 1) * sub) for s in range(SUB_TILES)]
    ys = [ys_ref[s * sub // S5_CHUNK:(s + 1) * sub // S5_CHUNK, 0, :, :].reshape(sub, ys_ref.shape[-1])
          for s in range(SUB_TILES)]
    pre = [_dot(y.astype(BF16), wg_ref[0]) for y in ys]
    glu = [(y * jax.nn.sigmoid(a + bg_ref[0])).astype(BF16) for y, a in zip(ys, pre)]
    yx = [_dot(yh_ref[0, rs, :], wo_ref[0, 0:dh, :]) + _dot(gl, wo_ref[0, dh:, :]) for rs, gl in zip(rows, glu)]
    for rs, v in zip(rows, yx):
        xo = x_ref[0, rs, :] + gate * _rms(v, g_ref[0, 1:2, :])
        xo_ref[0, rs, :] = xo
        hx_ref[0, rs, :] = (_rms(xo, g_ref[0, 2:3, :]) * (1.0 + scale) + shift).astype(BF16)


def _outproj(yh, ys4, x, mod, mod_row, layer, norm_g, w_glu, b_glu, w_out, tm):
    b, s, d = x.shape
    dh = yh.shape[-1]
    ds5 = ys4.shape[-1]
    tc = tm // S5_CHUNK
    kern = functools.partial(_outproj_kernel, d=d, dh=dh, tm=tm)
    lyr = lambda bi, i: (layer, 0, 0)
    t3 = lambda bi, i: (bi, i, 0)
    return pl.pallas_call(
        kern,
        grid=(b, s // tm),
        in_specs=[pl.BlockSpec((1, tm, dh), t3),
                  pl.BlockSpec((tc, 1, S5_CHUNK, ds5), lambda bi, i: (i, bi, 0, 0)),
                  pl.BlockSpec((1, tm, d), t3),
                  pl.BlockSpec((1, 1, mod.shape[-1]), lambda bi, i: (mod_row(bi), 0, 0)),
                  pl.BlockSpec((1,) + norm_g.shape[1:], lyr),
                  pl.BlockSpec((1, ds5, ds5), lyr), pl.BlockSpec((1, 1, ds5), lyr),
                  pl.BlockSpec((1, dh + ds5, d), lyr)],
        out_specs=[pl.BlockSpec((1, tm, d), t3), pl.BlockSpec((1, tm, d), t3)],
        out_shape=[jax.ShapeDtypeStruct((b, s, d), F32), jax.ShapeDtypeStruct((b, s, d), BF16)],
        compiler_params=_cparams("parallel", "parallel"),
        name="outproj",
    )(yh, ys4, x, mod, norm_g, w_glu, b_glu, w_out)


def _ffn_kernel(hm_ref, hp_ref, hn_ref, x_ref, mod_ref, g_ref, wg_ref, wv_ref, cw_ref, cb_ref, wd_ref,
                o_ref, acc_ref, *, tm, d, wg, vertical):
    i = pl.program_id(1)
    nt = pl.num_programs(1)
    j = pl.program_id(2)
    nj = pl.num_programs(2)
    hm = hm_ref[0]
    cw = cw_ref[0]
    if vertical:
        top = jnp.where(i > 0, 1.0, 0.0).astype(BF16)
        bot = jnp.where(i < nt - 1, 1.0, 0.0).astype(BF16)
        ha = jnp.concatenate([hp_ref[0] * top, hm, hn_ref[0] * bot], axis=0)
    else:
        ha = hm
    g = _dot(ha, wg_ref[0])

    ns = FFN_SUB_TILES if tm % (FFN_SUB_TILES * wg) == 0 else 1
    sub = tm // ns

    def vcol(dx, r0):
        if not vertical:
            return g[r0:r0 + sub] * cw[3 + dx:4 + dx]
        return (g[r0:r0 + sub] * cw[dx:dx + 1] + g[r0 + wg:r0 + wg + sub] * cw[3 + dx:4 + dx]
                + g[r0 + 2 * wg:r0 + 2 * wg + sub] * cw[6 + dx:7 + dx])

    col = lax.broadcasted_iota(jnp.int32, (sub, g.shape[-1]), 0) & (wg - 1)
    vs = [_dot(hm[s * sub:(s + 1) * sub], wv_ref[0]) for s in range(ns)]
    hmid = []
    for s in range(ns):
        r0 = s * sub
        conv = (vcol(1, r0) + jnp.where(col > 0, pltpu.roll(vcol(0, r0), 1, axis=0), 0.0)
                + jnp.where(col < wg - 1, pltpu.roll(vcol(2, r0), sub - 1, axis=0), 0.0) + cb_ref[0])
        hmid.append((jax.nn.gelu(conv) * vs[s]).astype(BF16))
    parts = [_dot(h, wd_ref[0]) for h in hmid]

    @pl.when(j == 0)
    def _():
        for s in range(ns):
            acc_ref[s * sub:(s + 1) * sub, :] = parts[s]

    @pl.when(j > 0)
    def _():
        for s in range(ns):
            acc_ref[s * sub:(s + 1) * sub, :] += parts[s]

    @pl.when(j == nj - 1)
    def _():
        gate = mod_ref[0, :, 5 * d:6 * d]
        o_ref[0] = x_ref[0] + gate * _rms(acc_ref[...], g_ref[0, 3:4, :])


def _ffn(hx, x, mod, mod_row, layer, norm_g, w_up, conv_w, conv_b, w_down, tm, wg, vertical):
    b, s, d = x.shape
    f = w_down.shape[1]
    fc = f // 2 if (f // 2) % LANES == 0 else f
    nf = f // fc
    nt = s // tm
    r = tm // wg if vertical else 1
    hb = wg if vertical else 16
    nhb = s // hb
    kern = functools.partial(_ffn_kernel, tm=tm, d=d, wg=wg, vertical=vertical)
    t3 = lambda bi, i, j: (bi, i, 0)
    return pl.pallas_call(
        kern,
        grid=(b, nt, nf),
        in_specs=[pl.BlockSpec((1, tm, d), t3),
                  pl.BlockSpec((1, hb, d), lambda bi, i, j: (bi, jnp.maximum(i * r - 1, 0), 0)),
                  pl.BlockSpec((1, hb, d), lambda bi, i, j: (bi, jnp.minimum((i + 1) * r, nhb - 1), 0)),
                  pl.BlockSpec((1, tm, d), t3),
                  pl.BlockSpec((1, 1, mod.shape[-1]), lambda bi, i, j: (mod_row(bi), 0, 0)),
                  pl.BlockSpec((1,) + norm_g.shape[1:], lambda bi, i, j: (layer, 0, 0)),
                  pl.BlockSpec((1, d, fc), lambda bi, i, j: (layer, 0, j)),
                  pl.BlockSpec((1, d, fc), lambda bi, i, j: (layer, 0, nf + j)),
                  pl.BlockSpec((1, 9, fc), lambda bi, i, j: (layer, 0, j)),
                  pl.BlockSpec((1, 1, fc), lambda bi, i, j: (layer, 0, j)),
                  pl.BlockSpec((1, fc, d), lambda bi, i, j: (layer, j, 0))],
        out_specs=pl.BlockSpec((1, tm, d), t3),
        out_shape=jax.ShapeDtypeStruct((b, s, d), F32),
        scratch_shapes=[pltpu.VMEM((tm, d), F32)],
        compiler_params=_cparams("parallel", "parallel", "arbitrary"),
        name="conv_glu_ffn",
    )(hx, hx, hx, x, mod, norm_g, w_up, w_up, conv_w, conv_b, w_down)


def kernel(x, c, ctx, c_ctx, w_ada, b_ada, norm_g, w_in, hy_short_w, hy_short_b,
           filt_w_in, filt_b_in, filt_w_hid, filt_b_hid, filt_freq, filt_w_out, hy_bias,
           s5_lam_re, s5_lam_im, s5_log_step, s5_b_re, s5_b_im, s5_c_re, s5_c_im, s5_d,
           s5_w_glu, s5_b_glu, w_out, ffn_w_up, ffn_conv_w, ffn_conv_b, ffn_w_down):
    depth = w_ada.shape[0]
    bsz, seq, d = x.shape
    lctx = ctx.shape[1]
    dh = hy_bias.shape[-1]
    G, P, H = s5_b_re.shape[2], s5_b_re.shape[3], s5_b_re.shape[4]
    dff = ffn_w_down.shape[1]
    assert bsz % 2 == 0 and bsz <= 4 and seq % GRID_W == 0 and GRID_W & (GRID_W - 1) == 0

    cond = jnp.zeros((8, d), F32).at[:bsz].set(c).at[bsz].set(c_ctx)
    mod = _ada_mod(cond, w_ada, b_ada).reshape(depth * 8, 1, 6 * d)

    filt_args = (filt_w_in, filt_b_in, filt_w_hid, filt_b_hid, filt_freq, filt_w_out)
    long_conv = _LongConv(seq, dh, _hyena_filters(seq, *filt_args, dh))
    k_ctx = _hyena_filters(lctx, *filt_args, dh)
    short_c = tuple(jnp.asarray(a, BF16) for a in _short_consts(lctx))

    w_in_b = w_in.astype(BF16)
    w_glu_b = s5_w_glu.astype(BF16)
    w_out_b = w_out.astype(BF16)
    w_up_b = ffn_w_up.astype(BF16)
    w_down_b = ffn_w_down.astype(BF16)
    sb = hy_short_b.reshape(depth, 1, 3 * dh)
    cw = ffn_conv_w.reshape(depth, 9, dff)
    cb = ffn_conv_b.reshape(depth, 1, dff)
    bg = s5_b_glu.reshape(depth, 1, -1)
    hb = hy_bias.reshape(depth, 1, dh)
    mats = _s5_matrices(s5_lam_re, s5_lam_im, s5_log_step, s5_b_re, s5_b_im, s5_c_re, s5_c_im, s5_d)

    tm = min(seq, ROW_TILE)
    tp = min(seq, PROJ_ROW_TILE)
    for l in range(depth):
        last = l == depth - 1
        row_x = lambda bi, l=l: 8 * l + bi
        row_c = lambda bi, l=l: 8 * l + bsz

        x0c, zc, uc = _inproj(ctx, mod, row_c, l, norm_g, w_in_b, hy_short_w, sb, dh, lctx)
        ysc, ctx_state = _s5_mixer(uc, mats, l, jnp.zeros((G, 2 * bsz, 4 * P), F32), H)

        x0, z, u = _inproj(x, mod, row_x, l, norm_g, w_in_b, hy_short_w, sb, dh, tp)
        ys, _ = _s5_mixer(u, mats, l, ctx_state, H)
        yh = long_conv(z, x0, hb, l)
        x, hx = _outproj(yh, ys, x, mod, row_x, l, norm_g, w_glu_b, bg, w_out_b, tp)
        x = _ffn(hx, x, mod, row_x, l, norm_g, w_up_b, cw, cb, w_down_b, tm, GRID_W, True)

        if not last:
            yhc = _short_conv(zc, x0c, k_ctx, l, short_c, hb)
            ctx, hc = _outproj(yhc, ysc, ctx, mod, row_c, l, norm_g, w_glu_b, bg, w_out_b, lctx)
            ctx = _ffn(hc, ctx, mod, row_c, l, norm_g, w_up_b, cw, cb, w_down_b, lctx, lctx, False)
    return x
```

```python
import functools
import math

import numpy as np
import jax
import jax.numpy as jnp
from jax import lax
from jax.experimental import pallas as pl
from jax.experimental.pallas import tpu as pltpu

GRID_W = 64
RMS_EPS = 1e-6
DECAY_TARGET = 1e-2
FAST_DECAY_PCT = 0.3
SLOW_DECAY_PCT = 1.5
S5_CHUNK = 16
S5_GROUPS_PER_STEP = 4
LANES = 128
HALO = 16
ROW_TILE = 512
PROJ_ROW_TILE = 1024
SUB_TILES = 4
IN_SUB_TILES = 2
FFN_SUB_TILES = 1
PACK_ROWS = 128
VMEM_LIMIT = 56 * 1024 * 1024

F32 = jnp.float32
BF16 = jnp.bfloat16
HIGHEST = lax.Precision.HIGHEST


def _cparams(*sem):
    return pltpu.CompilerParams(dimension_semantics=sem, vmem_limit_bytes=VMEM_LIMIT)


def _dot(a, b, **kw):
    return jnp.dot(a, b, preferred_element_type=F32, **kw)


def _ada_kernel(cond_ref, w_ref, b_ref, o_ref):
    cv = cond_ref[...]
    s = cv * jax.nn.sigmoid(cv)
    o_ref[0] = _dot(s, w_ref[0], precision=HIGHEST) + b_ref[0]


def _ada_mod(cond, w_ada, b_ada):
    depth, d, n = w_ada.shape
    tn = n // 4
    return pl.pallas_call(
        _ada_kernel,
        grid=(depth, n // tn),
        in_specs=[pl.BlockSpec((8, d), lambda l, j: (0, 0)),
                  pl.BlockSpec((1, d, tn), lambda l, j: (l, 0, j)),
                  pl.BlockSpec((1, 1, tn), lambda l, j: (l, 0, j))],
        out_specs=pl.BlockSpec((1, 8, tn), lambda l, j: (l, 0, j)),
        out_shape=jax.ShapeDtypeStruct((depth, 8, n), F32),
        compiler_params=_cparams("parallel", "parallel"),
        name="ada_mod",
    )(cond, w_ada, b_ada.reshape(depth, 1, n))


def _rms(v, g):
    ms = jnp.mean(v * v, axis=-1, keepdims=True)
    return v * lax.rsqrt(ms + RMS_EPS) * g


def _inproj_kernel(xm_ref, xp_ref, xn_ref, mod_ref, g_ref, w_ref, sw_ref, sb_ref,
                   x0_ref, z_ref, u_ref, p_scr, *, tm, d, dh):
    i = pl.program_id(1)
    nt = pl.num_programs(1)
    shift = mod_ref[0, :, 0:d]
    scale = mod_ref[0, :, d:2 * d]
    xa = jnp.concatenate([xp_ref[0], xm_ref[0], xn_ref[0]], axis=0)
    sw = sw_ref[0]
    ns = IN_SUB_TILES
    sub = tm // ns
    cut = [0] + [HALO + s * sub for s in range(1, ns)] + [tm + 2 * HALO]

    def project(s):
        lo, hi = cut[s], cut[s + 1]
        xn = (_rms(xa[lo:hi], g_ref[0, 0:1, :]) * (1.0 + scale) + shift).astype(BF16)
        p = _dot(xn, w_ref[0])
        p_scr[lo:hi, :] = p[:, :3 * dh]
        if s == 0:
            inside = jnp.where(i > 0, 1.0, 0.0).astype(F32)
            p_scr[HALO - 8:HALO, :] = p[HALO - 8:HALO, :3 * dh] * inside
        if s == ns - 1:
            inside = jnp.where(i < nt - 1, 1.0, 0.0).astype(F32)
            p_scr[HALO + tm:HALO + tm + 8, :] = p[HALO + tm - lo:HALO + tm + 8 - lo, :3 * dh] * inside
        a, b = max(lo, HALO), min(hi, HALO + tm)
        u = p[a - lo:b - lo, 3 * dh:]
        u_ref[(a - HALO) // S5_CHUNK:(b - HALO) // S5_CHUNK, 0, :, :] = u.reshape(
            (b - a) // S5_CHUNK, S5_CHUNK, u.shape[-1])

    def conv_gate(s):
        r0 = HALO + s * sub
        conv = (p_scr[pl.ds(r0 - 1, sub), :] * sw[0:1] + p_scr[pl.ds(r0, sub), :] * sw[1:2]
                + p_scr[pl.ds(r0 + 1, sub), :] * sw[2:3] + sb_ref[0])
        rs = slice(s * sub, (s + 1) * sub)
        x0_ref[0, rs, :] = conv[:, :dh].astype(BF16)
        z_ref[0, rs, :] = (conv[:, dh:2 * dh] * conv[:, 2 * dh:]).astype(BF16)

    project(0)
    for s in range(ns):
        if s + 1 < ns:
            project(s + 1)
        conv_gate(s)


def _inproj(x, mod, mod_row, layer, norm_g, w_in, sw, sb, dh, tm):
    b, s, d = x.shape
    dp = w_in.shape[-1]
    ds5 = dp - 3 * dh
    nt = s // tm
    r = tm // HALO
    nh = s // HALO
    tc = tm // S5_CHUNK
    kern = functools.partial(_inproj_kernel, tm=tm, d=d, dh=dh)
    lyr = lambda bi, i: (layer, 0, 0)
    return pl.pallas_call(
        kern,
        grid=(b, nt),
        in_specs=[pl.BlockSpec((1, tm, d), lambda bi, i: (bi, i, 0)),
                  pl.BlockSpec((1, HALO, d), lambda bi, i: (bi, jnp.maximum(i * r - 1, 0), 0)),
                  pl.BlockSpec((1, HALO, d), lambda bi, i: (bi, jnp.minimum((i + 1) * r, nh - 1), 0)),
                  pl.BlockSpec((1, 1, mod.shape[-1]), lambda bi, i: (mod_row(bi), 0, 0)),
                  pl.BlockSpec((1,) + norm_g.shape[1:], lyr),
                  pl.BlockSpec((1, d, dp), lyr),
                  pl.BlockSpec((1, 3, 3 * dh), lyr),
                  pl.BlockSpec((1, 1, 3 * dh), lyr)],
        out_specs=[pl.BlockSpec((1, tm, dh), lambda bi, i: (bi, i, 0)),
                   pl.BlockSpec((1, tm, dh), lambda bi, i: (bi, i, 0)),
                   pl.BlockSpec((tc, 1, S5_CHUNK, ds5), lambda bi, i: (i, bi, 0, 0))],
        out_shape=[jax.ShapeDtypeStruct((b, s, dh), BF16),
                   jax.ShapeDtypeStruct((b, s, dh), BF16),
                   jax.ShapeDtypeStruct((s // S5_CHUNK, b, S5_CHUNK, ds5), F32)],
        scratch_shapes=[pltpu.VMEM((tm + 2 * HALO, 3 * dh), F32)],
        compiler_params=_cparams("parallel", "arbitrary"),
        name="inproj",
    )(x, x, x, mod, norm_g, w_in, sw, sb)


def _filter_feats(L, emb):
    bands = (emb - 1) // 2
    t = np.linspace(0.0, 1.0, L, dtype=np.float32).astype(np.float64)[:, None]
    w = (2.0 * math.pi / L) * np.arange(L, dtype=np.float64)[:, None]
    f = np.linspace(1e-4, bands - 1, bands, dtype=np.float32).astype(np.float64)[None, :]
    z = np.concatenate([t, np.cos(f * w), -np.sin(f * w)], axis=-1)
    zp = np.zeros((L, LANES), np.float32)
    zp[:, :emb] = z
    return zp


def _filter_kernel(z_ref, win_ref, bin_ref, whid_ref, bhid_ref, fr_ref, wf_ref, wb_ref, dl_ref, o_ref):
    z = z_ref[...]
    fr = fr_ref[0]
    h = jnp.sin(fr * (_dot(z, win_ref[0], precision=HIGHEST) + bin_ref[0]))
    for i in range(whid_ref.shape[1]):
        h = jnp.sin(fr * (_dot(h, whid_ref[0, i], precision=HIGHEST) + bhid_ref[0, i]))
    o_ref[0, 0] = _dot(h, wf_ref[0], precision=HIGHEST) * jnp.exp(-z[:, 0:1] * dl_ref[...])
    hb = _dot(h, wb_ref[0], precision=HIGHEST) * jnp.exp(-z[:, LANES:LANES + 1] * dl_ref[...])
    first = (pl.program_id(1) == 0) & (lax.broadcasted_iota(jnp.int32, (z.shape[0], 1), 0) == 0)
    o_ref[0, 1] = jnp.where(first, 0.0, hb)


def _block_diag2(w):
    zero = jnp.zeros_like(w)
    return jnp.concatenate([jnp.concatenate([w, zero], axis=-1), jnp.concatenate([zero, w], axis=-1)], axis=-2)


def _hyena_filters(L, f_w_in, f_b_in, f_w_hid, f_b_hid, f_freq, f_w_out, dh):
    depth, emb, hid = f_w_in.shape
    n_inner = f_w_hid.shape[1]
    tl = min(L, 1024)
    z1 = _filter_feats(L, emb)
    z = jnp.asarray(np.concatenate([z1, np.concatenate([z1[:1], z1[:0:-1]], axis=0)], axis=1))
    win = _block_diag2(jnp.zeros((depth, LANES, hid), F32).at[:, :emb].set(f_w_in))
    zero = jnp.zeros((depth, hid, dh), F32)
    wf = jnp.concatenate([f_w_out[:, :, :dh], zero], axis=1)
    wb = jnp.concatenate([zero, f_w_out[:, :, dh:]], axis=1)
    twice = lambda v, shp: jnp.tile(v.reshape(shp), (1,) * (len(shp) - 1) + (2,))
    deltas = np.abs(np.linspace(math.log(DECAY_TARGET) / FAST_DECAY_PCT,
                                math.log(DECAY_TARGET) / SLOW_DECAY_PCT, dh, dtype=np.float32))[None, :]
    h2 = 2 * hid
    out = pl.pallas_call(
        _filter_kernel,
        grid=(depth, L // tl),
        in_specs=[pl.BlockSpec((tl, 2 * LANES), lambda l, i: (i, 0)),
                  pl.BlockSpec((1, 2 * LANES, h2), lambda l, i: (l, 0, 0)),
                  pl.BlockSpec((1, 1, h2), lambda l, i: (l, 0, 0)),
                  pl.BlockSpec((1, n_inner, h2, h2), lambda l, i: (l, 0, 0, 0)),
                  pl.BlockSpec((1, n_inner, 1, h2), lambda l, i: (l, 0, 0, 0)),
                  pl.BlockSpec((1, 1, h2), lambda l, i: (l, 0, 0)),
                  pl.BlockSpec((1, h2, dh), lambda l, i: (l, 0, 0)),
                  pl.BlockSpec((1, h2, dh), lambda l, i: (l, 0, 0)),
                  pl.BlockSpec((1, dh), lambda l, i: (0, 0))],
        out_specs=pl.BlockSpec((1, 2, tl, dh), lambda l, i: (l, 0, i, 0)),
        out_shape=jax.ShapeDtypeStruct((depth, 2, L, dh), F32),
        compiler_params=_cparams("parallel", "parallel"),
        name="hyena_filter",
    )(z, win, twice(f_b_in, (depth, 1, hid)), _block_diag2(f_w_hid), twice(f_b_hid, (depth, n_inner, 1, hid)),
      twice(f_freq, (depth, 1, hid)), wf, wb, jnp.asarray(deltas))
    return out.reshape(depth, 2 * L, dh)


def _dft_consts(n1, n2):
    n = n1 * n2
    n2h = n2 // 2
    k2 = np.arange(n2)[:, None]
    a = -2.0 * np.pi * k2 * np.arange(n2)[None, :] / n2
    fr, fi = np.cos(a), np.sin(a)
    f1 = np.block([[fr[:, :n2h], -fi[:, :n2h]], [fi[:, :n2h], fr[:, :n2h]]])
    f1k = np.concatenate([fr, fi], axis=0)
    cr, ci = fr[:n2h] / n, -fi[:n2h] / n
    f3 = np.block([[cr, -ci], [ci, cr]])
    b = -2.0 * np.pi * np.arange(n1)[:, None] * np.arange(n1)[None, :] / n1
    t = -2.0 * np.pi * np.arange(n2)[:, None] * np.arange(n1)[None, :] / n
    return dict(f1=f1.astype(np.float32), f1k=f1k.astype(np.float32), f3=f3.astype(np.float32),
                gr=np.cos(b).astype(np.float32), gi=np.sin(b).astype(np.float32),
                tr=np.cos(t).astype(np.float32)[:, None, :], ti=np.sin(t).astype(np.float32)[:, None, :])


def _tables_kernel(gr_ref, gi_ref, tr_ref, ti_ref, m1_ref, m2_ref, *, kb):
    gr, gi = gr_ref[...], gi_ref[...]
    for q in range(kb):
        tr, ti = tr_ref[q], ti_ref[q]
        re = gr * tr - gi * ti
        im = gr * ti + gi * tr
        m1_ref[q] = jnp.concatenate([jnp.concatenate([re, -im], axis=1),
                                     jnp.concatenate([im, re], axis=1)], axis=0).astype(BF16)
        ret, imt = re.T, im.T
        m2_ref[q] = jnp.concatenate([jnp.concatenate([ret, imt], axis=1),
                                     jnp.concatenate([-imt, ret], axis=1)], axis=0).astype(BF16)


def _dft_tables(c, n1, n2):
    kb = 8
    kern = functools.partial(_tables_kernel, kb=kb)
    shp = jax.ShapeDtypeStruct((n2, 2 * n1, 2 * n1), BF16)
    return pl.pallas_call(
        kern,
        grid=(n2 // kb,),
        in_specs=[pl.BlockSpec((n1, n1), lambda i: (0, 0)),
                  pl.BlockSpec((n1, n1), lambda i: (0, 0)),
                  pl.BlockSpec((kb, 1, n1), lambda i: (i, 0, 0)),
                  pl.BlockSpec((kb, 1, n1), lambda i: (i, 0, 0))],
        out_specs=[pl.BlockSpec((kb, 2 * n1, 2 * n1), lambda i: (i, 0, 0)),
                   pl.BlockSpec((kb, 2 * n1, 2 * n1), lambda i: (i, 0, 0))],
        out_shape=[shp, shp],
        compiler_params=_cparams("parallel"),
        name="dft_tables",
    )(jnp.asarray(c["gr"]), jnp.asarray(c["gi"]), jnp.asarray(c["tr"]), jnp.asarray(c["ti"]))


def _swap_major(v):
    return pltpu.einshape("abc->bac", v)


def _s1_kernel(z_ref, f_ref, o_ref, y_scr, *, n2, nb1):
    zt = [_swap_major(z_ref[s].astype(F32)) for s in range(2)]
    for j in range(nb1):
        x = jnp.concatenate([zt[0][j], zt[1][j]], axis=0).astype(BF16)
        y = _dot(f_ref[...], x)
        y_scr[0, j] = y[:n2]
        y_scr[1, j] = y[n2:]
    for s in range(2):
        o_ref[0, s] = _swap_major(y_scr[s]).astype(BF16)


def _s1k_kernel(k_ref, f_ref, o_ref, y_scr, *, n2, nb1):
    kt = _swap_major(k_ref[0])
    for j in range(nb1):
        y = _dot(f_ref[...], kt[j].astype(BF16))
        y_scr[0, j] = y[:n2]
        y_scr[1, j] = y[n2:]
    for s in range(2):
        o_ref[0, s] = _swap_major(y_scr[s]).astype(BF16)


def _dft_stage1(z4, f1, n2, nb1):
    b, n2h, n1, c = z4.shape
    kern = functools.partial(_s1_kernel, n2=n2, nb1=nb1)
    return pl.pallas_call(
        kern,
        grid=(b // 2, n1 // nb1),
        in_specs=[pl.BlockSpec((2, n2h, nb1, c), lambda p, j: (p, 0, j, 0)),
                  pl.BlockSpec((2 * n2, 2 * n2h), lambda p, j: (0, 0))],
        out_specs=pl.BlockSpec((1, 2, n2, nb1, c), lambda p, j: (p, 0, 0, j, 0)),
        out_shape=jax.ShapeDtypeStruct((b // 2, 2, n2, n1, c), BF16),
        scratch_shapes=[pltpu.VMEM((2, nb1, n2, c), F32)],
        compiler_params=_cparams("parallel", "parallel"),
        name="dft_stage1",
    )(z4, f1)


def _dft_stage1_filter(k4, f1k, nb1):
    depth, n2, n1, c = k4.shape
    kern = functools.partial(_s1k_kernel, n2=n2, nb1=nb1)
    return pl.pallas_call(
        kern,
        grid=(depth, n1 // nb1),
        in_specs=[pl.BlockSpec((1, n2, nb1, c), lambda l, j: (l, 0, j, 0)),
                  pl.BlockSpec((2 * n2, n2), lambda l, j: (0, 0))],
        out_specs=pl.BlockSpec((1, 2, n2, nb1, c), lambda l, j: (l, 0, 0, j, 0)),
        out_shape=jax.ShapeDtypeStruct((depth, 2, n2, n1, c), BF16),
        scratch_shapes=[pltpu.VMEM((2, nb1, n2, c), F32)],
        compiler_params=_cparams("parallel", "parallel"),
        name="dft_stage1_filter",
    )(k4, f1k)


def _s2k_kernel(b_ref, m1_ref, o_ref, *, kb, n1):
    for q in range(kb):
        rows = slice(q * n1, (q + 1) * n1)
        xin = jnp.concatenate([b_ref[0, 0, rows, :], b_ref[0, 1, rows, :]], axis=0).astype(BF16)
        xf = _dot(m1_ref[q], xin)
        o_ref[0, 0, rows, :] = xf[:n1]
        o_ref[0, 1, rows, :] = xf[n1:]


def _dft_stage2_filter(bv, m1, n1, kb):
    depth, _, n, c = bv.shape
    kern = functools.partial(_s2k_kernel, kb=kb, n1=n1)
    return pl.pallas_call(
        kern,
        grid=(n // (kb * n1), depth),
        in_specs=[pl.BlockSpec((1, 2, kb * n1, c), lambda k, l: (l, 0, k, 0)),
                  pl.BlockSpec((kb, 2 * n1, 2 * n1), lambda k, l: (k, 0, 0))],
        out_specs=pl.BlockSpec((1, 2, kb * n1, c), lambda k, l: (l, 0, k, 0)),
        out_shape=jax.ShapeDtypeStruct((depth, 2, n, c), F32),
        compiler_params=_cparams("parallel", "parallel"),
        name="dft_stage2_filter",
    )(bv, m1)


def _s2_kernel(b_ref, m1_ref, kf_ref, m2_ref, o_ref, *, kb, n1):
    for q in range(kb):
        rows = slice(q * n1, (q + 1) * n1)
        xin = jnp.concatenate([b_ref[0, 0, rows, :], b_ref[0, 1, rows, :]], axis=0).astype(BF16)
        xf = _dot(m1_ref[q], xin)
        xr, xi = xf[:n1], xf[n1:]
        kr, ki = kf_ref[0, 0, rows, :], kf_ref[0, 1, rows, :]
        yin = jnp.concatenate([xr * kr - xi * ki, xr * ki + xi * kr], axis=0).astype(BF16)
        g = _dot(m2_ref[q], yin)
        o_ref[0, 0, rows, :] = g[:n1].astype(BF16)
        o_ref[0, 1, rows, :] = g[n1:].astype(BF16)


def _dft_stage2(bv, m1, kf, layer, m2, n1, kb):
    p, _, n, c = bv.shape
    kern = functools.partial(_s2_kernel, kb=kb, n1=n1)
    return pl.pallas_call(
        kern,
        grid=(n // (kb * n1), p),
        in_specs=[pl.BlockSpec((1, 2, kb * n1, c), lambda k, q: (q, 0, k, 0)),
                  pl.BlockSpec((kb, 2 * n1, 2 * n1), lambda k, q: (k, 0, 0)),
                  pl.BlockSpec((1, 2, kb * n1, c), lambda k, q: (layer, 0, k, 0)),
                  pl.BlockSpec((kb, 2 * n1, 2 * n1), lambda k, q: (k, 0, 0))],
        out_specs=pl.BlockSpec((1, 2, kb * n1, c), lambda k, q: (q, 0, k, 0)),
        out_shape=jax.ShapeDtypeStruct((p, 2, n, c), BF16),
        compiler_params=_cparams("parallel", "arbitrary"),
        name="dft_stage2",
    )(bv, m1, kf, m2)


def _s3_kernel(g_ref, f_ref, z_ref, x0_ref, bias_ref, o_ref, y_scr, *, n2h, nb1):
    gt = [_swap_major(g_ref[0, s].astype(F32)) for s in range(2)]
    for j in range(nb1):
        gin = jnp.concatenate([gt[0][j], gt[1][j]], axis=0).astype(BF16)
        y = _dot(f_ref[...], gin)
        y_scr[0, j] = y[:n2h]
        y_scr[1, j] = y[n2h:]
    for s in range(2):
        yh = _swap_major(y_scr[s]) + z_ref[s].astype(F32) * bias_ref[0]
        o_ref[s] = (x0_ref[s].astype(F32) * yh).astype(BF16)


def _dft_stage3(g5, f3, z4, x04, bias, layer, nb1):
    p, _, n2, n1, c = g5.shape
    n2h = n2 // 2
    kern = functools.partial(_s3_kernel, n2h=n2h, nb1=nb1)
    return pl.pallas_call(
        kern,
        grid=(p, n1 // nb1),
        in_specs=[pl.BlockSpec((1, 2, n2, nb1, c), lambda q, j: (q, 0, 0, j, 0)),
                  pl.BlockSpec((2 * n2h, 2 * n2), lambda q, j: (0, 0)),
                  pl.BlockSpec((2, n2h, nb1, c), lambda q, j: (q, 0, j, 0)),
                  pl.BlockSpec((2, n2h, nb1, c), lambda q, j: (q, 0, j, 0)),
                  pl.BlockSpec((1, 1, c), lambda q, j: (layer, 0, 0))],
        out_specs=pl.BlockSpec((2, n2h, nb1, c), lambda q, j: (q, 0, j, 0)),
        out_shape=jax.ShapeDtypeStruct((2 * p, n2h, n1, c), BF16),
        scratch_shapes=[pltpu.VMEM((2, nb1, n2h, c), F32)],
        compiler_params=_cparams("parallel", "parallel"),
        name="dft_stage3",
    )(g5, f3, z4, x04, bias)


class _LongConv:
    def __init__(self, L, c, k2s):
        n = 2 * L
        n1 = 1 << (int(math.log2(n)) // 2)
        n2 = n // n1
        assert n1 * n2 == n and n1 == n2, "long-convolution path needs 2L to be a square power of two"
        self.L, self.c, self.n1, self.n2 = L, c, n1, n2
        self.nb1 = min(n1, 16)
        self.kb = 8
        cst = _dft_consts(n1, n2)
        self.f1 = jnp.asarray(cst["f1"], BF16)
        self.f3 = jnp.asarray(cst["f3"], BF16)
        self.m1, self.m2 = _dft_tables(cst, n1, n2)
        depth = k2s.shape[0]
        bk = _dft_stage1_filter(k2s.reshape(depth, n2, n1, c), jnp.asarray(cst["f1k"], BF16), self.nb1)
        self.kf = _dft_stage2_filter(bk.reshape(depth, 2, n, c), self.m1, n1, self.kb)

    def __call__(self, z, x0, bias, layer):
        b, L, c = z.shape
        n1, n2 = self.n1, self.n2
        z4 = z.reshape(b, n2 // 2, n1, c)
        x04 = x0.reshape(b, n2 // 2, n1, c)
        b5 = _dft_stage1(z4, self.f1, n2, self.nb1)
        gv = _dft_stage2(b5.reshape(b // 2, 2, n1 * n2, c), self.m1, self.kf, layer, self.m2, n1, self.kb)
        y = _dft_stage3(gv.reshape(b // 2, 2, n2, n1, c), self.f3, z4, x04, bias, layer, self.nb1)
        return y.reshape(b, L, c)


def _short_consts(L):
    n = 2 * L
    a = -2.0 * np.pi * np.arange(n)[:, None] * np.arange(n)[None, :] / n
    fr, fi = np.cos(a), np.sin(a)
    ff = np.block([[fr[:, :L], -fi[:, :L]], [fi[:, :L], fr[:, :L]]])
    fk = np.concatenate([fr, fi], axis=0)
    cr, ci = fr[:L] / n, -fi[:L] / n
    finv = np.block([[cr, -ci], [ci, cr]])
    return ff.astype(np.float32), fk.astype(np.float32), finv.astype(np.float32)


def _short_conv_kernel(z_ref, x0_ref, k_ref, ff_ref, fk_ref, fi_ref, bias_ref, o_ref, *, L):
    n = 2 * L
    x = jnp.concatenate([z_ref[0], z_ref[1]], axis=0)
    xf = _dot(ff_ref[...], x)
    kf = _dot(fk_ref[...], k_ref[0].astype(BF16))
    xr, xi, kr, ki = xf[:n], xf[n:], kf[:n], kf[n:]
    yin = jnp.concatenate([xr * kr - xi * ki, xr * ki + xi * kr], axis=0).astype(BF16)
    y = _dot(fi_ref[...], yin)
    for s in range(2):
        yh = y[s * L:(s + 1) * L] + z_ref[s].astype(F32) * bias_ref[0]
        o_ref[s] = (x0_ref[s].astype(F32) * yh).astype(BF16)


def _short_conv(z, x0, k2s, layer, consts, bias):
    b, L, c = z.shape
    n = 2 * L
    cb = min(c, 256)
    ff, fk, finv = consts
    kern = functools.partial(_short_conv_kernel, L=L)
    return pl.pallas_call(
        kern,
        grid=(b // 2, c // cb),
        in_specs=[pl.BlockSpec((2, L, cb), lambda p, j: (p, 0, j)),
                  pl.BlockSpec((2, L, cb), lambda p, j: (p, 0, j)),
                  pl.BlockSpec((1, n, cb), lambda p, j: (layer, 0, j)),
                  pl.BlockSpec((2 * n, 2 * L), lambda p, j: (0, 0)),
                  pl.BlockSpec((2 * n, n), lambda p, j: (0, 0)),
                  pl.BlockSpec((2 * L, 2 * n), lambda p, j: (0, 0)),
                  pl.BlockSpec((1, 1, cb), lambda p, j: (layer, 0, j))],
        out_specs=pl.BlockSpec((2, L, cb), lambda p, j: (p, 0, j)),
        out_shape=jax.ShapeDtypeStruct((b, L, c), BF16),
        compiler_params=_cparams("parallel", "parallel"),
        name="short_conv",
    )(z, x0, k2s, ff, fk, finv, bias)


def _s5_matrices(lam_re, lam_im, log_step, b_re, b_im, c_re, c_im, d):
    T = S5_CHUNK
    lr, li = lam_re.astype(F32), lam_im.astype(F32)
    dt = jnp.exp(log_step.astype(F32))[..., None]
    mag = jnp.exp(lr * dt)
    a_r, a_i = mag * jnp.cos(li * dt), mag * jnp.sin(li * dt)
    den = lr * lr + li * li
    q_r = ((a_r - 1.0) * lr + a_i * li) / den
    q_i = (a_i * lr - (a_r - 1.0) * li) / den
    br, bi = b_re.astype(F32), b_im.astype(F32)
    bb_r = q_r[..., None] * br - q_i[..., None] * bi
    bb_i = q_r[..., None] * bi + q_i[..., None] * br
    D, _, G, P, H = bb_r.shape
    pr, pi = [jnp.ones_like(a_r)], [jnp.zeros_like(a_i)]
    for _ in range(T):
        pr.append(pr[-1] * a_r - pi[-1] * a_i)
        pi.append(pr[-2] * a_i + pi[-1] * a_r)
    pw_r, pw_i = jnp.stack(pr, axis=-1), jnp.stack(pi, axis=-1)
    ct_r = jnp.swapaxes(c_re.astype(F32), -1, -2)
    ct_i = jnp.swapaxes(c_im.astype(F32), -1, -2)
    cat_r = (pw_r[..., :, None] * ct_r[..., None, :]
             - pw_i[..., :, None] * ct_i[..., None, :]).reshape(D, 2, G, P, (T + 1) * H)
    cat_i = (pw_i[..., :, None] * ct_r[..., None, :]
             + pw_r[..., :, None] * ct_i[..., None, :]).reshape(D, 2, G, P, (T + 1) * H)
    kall = (jnp.einsum('dkgph,dkgpn->dkghn', bb_r, cat_r[..., :T * H], precision=HIGHEST)
            - jnp.einsum('dkgph,dkgpn->dkghn', bb_i, cat_i[..., :T * H], precision=HIGHEST))
    kf = kall[:, 0]
    kb_rev = jnp.flip(kall[:, 1].reshape(D, G, H, T, H), axis=-2).reshape(D, G, H, T * H)
    zpad = jnp.zeros((D, G, H, (T - 1) * H), F32)
    kf_pad = jnp.concatenate([zpad, kf], axis=-1)
    kb_pad = jnp.concatenate([kb_rev, zpad], axis=-1)
    rows = [kf_pad[..., (T - 1 - j) * H:(2 * T - 1 - j) * H] + kb_pad[..., (T - 1 - j) * H:(2 * T - 1 - j) * H]
            for j in range(T)]
    m_mat = jnp.stack(rows, axis=2).reshape(D, G, T * H, T * H)
    bt_r, bt_i = jnp.swapaxes(bb_r, -1, -2), jnp.swapaxes(bb_i, -1, -2)
    pj_r = jnp.moveaxis(pw_r[..., :T], -1, -2)
    pj_i = jnp.moveaxis(pw_i[..., :T], -1, -2)

    def end_state(k, flip):
        qr, qi = pj_r[:, k], pj_i[:, k]
        if flip:
            qr, qi = jnp.flip(qr, axis=-2), jnp.flip(qi, axis=-2)
        er = qr[..., :, None, :] * bt_r[:, k][..., None, :, :] - qi[..., :, None, :] * bt_i[:, k][..., None, :, :]
        ei = qr[..., :, None, :] * bt_i[:, k][..., None, :, :] + qi[..., :, None, :] * bt_r[:, k][..., None, :, :]
        return er.reshape(D, G, T * H, P), ei.reshape(D, G, T * H, P)

    ef_r, ef_i = end_state(0, True)
    eb_r, eb_i = end_state(1, False)
    e_mat = jnp.concatenate([ef_r, eb_r, ef_i, eb_i], axis=-1)
    cf_r, cf_i = cat_r[:, 0][..., H:], -cat_i[:, 0][..., H:]
    rev = lambda v: jnp.flip(v.reshape(D, G, P, T, H), axis=-2).reshape(D, G, P, T * H)
    cb_r, cb_i = rev(cat_r[:, 1][..., H:]), rev(-cat_i[:, 1][..., H:])
    zp = jnp.zeros_like(cf_r)
    w_out = jnp.concatenate([m_mat, cf_r, zp, cf_i, zp, zp, cb_r, zp, cb_i], axis=-2)
    at_r = jnp.concatenate([pw_r[:, 0, :, :, T], pw_r[:, 1, :, :, T]], axis=-1)[:, :, None, :]
    at_i = jnp.concatenate([pw_i[:, 0, :, :, T], pw_i[:, 1, :, :, T]], axis=-1)[:, :, None, :]
    d_t = jnp.tile(d.astype(F32).reshape(D, G, 1, H), (1, 1, 1, T))
    return e_mat.astype(BF16), w_out.astype(BF16), at_r, at_i, d_t


def _s5_kernel(x_ref, e_ref, w_ref, ar_ref, ai_ref, d_ref, init_ref, y_ref, fin_ref,
               e_scr, sa_scr, sb_scr, *, gb, nb, nc, p2):
    nt = nc // 2
    lane = lax.broadcasted_iota(jnp.int32, (2 * nb, p2), 1)
    is_fwd = lane < (p2 // 2)
    first = lax.broadcasted_iota(jnp.int32, (2 * nb, p2), 0) < nb
    swap = lambda v: pltpu.roll(v, nb, axis=0)
    for g in range(gb):
        e_scr[g] = _dot(x_ref[g], e_ref[g])

    def step(k, carry):
        out = []
        rf = pl.ds(pl.multiple_of(k * 2 * nb, 2 * nb), 2 * nb)
        rb = pl.ds(pl.multiple_of((nt - 1 - k) * 2 * nb, 2 * nb), 2 * nb)
        for g in range(gb):
            cr, ci = carry[2 * g], carry[2 * g + 1]
            ar, ai = ar_ref[g], ai_ref[g]
            er = jnp.where(is_fwd, e_scr[g, rf, 0:p2], swap(e_scr[g, rb, 0:p2]))
            ei = jnp.where(is_fwd, e_scr[g, rf, p2:2 * p2], swap(e_scr[g, rb, p2:2 * p2]))
            ur = ar * cr - ai * ci + er
            ui = ar * ci + ai * cr + ei
            ur4, ui4 = swap(ur), swap(ui)
            sr = jnp.where(first, cr, ur4)
            si = jnp.where(first, ci, ui4)
            sa_scr[g, rf, 0:p2] = sr
            sa_scr[g, rf, p2:2 * p2] = si
            sb_scr[g, rb, 0:p2] = swap(sr)
            sb_scr[g, rb, p2:2 * p2] = swap(si)
            xr = jnp.where(first, ur, ur4)
            xi = jnp.where(first, ui, ui4)
            zr = ar * xr - ai * xi + er
            zi = ar * xi + ai * xr + ei
            out.append(jnp.where(first, swap(zr), zr))
            out.append(jnp.where(first, swap(zi), zi))
        return tuple(out)

    init = []
    for g in range(gb):
        init += [init_ref[g, :, 0:p2], init_ref[g, :, p2:2 * p2]]
    fin = lax.fori_loop(0, nt, step, tuple(init))
    for g in range(gb):
        fin_ref[g] = jnp.concatenate([fin[2 * g], fin[2 * g + 1]], axis=1)
        x = x_ref[g]
        lhs = jnp.concatenate([x, sa_scr[g].astype(BF16), sb_scr[g].astype(BF16)], axis=1)
        y = _dot(lhs, w_ref[g]) + x.astype(F32) * d_ref[g]
        y_ref[g] = jax.nn.gelu(y).astype(BF16)


def _s5_scan(xg, mats, layer, init, nb):
    e_mat, w_out, at_r, at_i, d_t = mats
    G, R, th = xg.shape
    p4 = e_mat.shape[-1]
    p2 = p4 // 2
    gb = S5_GROUPS_PER_STEP if G % S5_GROUPS_PER_STEP == 0 else 2
    nc = R // nb
    assert 2 * nb == 8 and nc % 2 == 0, "two chunks of batch rows must fill one 8-sublane tile"
    kern = functools.partial(_s5_kernel, gb=gb, nb=nb, nc=nc, p2=p2)
    g3 = lambda i: (i, 0, 0)
    l4 = lambda i: (layer, i, 0, 0)
    return pl.pallas_call(
        kern,
        grid=(G // gb,),
        in_specs=[pl.BlockSpec((gb, R, th), g3),
                  pl.BlockSpec((None, gb, th, p4), l4),
                  pl.BlockSpec((None, gb, th + 2 * p4, th), l4),
                  pl.BlockSpec((None, gb, 1, p2), l4),
                  pl.BlockSpec((None, gb, 1, p2), l4),
                  pl.BlockSpec((None, gb, 1, th), l4),
                  pl.BlockSpec((gb, 2 * nb, p4), g3)],
        out_specs=[pl.BlockSpec((gb, R, th), g3),
                   pl.BlockSpec((gb, 2 * nb, p4), g3)],
        out_shape=[jax.ShapeDtypeStruct((G, R, th), BF16),
                   jax.ShapeDtypeStruct((G, 2 * nb, p4), F32)],
        scratch_shapes=[pltpu.VMEM((gb, R, p4), F32),
                        pltpu.VMEM((gb, R, p4), F32),
                        pltpu.VMEM((gb, R, p4), F32)],
        compiler_params=_cparams("parallel"),
        name="s5_scan",
    )(xg, e_mat, w_out, at_r, at_i, d_t, init)


def _lane_group(rows, h):
    return lax.broadcasted_iota(jnp.int32, (rows, LANES), 1) // h


def _block_transpose(vs, h):
    n = len(vs)
    assert n * h == LANES and n & (n - 1) == 0
    blk = _lane_group(vs[0].shape[0], h)
    s = 1
    while s < n:
        upper = (blk & s) != 0
        out = list(vs)
        for i in range(n):
            if i & s == 0:
                a, b = vs[i], vs[i | s]
                out[i] = jnp.where(upper, pltpu.roll(b, s * h, axis=1), a)
                out[i | s] = jnp.where(upper, b, pltpu.roll(a, LANES - s * h, axis=1))
        vs = out
        s *= 2
    return vs


def _s5_pack_kernel(u_ref, o_ref, t_scr, *, h, rows):
    gl = LANES // h
    t_scr[...] = _swap_major(u_ref[...])
    pr = min(rows, PACK_ROWS)
    for rb in range(rows // pr):
        rs = slice(rb * pr, (rb + 1) * pr)
        for half in range(S5_CHUNK // gl):
            xs = _block_transpose([t_scr[half * gl + jj, rs, :] for jj in range(gl)], h)
            for g in range(gl):
                o_ref[g, rs, half * LANES:(half + 1) * LANES] = xs[g].astype(BF16)


def _s5_unpack_kernel(y_ref, o_ref, t_scr, *, h, rows):
    gl = LANES // h
    pr = min(rows, PACK_ROWS)
    for rb in range(rows // pr):
        rs = slice(rb * pr, (rb + 1) * pr)
        for half in range(S5_CHUNK // gl):
            ys = _block_transpose([y_ref[g, rs, half * LANES:(half + 1) * LANES].astype(F32) for g in range(gl)], h)
            for tt in range(gl):
                t_scr[half * gl + tt, rs, :] = ys[tt]
    o_ref[...] = _swap_major(t_scr[...])


def _s5_pack(u3, h):
    R, t, w = u3.shape
    gl = LANES // h
    rows = min(R, 256)
    kern = functools.partial(_s5_pack_kernel, h=h, rows=rows)
    return pl.pallas_call(
        kern,
        grid=(w // LANES, R // rows),
        in_specs=[pl.BlockSpec((rows, t, LANES), lambda l, i: (i, 0, l))],
        out_specs=pl.BlockSpec((gl, rows, t * h), lambda l, i: (l, i, 0)),
        out_shape=jax.ShapeDtypeStruct((w // h, R, t * h), BF16),
        scratch_shapes=[pltpu.VMEM((t, rows, LANES), F32)],
        compiler_params=_cparams("parallel", "parallel"),
        name="s5_pack",
    )(u3)


def _s5_unpack(yg, h):
    G, R, th = yg.shape
    t = th // h
    gl = LANES // h
    rows = min(R, 256)
    kern = functools.partial(_s5_unpack_kernel, h=h, rows=rows)
    return pl.pallas_call(
        kern,
        grid=(G // gl, R // rows),
        in_specs=[pl.BlockSpec((gl, rows, th), lambda l, i: (l, i, 0))],
        out_specs=pl.BlockSpec((rows, t, LANES), lambda l, i: (i, 0, l)),
        out_shape=jax.ShapeDtypeStruct((R, t, G * h), F32),
        scratch_shapes=[pltpu.VMEM((t, rows, LANES), F32)],
        compiler_params=_cparams("parallel", "parallel"),
        name="s5_unpack",
    )(yg)


def _s5_mixer(u4, mats, layer, init, h):
    lc, nb, t, w = u4.shape
    yg, fin = _s5_scan(_s5_pack(u4.reshape(lc * nb, t, w), h), mats, layer, init, nb)
    return _s5_unpack(yg, h).reshape(lc, nb, t, w), fin


def _outproj_kernel(yh_ref, ys_ref, x_ref, mod_ref, g_ref, wg_ref, bg_ref, wo_ref,
                    xo_ref, hx_ref, *, d, dh, tm):
    gate = mod_ref[0, :, 2 * d:3 * d]
    shift = mod_ref[0, :, 3 * d:4 * d]
    scale = mod_ref[0, :, 4 * d:5 * d]
    sub = tm // SUB_TILES
    rows = [slice(s * sub, (s + 1) * sub) for s in range(SUB_TILES)]
    ys = [ys_ref[s * sub // S5_CHUNK:(s + 1) * sub // S5_CHUNK, 0, :, :].reshape(sub, ys_ref.shape[-1])
          for s in range(SUB_TILES)]
    pre = [_dot(y.astype(BF16), wg_ref[0]) for y in ys]
    glu = [(y * jax.nn.sigmoid(a + bg_ref[0])).astype(BF16) for y, a in zip(ys, pre)]
    yx = [_dot(yh_ref[0, rs, :], wo_ref[0, 0:dh, :]) + _dot(gl, wo_ref[0, dh:, :]) for rs, gl in zip(rows, glu)]
    for rs, v in zip(rows, yx):
        xo = x_ref[0, rs, :] + gate * _rms(v, g_ref[0, 1:2, :])
        xo_ref[0, rs, :] = xo
        hx_ref[0, rs, :] = (_rms(xo, g_ref[0, 2:3, :]) * (1.0 + scale) + shift).astype(BF16)


def _outproj(yh, ys4, x, mod, mod_row, layer, norm_g, w_glu, b_glu, w_out, tm):
    b, s, d = x.shape
    dh = yh.shape[-1]
    ds5 = ys4.shape[-1]
    tc = tm // S5_CHUNK
    kern = functools.partial(_outproj_kernel, d=d, dh=dh, tm=tm)
    lyr = lambda bi, i: (layer, 0, 0)
    t3 = lambda bi, i: (bi, i, 0)
    return pl.pallas_call(
        kern,
        grid=(b, s // tm),
        in_specs=[pl.BlockSpec((1, tm, dh), t3),
                  pl.BlockSpec((tc, 1, S5_CHUNK, ds5), lambda bi, i: (i, bi, 0, 0)),
                  pl.BlockSpec((1, tm, d), t3),
                  pl.BlockSpec((1, 1, mod.shape[-1]), lambda bi, i: (mod_row(bi), 0, 0)),
                  pl.BlockSpec((1,) + norm_g.shape[1:], lyr),
                  pl.BlockSpec((1, ds5, ds5), lyr), pl.BlockSpec((1, 1, ds5), lyr),
                  pl.BlockSpec((1, dh + ds5, d), lyr)],
        out_specs=[pl.BlockSpec((1, tm, d), t3), pl.BlockSpec((1, tm, d), t3)],
        out_shape=[jax.ShapeDtypeStruct((b, s, d), F32), jax.ShapeDtypeStruct((b, s, d), BF16)],
        compiler_params=_cparams("parallel", "parallel"),
        name="outproj",
    )(yh, ys4, x, mod, norm_g, w_glu, b_glu, w_out)


def _ffn_kernel(hm_ref, hp_ref, hn_ref, x_ref, mod_ref, g_ref, wg_ref, wv_ref, cw_ref, cb_ref, wd_ref,
                o_ref, acc_ref, *, tm, d, wg, vertical):
    i = pl.program_id(1)
    nt = pl.num_programs(1)
    j = pl.program_id(2)
    nj = pl.num_programs(2)
    hm = hm_ref[0]
    cw = cw_ref[0]
    if vertical:
        top = jnp.where(i > 0, 1.0, 0.0).astype(BF16)
        bot = jnp.where(i < nt - 1, 1.0, 0.0).astype(BF16)
        ha = jnp.concatenate([hp_ref[0] * top, hm, hn_ref[0] * bot], axis=0)
    else:
        ha = hm
    g = _dot(ha, wg_ref[0])

    ns = FFN_SUB_TILES if tm % (FFN_SUB_TILES * wg) == 0 else 1
    sub = tm // ns

    def vcol(dx, r0):
        if not vertical:
            return g[r0:r0 + sub] * cw[3 + dx:4 + dx]
        return (g[r0:r0 + sub] * cw[dx:dx + 1] + g[r0 + wg:r0 + wg + sub] * cw[3 + dx:4 + dx]
                + g[r0 + 2 * wg:r0 + 2 * wg + sub] * cw[6 + dx:7 + dx])

    col = lax.broadcasted_iota(jnp.int32, (sub, g.shape[-1]), 0) & (wg - 1)
    vs = [_dot(hm[s * sub:(s + 1) * sub], wv_ref[0]) for s in range(ns)]
    hmid = []
    for s in range(ns):
        r0 = s * sub
        conv = (vcol(1, r0) + jnp.where(col > 0, pltpu.roll(vcol(0, r0), 1, axis=0), 0.0)
                + jnp.where(col < wg - 1, pltpu.roll(vcol(2, r0), sub - 1, axis=0), 0.0) + cb_ref[0])
        hmid.append((jax.nn.gelu(conv) * vs[s]).astype(BF16))
    parts = [_dot(h, wd_ref[0]) for h in hmid]

    @pl.when(j == 0)
    def _():
        for s in range(ns):
            acc_ref[s * sub:(s + 1) * sub, :] = parts[s]

    @pl.when(j > 0)
    def _():
        for s in range(ns):
            acc_ref[s * sub:(s + 1) * sub, :] += parts[s]

    @pl.when(j == nj - 1)
    def _():
        gate = mod_ref[0, :, 5 * d:6 * d]
        o_ref[0] = x_ref[0] + gate * _rms(acc_ref[...], g_ref[0, 3:4, :])


def _ffn(hx, x, mod, mod_row, layer, norm_g, w_up, conv_w, conv_b, w_down, tm, wg, vertical):
    b, s, d = x.shape
    f = w_down.shape[1]
    fc = f // 2 if (f // 2) % LANES == 0 else f
    nf = f // fc
    nt = s // tm
    r = tm // wg if vertical else 1
    hb = wg if vertical else 16
    nhb = s // hb
    kern = functools.partial(_ffn_kernel, tm=tm, d=d, wg=wg, vertical=vertical)
    t3 = lambda bi, i, j: (bi, i, 0)
    return pl.pallas_call(
        kern,
        grid=(b, nt, nf),
        in_specs=[pl.BlockSpec((1, tm, d), t3),
                  pl.BlockSpec((1, hb, d), lambda bi, i, j: (bi, jnp.maximum(i * r - 1, 0), 0)),
                  pl.BlockSpec((1, hb, d), lambda bi, i, j: (bi, jnp.minimum((i + 1) * r, nhb - 1), 0)),
                  pl.BlockSpec((1, tm, d), t3),
                  pl.BlockSpec((1, 1, mod.shape[-1]), lambda bi, i, j: (mod_row(bi), 0, 0)),
                  pl.BlockSpec((1,) + norm_g.shape[1:], lambda bi, i, j: (layer, 0, 0)),
                  pl.BlockSpec((1, d, fc), lambda bi, i, j: (layer, 0, j)),
                  pl.BlockSpec((1, d, fc), lambda bi, i, j: (layer, 0, nf + j)),
                  pl.BlockSpec((1, 9, fc), lambda bi, i, j: (layer, 0, j)),
                  pl.BlockSpec((1, 1, fc), lambda bi, i, j: (layer, 0, j)),
                  pl.BlockSpec((1, fc, d), lambda bi, i, j: (layer, j, 0))],
        out_specs=pl.BlockSpec((1, tm, d), t3),
        out_shape=jax.ShapeDtypeStruct((b, s, d), F32),
        scratch_shapes=[pltpu.VMEM((tm, d), F32)],
        compiler_params=_cparams("parallel", "parallel", "arbitrary"),
        name="conv_glu_ffn",
    )(hx, hx, hx, x, mod, norm_g, w_up, w_up, conv_w, conv_b, w_down)


def kernel(x, c, ctx, c_ctx, w_ada, b_ada, norm_g, w_in, hy_short_w, hy_short_b,
           filt_w_in, filt_b_in, filt_w_hid, filt_b_hid, filt_freq, filt_w_out, hy_bias,
           s5_lam_re, s5_lam_im, s5_log_step, s5_b_re, s5_b_im, s5_c_re, s5_c_im, s5_d,
           s5_w_glu, s5_b_glu, w_out, ffn_w_up, ffn_conv_w, ffn_conv_b, ffn_w_down):
    depth = w_ada.shape[0]
    bsz, seq, d = x.shape
    lctx = ctx.shape[1]
    dh = hy_bias.shape[-1]
    G, P, H = s5_b_re.shape[2], s5_b_re.shape[3], s5_b_re.shape[4]
    dff = ffn_w_down.shape[1]
    assert bsz % 2 == 0 and bsz <= 4 and seq % GRID_W == 0 and GRID_W & (GRID_W - 1) == 0

    cond = jnp.zeros((8, d), F32).at[:bsz].set(c).at[bsz].set(c_ctx)
    mod = _ada_mod(cond, w_ada, b_ada).reshape(depth * 8, 1, 6 * d)

    filt_args = (filt_w_in, filt_b_in, filt_w_hid, filt_b_hid, filt_freq, filt_w_out)
    long_conv = _LongConv(seq, dh, _hyena_filters(seq, *filt_args, dh))
    k_ctx = _hyena_filters(lctx, *filt_args, dh)
    short_c = tuple(jnp.asarray(a, BF16) for a in _short_consts(lctx))

    w_in_b = w_in.astype(BF16)
    w_glu_b = s5_w_glu.astype(BF16)
    w_out_b = w_out.astype(BF16)
    w_up_b = ffn_w_up.astype(BF16)
    w_down_b = ffn_w_down.astype(BF16)
    sb = hy_short_b.reshape(depth, 1, 3 * dh)
    cw = ffn_conv_w.reshape(depth, 9, dff)
    cb = ffn_conv_b.reshape(depth, 1, dff)
    bg = s5_b_glu.reshape(depth, 1, -1)
    hb = hy_bias.reshape(depth, 1, dh)
    mats = _s5_matrices(s5_lam_re, s5_lam_im, s5_log_step, s5_b_re, s5_b_im, s5_c_re, s5_c_im, s5_d)

    tm = min(seq, ROW_TILE)
    tp = min(seq, PROJ_ROW_TILE)
    for l in range(depth):
        last = l == depth - 1
        row_x = lambda bi, l=l: 8 * l + bi
        row_c = lambda bi, l=l: 8 * l + bsz

        x0c, zc, uc = _inproj(ctx, mod, row_c, l, norm_g, w_in_b, hy_short_w, sb, dh, lctx)
        ysc, ctx_state = _s5_mixer(uc, mats, l, jnp.zeros((G, 2 * bsz, 4 * P), F32), H)

        x0, z, u = _inproj(x, mod, row_x, l, norm_g, w_in_b, hy_short_w, sb, dh, tp)
        ys, _ = _s5_mixer(u, mats, l, ctx_state, H)
        yh = long_conv(z, x0, hb, l)
        x, hx = _outproj(yh, ys, x, mod, row_x, l, norm_g, w_glu_b, bg, w_out_b, tp)
        x = _ffn(hx, x, mod, row_x, l, norm_g, w_up_b, cw, cb, w_down_b, tm, GRID_W, True)

        if not last:
            yhc = _short_conv(zc, x0c, k_ctx, l, short_c, hb)
            ctx, hc = _outproj(yhc, ysc, ctx, mod, row_c, l, norm_g, w_glu_b, bg, w_out_b, lctx)
            ctx = _ffn(hc, ctx, mod, row_c, l, norm_g, w_up_b, cw, cb, w_down_b, lctx, lctx, False)
    return x
```

```python
import functools
import math

import numpy as np
import jax
import jax.numpy as jnp
from jax import lax
from jax.experimental import pallas as pl
from jax.experimental.pallas import tpu as pltpu

GRID_W = 64
RMS_EPS = 1e-6
DECAY_TARGET = 1e-2
FAST_DECAY_PCT = 0.3
SLOW_DECAY_PCT = 1.5
S5_CHUNK = 16
S5_GROUPS_PER_STEP = 4
LANES = 128
HALO = 16
ROW_TILE = 512
PROJ_ROW_TILE = 1024
SUB_TILES = 4
IN_SUB_TILES = 2
FFN_SUB_TILES = 1
FFN_TILE_GROUP = 2
PACK_ROWS = 128
VMEM_LIMIT = 56 * 1024 * 1024

F32 = jnp.float32
BF16 = jnp.bfloat16
HIGHEST = lax.Precision.HIGHEST


def _cparams(*sem):
    return pltpu.CompilerParams(dimension_semantics=sem, vmem_limit_bytes=VMEM_LIMIT)


def _dot(a, b, **kw):
    return jnp.dot(a, b, preferred_element_type=F32, **kw)


def _ada_kernel(cond_ref, w_ref, b_ref, o_ref):
    cv = cond_ref[...]
    s = cv * jax.nn.sigmoid(cv)
    o_ref[0] = _dot(s, w_ref[0], precision=HIGHEST) + b_ref[0]


def _ada_mod(cond, w_ada, b_ada):
    depth, d, n = w_ada.shape
    tn = n // 4
    return pl.pallas_call(
        _ada_kernel,
        grid=(depth, n // tn),
        in_specs=[pl.BlockSpec((8, d), lambda l, j: (0, 0)),
                  pl.BlockSpec((1, d, tn), lambda l, j: (l, 0, j)),
                  pl.BlockSpec((1, 1, tn), lambda l, j: (l, 0, j))],
        out_specs=pl.BlockSpec((1, 8, tn), lambda l, j: (l, 0, j)),
        out_shape=jax.ShapeDtypeStruct((depth, 8, n), F32),
        compiler_params=_cparams("parallel", "parallel"),
        name="ada_mod",
    )(cond, w_ada, b_ada.reshape(depth, 1, n))


def _rms(v, g):
    ms = jnp.mean(v * v, axis=-1, keepdims=True)
    return v * lax.rsqrt(ms + RMS_EPS) * g


def _inproj_kernel(xm_ref, xp_ref, xn_ref, mod_ref, g_ref, w_ref, sw_ref, sb_ref,
                   x0_ref, z_ref, u_ref, p_scr, *, tm, d, dh):
    i = pl.program_id(1)
    nt = pl.num_programs(1)
    shift = mod_ref[0, :, 0:d]
    scale = mod_ref[0, :, d:2 * d]
    xa = jnp.concatenate([xp_ref[0], xm_ref[0], xn_ref[0]], axis=0)
    sw = sw_ref[0]
    ns = IN_SUB_TILES
    sub = tm // ns
    cut = [0] + [HALO + s * sub for s in range(1, ns)] + [tm + 2 * HALO]

    def project(s):
        lo, hi = cut[s], cut[s + 1]
        xn = (_rms(xa[lo:hi], g_ref[0, 0:1, :]) * (1.0 + scale) + shift).astype(BF16)
        p = _dot(xn, w_ref[0])
        p_scr[lo:hi, :] = p[:, :3 * dh]
        if s == 0:
            inside = jnp.where(i > 0, 1.0, 0.0).astype(F32)
            p_scr[HALO - 8:HALO, :] = p[HALO - 8:HALO, :3 * dh] * inside
        if s == ns - 1:
            inside = jnp.where(i < nt - 1, 1.0, 0.0).astype(F32)
            p_scr[HALO + tm:HALO + tm + 8, :] = p[HALO + tm - lo:HALO + tm + 8 - lo, :3 * dh] * inside
        a, b = max(lo, HALO), min(hi, HALO + tm)
        u = p[a - lo:b - lo, 3 * dh:].astype(BF16)
        u_ref[(a - HALO) // S5_CHUNK:(b - HALO) // S5_CHUNK, 0, :, :] = u.reshape(
            (b - a) // S5_CHUNK, S5_CHUNK, u.shape[-1])

    def conv_gate(s):
        r0 = HALO + s * sub
        conv = (p_scr[pl.ds(r0 - 1, sub), :] * sw[0:1] + p_scr[pl.ds(r0, sub), :] * sw[1:2]
                + p_scr[pl.ds(r0 + 1, sub), :] * sw[2:3] + sb_ref[0])
        rs = slice(s * sub, (s + 1) * sub)
        x0_ref[0, rs, :] = conv[:, :dh].astype(BF16)
        z_ref[0, rs, :] = (conv[:, dh:2 * dh] * conv[:, 2 * dh:]).astype(BF16)

    project(0)
    for s in range(ns):
        if s + 1 < ns:
            project(s + 1)
        conv_gate(s)


def _inproj(x, mod, mod_row, layer, norm_g, w_in, sw, sb, dh, tm):
    b, s, d = x.shape
    dp = w_in.shape[-1]
    ds5 = dp - 3 * dh
    nt = s // tm
    r = tm // HALO
    nh = s // HALO
    tc = tm // S5_CHUNK
    kern = functools.partial(_inproj_kernel, tm=tm, d=d, dh=dh)
    lyr = lambda bi, i: (layer, 0, 0)
    return pl.pallas_call(
        kern,
        grid=(b, nt),
        in_specs=[pl.BlockSpec((1, tm, d), lambda bi, i: (bi, i, 0)),
                  pl.BlockSpec((1, HALO, d), lambda bi, i: (bi, jnp.maximum(i * r - 1, 0), 0)),
                  pl.BlockSpec((1, HALO, d), lambda bi, i: (bi, jnp.minimum((i + 1) * r, nh - 1), 0)),
                  pl.BlockSpec((1, 1, mod.shape[-1]), lambda bi, i: (mod_row(bi), 0, 0)),
                  pl.BlockSpec((1,) + norm_g.shape[1:], lyr),
                  pl.BlockSpec((1, d, dp), lyr),
                  pl.BlockSpec((1, 3, 3 * dh), lyr),
                  pl.BlockSpec((1, 1, 3 * dh), lyr)],
        out_specs=[pl.BlockSpec((1, tm, dh), lambda bi, i: (bi, i, 0)),
                   pl.BlockSpec((1, tm, dh), lambda bi, i: (bi, i, 0)),
                   pl.BlockSpec((tc, 1, S5_CHUNK, ds5), lambda bi, i: (i, bi, 0, 0))],
        out_shape=[jax.ShapeDtypeStruct((b, s, dh), BF16),
                   jax.ShapeDtypeStruct((b, s, dh), BF16),
                   jax.ShapeDtypeStruct((s // S5_CHUNK, b, S5_CHUNK, ds5), BF16)],
        scratch_shapes=[pltpu.VMEM((tm + 2 * HALO, 3 * dh), F32)],
        compiler_params=_cparams("parallel", "arbitrary"),
        name="inproj",
    )(x, x, x, mod, norm_g, w_in, sw, sb)


def _filter_feats(L, emb):
    bands = (emb - 1) // 2
    t = np.linspace(0.0, 1.0, L, dtype=np.float32).astype(np.float64)[:, None]
    w = (2.0 * math.pi / L) * np.arange(L, dtype=np.float64)[:, None]
    f = np.linspace(1e-4, bands - 1, bands, dtype=np.float32).astype(np.float64)[None, :]
    z = np.concatenate([t, np.cos(f * w), -np.sin(f * w)], axis=-1)
    zp = np.zeros((L, LANES), np.float32)
    zp[:, :emb] = z
    return zp


def _filter_kernel(z_ref, win_ref, bin_ref, whid_ref, bhid_ref, fr_ref, wf_ref, wb_ref, dl_ref, o_ref):
    z = z_ref[...]
    fr = fr_ref[0]
    h = jnp.sin(fr * (_dot(z, win_ref[0], precision=HIGHEST) + bin_ref[0]))
    for i in range(whid_ref.shape[1]):
        h = jnp.sin(fr * (_dot(h, whid_ref[0, i], precision=HIGHEST) + bhid_ref[0, i]))
    o_ref[0, 0] = _dot(h, wf_ref[0], precision=HIGHEST) * jnp.exp(-z[:, 0:1] * dl_ref[...])
    hb = _dot(h, wb_ref[0], precision=HIGHEST) * jnp.exp(-z[:, LANES:LANES + 1] * dl_ref[...])
    first = (pl.program_id(1) == 0) & (lax.broadcasted_iota(jnp.int32, (z.shape[0], 1), 0) == 0)
    o_ref[0, 1] = jnp.where(first, 0.0, hb)


def _block_diag2(w):
    zero = jnp.zeros_like(w)
    return jnp.concatenate([jnp.concatenate([w, zero], axis=-1), jnp.concatenate([zero, w], axis=-1)], axis=-2)


def _hyena_filters(L, f_w_in, f_b_in, f_w_hid, f_b_hid, f_freq, f_w_out, dh):
    depth, emb, hid = f_w_in.shape
    n_inner = f_w_hid.shape[1]
    tl = min(L, 1024)
    z1 = _filter_feats(L, emb)
    z = jnp.asarray(np.concatenate([z1, np.concatenate([z1[:1], z1[:0:-1]], axis=0)], axis=1))
    win = _block_diag2(jnp.zeros((depth, LANES, hid), F32).at[:, :emb].set(f_w_in))
    zero = jnp.zeros((depth, hid, dh), F32)
    wf = jnp.concatenate([f_w_out[:, :, :dh], zero], axis=1)
    wb = jnp.concatenate([zero, f_w_out[:, :, dh:]], axis=1)
    twice = lambda v, shp: jnp.tile(v.reshape(shp), (1,) * (len(shp) - 1) + (2,))
    deltas = np.abs(np.linspace(math.log(DECAY_TARGET) / FAST_DECAY_PCT,
                                math.log(DECAY_TARGET) / SLOW_DECAY_PCT, dh, dtype=np.float32))[None, :]
    h2 = 2 * hid
    out = pl.pallas_call(
        _filter_kernel,
        grid=(depth, L // tl),
        in_specs=[pl.BlockSpec((tl, 2 * LANES), lambda l, i: (i, 0)),
                  pl.BlockSpec((1, 2 * LANES, h2), lambda l, i: (l, 0, 0)),
                  pl.BlockSpec((1, 1, h2), lambda l, i: (l, 0, 0)),
                  pl.BlockSpec((1, n_inner, h2, h2), lambda l, i: (l, 0, 0, 0)),
                  pl.BlockSpec((1, n_inner, 1, h2), lambda l, i: (l, 0, 0, 0)),
                  pl.BlockSpec((1, 1, h2), lambda l, i: (l, 0, 0)),
                  pl.BlockSpec((1, h2, dh), lambda l, i: (l, 0, 0)),
                  pl.BlockSpec((1, h2, dh), lambda l, i: (l, 0, 0)),
                  pl.BlockSpec((1, dh), lambda l, i: (0, 0))],
        out_specs=pl.BlockSpec((1, 2, tl, dh), lambda l, i: (l, 0, i, 0)),
        out_shape=jax.ShapeDtypeStruct((depth, 2, L, dh), F32),
        compiler_params=_cparams("parallel", "parallel"),
        name="hyena_filter",
    )(z, win, twice(f_b_in, (depth, 1, hid)), _block_diag2(f_w_hid), twice(f_b_hid, (depth, n_inner, 1, hid)),
      twice(f_freq, (depth, 1, hid)), wf, wb, jnp.asarray(deltas))
    return out.reshape(depth, 2 * L, dh)


def _dft_consts(n1, n2):
    n = n1 * n2
    n2h = n2 // 2
    k2 = np.arange(n2)[:, None]
    a = -2.0 * np.pi * k2 * np.arange(n2)[None, :] / n2
    fr, fi = np.cos(a), np.sin(a)
    f1 = np.block([[fr[:, :n2h], -fi[:, :n2h]], [fi[:, :n2h], fr[:, :n2h]]])
    f1k = np.concatenate([fr, fi], axis=0)
    cr, ci = fr[:n2h] / n, -fi[:n2h] / n
    f3 = np.block([[cr, -ci], [ci, cr]])
    b = -2.0 * np.pi * np.arange(n1)[:, None] * np.arange(n1)[None, :] / n1
    t = -2.0 * np.pi * np.arange(n2)[:, None] * np.arange(n1)[None, :] / n
    return dict(f1=f1.astype(np.float32), f1k=f1k.astype(np.float32), f3=f3.astype(np.float32),
                gr=np.cos(b).astype(np.float32), gi=np.sin(b).astype(np.float32),
                tr=np.cos(t).astype(np.float32)[:, None, :], ti=np.sin(t).astype(np.float32)[:, None, :])


def _tables_kernel(gr_ref, gi_ref, tr_ref, ti_ref, m1_ref, m2_ref, *, kb):
    gr, gi = gr_ref[...], gi_ref[...]
    for q in range(kb):
        tr, ti = tr_ref[q], ti_ref[q]
        re = gr * tr - gi * ti
        im = gr * ti + gi * tr
        m1_ref[q] = jnp.concatenate([jnp.concatenate([re, -im], axis=1),
                                     jnp.concatenate([im, re], axis=1)], axis=0).astype(BF16)
        ret, imt = re.T, im.T
        m2_ref[q] = jnp.concatenate([jnp.concatenate([ret, imt], axis=1),
                                     jnp.concatenate([-imt, ret], axis=1)], axis=0).astype(BF16)


def _dft_tables(c, n1, n2):
    kb = 8
    kern = functools.partial(_tables_kernel, kb=kb)
    shp = jax.ShapeDtypeStruct((n2, 2 * n1, 2 * n1), BF16)
    return pl.pallas_call(
        kern,
        grid=(n2 // kb,),
        in_specs=[pl.BlockSpec((n1, n1), lambda i: (0, 0)),
                  pl.BlockSpec((n1, n1), lambda i: (0, 0)),
                  pl.BlockSpec((kb, 1, n1), lambda i: (i, 0, 0)),
                  pl.BlockSpec((kb, 1, n1), lambda i: (i, 0, 0))],
        out_specs=[pl.BlockSpec((kb, 2 * n1, 2 * n1), lambda i: (i, 0, 0)),
                   pl.BlockSpec((kb, 2 * n1, 2 * n1), lambda i: (i, 0, 0))],
        out_shape=[shp, shp],
        compiler_params=_cparams("parallel"),
        name="dft_tables",
    )(jnp.asarray(c["gr"]), jnp.asarray(c["gi"]), jnp.asarray(c["tr"]), jnp.asarray(c["ti"]))


def _swap_major(v):
    return pltpu.einshape("abc->bac", v)


def _s1_kernel(z_ref, f_ref, o_ref, y_scr, *, n2, nb1):
    zt = [_swap_major(z_ref[s].astype(F32)) for s in range(2)]
    for j in range(nb1):
        x = jnp.concatenate([zt[0][j], zt[1][j]], axis=0).astype(BF16)
        y = _dot(f_ref[...], x)
        y_scr[0, j] = y[:n2]
        y_scr[1, j] = y[n2:]
    for s in range(2):
        o_ref[0, s] = _swap_major(y_scr[s]).astype(BF16)


def _s1k_kernel(k_ref, f_ref, o_ref, y_scr, *, n2, nb1):
    kt = _swap_major(k_ref[0])
    for j in range(nb1):
        y = _dot(f_ref[...], kt[j].astype(BF16))
        y_scr[0, j] = y[:n2]
        y_scr[1, j] = y[n2:]
    for s in range(2):
        o_ref[0, s] = _swap_major(y_scr[s]).astype(BF16)


def _dft_stage1(z4, f1, n2, nb1):
    b, n2h, n1, c = z4.shape
    kern = functools.partial(_s1_kernel, n2=n2, nb1=nb1)
    return pl.pallas_call(
        kern,
        grid=(b // 2, n1 // nb1),
        in_specs=[pl.BlockSpec((2, n2h, nb1, c), lambda p, j: (p, 0, j, 0)),
                  pl.BlockSpec((2 * n2, 2 * n2h), lambda p, j: (0, 0))],
        out_specs=pl.BlockSpec((1, 2, n2, nb1, c), lambda p, j: (p, 0, 0, j, 0)),
        out_shape=jax.ShapeDtypeStruct((b // 2, 2, n2, n1, c), BF16),
        scratch_shapes=[pltpu.VMEM((2, nb1, n2, c), F32)],
        compiler_params=_cparams("parallel", "parallel"),
        name="dft_stage1",
    )(z4, f1)


def _dft_stage1_filter(k4, f1k, nb1):
    depth, n2, n1, c = k4.shape
    kern = functools.partial(_s1k_kernel, n2=n2, nb1=nb1)
    return pl.pallas_call(
        kern,
        grid=(depth, n1 // nb1),
        in_specs=[pl.BlockSpec((1, n2, nb1, c), lambda l, j: (l, 0, j, 0)),
                  pl.BlockSpec((2 * n2, n2), lambda l, j: (0, 0))],
        out_specs=pl.BlockSpec((1, 2, n2, nb1, c), lambda l, j: (l, 0, 0, j, 0)),
        out_shape=jax.ShapeDtypeStruct((depth, 2, n2, n1, c), BF16),
        scratch_shapes=[pltpu.VMEM((2, nb1, n2, c), F32)],
        compiler_params=_cparams("parallel", "parallel"),
        name="dft_stage1_filter",
    )(k4, f1k)


def _s2k_kernel(b_ref, m1_ref, o_ref, *, kb, n1):
    for q in range(kb):
        rows = slice(q * n1, (q + 1) * n1)
        xin = jnp.concatenate([b_ref[0, 0, rows, :], b_ref[0, 1, rows, :]], axis=0).astype(BF16)
        xf = _dot(m1_ref[q], xin)
        o_ref[0, 0, rows, :] = xf[:n1]
        o_ref[0, 1, rows, :] = xf[n1:]


def _dft_stage2_filter(bv, m1, n1, kb):
    depth, _, n, c = bv.shape
    kern = functools.partial(_s2k_kernel, kb=kb, n1=n1)
    return pl.pallas_call(
        kern,
        grid=(n // (kb * n1), depth),
        in_specs=[pl.BlockSpec((1, 2, kb * n1, c), lambda k, l: (l, 0, k, 0)),
                  pl.BlockSpec((kb, 2 * n1, 2 * n1), lambda k, l: (k, 0, 0))],
        out_specs=pl.BlockSpec((1, 2, kb * n1, c), lambda k, l: (l, 0, k, 0)),
        out_shape=jax.ShapeDtypeStruct((depth, 2, n, c), F32),
        compiler_params=_cparams("parallel", "parallel"),
        name="dft_stage2_filter",
    )(bv, m1)


def _s2_kernel(b_ref, m1_ref, kf_ref, m2_ref, o_ref, *, kb, n1):
    for q in range(kb):
        rows = slice(q * n1, (q + 1) * n1)
        xin = jnp.concatenate([b_ref[0, 0, rows, :], b_ref[0, 1, rows, :]], axis=0).astype(BF16)
        xf = _dot(m1_ref[q], xin)
        xr, xi = xf[:n1], xf[n1:]
        kr, ki = kf_ref[0, 0, rows, :], kf_ref[0, 1, rows, :]
        yin = jnp.concatenate([xr * kr - xi * ki, xr * ki + xi * kr], axis=0).astype(BF16)
        g = _dot(m2_ref[q], yin)
        o_ref[0, 0, rows, :] = g[:n1].astype(BF16)
        o_ref[0, 1, rows, :] = g[n1:].astype(BF16)


def _dft_stage2(bv, m1, kf, layer, m2, n1, kb):
    p, _, n, c = bv.shape
    kern = functools.partial(_s2_kernel, kb=kb, n1=n1)
    return pl.pallas_call(
        kern,
        grid=(n // (kb * n1), p),
        in_specs=[pl.BlockSpec((1, 2, kb * n1, c), lambda k, q: (q, 0, k, 0)),
                  pl.BlockSpec((kb, 2 * n1, 2 * n1), lambda k, q: (k, 0, 0)),
                  pl.BlockSpec((1, 2, kb * n1, c), lambda k, q: (layer, 0, k, 0)),
                  pl.BlockSpec((kb, 2 * n1, 2 * n1), lambda k, q: (k, 0, 0))],
        out_specs=pl.BlockSpec((1, 2, kb * n1, c), lambda k, q: (q, 0, k, 0)),
        out_shape=jax.ShapeDtypeStruct((p, 2, n, c), BF16),
        compiler_params=_cparams("parallel", "arbitrary"),
        name="dft_stage2",
    )(bv, m1, kf, m2)


def _s3_kernel(g_ref, f_ref, z_ref, x0_ref, bias_ref, o_ref, y_scr, *, n2h, nb1):
    gt = [_swap_major(g_ref[0, s].astype(F32)) for s in range(2)]
    for j in range(nb1):
        gin = jnp.concatenate([gt[0][j], gt[1][j]], axis=0).astype(BF16)
        y = _dot(f_ref[...], gin)
        y_scr[0, j] = y[:n2h]
        y_scr[1, j] = y[n2h:]
    for s in range(2):
        yh = _swap_major(y_scr[s]) + z_ref[s].astype(F32) * bias_ref[0]
        o_ref[s] = (x0_ref[s].astype(F32) * yh).astype(BF16)


def _dft_stage3(g5, f3, z4, x04, bias, layer, nb1):
    p, _, n2, n1, c = g5.shape
    n2h = n2 // 2
    kern = functools.partial(_s3_kernel, n2h=n2h, nb1=nb1)
    return pl.pallas_call(
        kern,
        grid=(p, n1 // nb1),
        in_specs=[pl.BlockSpec((1, 2, n2, nb1, c), lambda q, j: (q, 0, 0, j, 0)),
                  pl.BlockSpec((2 * n2h, 2 * n2), lambda q, j: (0, 0)),
                  pl.BlockSpec((2, n2h, nb1, c), lambda q, j: (q, 0, j, 0)),
                  pl.BlockSpec((2, n2h, nb1, c), lambda q, j: (q, 0, j, 0)),
                  pl.BlockSpec((1, 1, c), lambda q, j: (layer, 0, 0))],
        out_specs=pl.BlockSpec((2, n2h, nb1, c), lambda q, j: (q, 0, j, 0)),
        out_shape=jax.ShapeDtypeStruct((2 * p, n2h, n1, c), BF16),
        scratch_shapes=[pltpu.VMEM((2, nb1, n2h, c), F32)],
        compiler_params=_cparams("parallel", "parallel"),
        name="dft_stage3",
    )(g5, f3, z4, x04, bias)


class _LongConv:
    def __init__(self, L, c, k2s):
        n = 2 * L
        n1 = 1 << (int(math.log2(n)) // 2)
        n2 = n // n1
        assert n1 * n2 == n and n1 == n2, "long-convolution path needs 2L to be a square power of two"
        self.L, self.c, self.n1, self.n2 = L, c, n1, n2
        self.nb1 = min(n1, 16)
        self.kb = 8
        cst = _dft_consts(n1, n2)
        self.f1 = jnp.asarray(cst["f1"], BF16)
        self.f3 = jnp.asarray(cst["f3"], BF16)
        self.m1, self.m2 = _dft_tables(cst, n1, n2)
        depth = k2s.shape[0]
        bk = _dft_stage1_filter(k2s.reshape(depth, n2, n1, c), jnp.asarray(cst["f1k"], BF16), self.nb1)
        self.kf = _dft_stage2_filter(bk.reshape(depth, 2, n, c), self.m1, n1, self.kb)

    def __call__(self, z, x0, bias, layer):
        b, L, c = z.shape
        n1, n2 = self.n1, self.n2
        z4 = z.reshape(b, n2 // 2, n1, c)
        x04 = x0.reshape(b, n2 // 2, n1, c)
        b5 = _dft_stage1(z4, self.f1, n2, self.nb1)
        gv = _dft_stage2(b5.reshape(b // 2, 2, n1 * n2, c), self.m1, self.kf, layer, self.m2, n1, self.kb)
        y = _dft_stage3(gv.reshape(b // 2, 2, n2, n1, c), self.f3, z4, x04, bias, layer, self.nb1)
        return y.reshape(b, L, c)


def _short_consts(L):
    n = 2 * L
    a = -2.0 * np.pi * np.arange(n)[:, None] * np.arange(n)[None, :] / n
    fr, fi = np.cos(a), np.sin(a)
    ff = np.block([[fr[:, :L], -fi[:, :L]], [fi[:, :L], fr[:, :L]]])
    fk = np.concatenate([fr, fi], axis=0)
    cr, ci = fr[:L] / n, -fi[:L] / n
    finv = np.block([[cr, -ci], [ci, cr]])
    return ff.astype(np.float32), fk.astype(np.float32), finv.astype(np.float32)


def _short_conv_kernel(z_ref, x0_ref, k_ref, ff_ref, fk_ref, fi_ref, bias_ref, o_ref, *, L):
    n = 2 * L
    x = jnp.concatenate([z_ref[0], z_ref[1]], axis=0)
    xf = _dot(ff_ref[...], x)
    kf = _dot(fk_ref[...], k_ref[0].astype(BF16))
    xr, xi, kr, ki = xf[:n], xf[n:], kf[:n], kf[n:]
    yin = jnp.concatenate([xr * kr - xi * ki, xr * ki + xi * kr], axis=0).astype(BF16)
    y = _dot(fi_ref[...], yin)
    for s in range(2):
        yh = y[s * L:(s + 1) * L] + z_ref[s].astype(F32) * bias_ref[0]
        o_ref[s] = (x0_ref[s].astype(F32) * yh).astype(BF16)


def _short_conv(z, x0, k2s, layer, consts, bias):
    b, L, c = z.shape
    n = 2 * L
    cb = min(c, 256)
    ff, fk, finv = consts
    kern = functools.partial(_short_conv_kernel, L=L)
    return pl.pallas_call(
        kern,
        grid=(b // 2, c // cb),
        in_specs=[pl.BlockSpec((2, L, cb), lambda p, j: (p, 0, j)),
                  pl.BlockSpec((2, L, cb), lambda p, j: (p, 0, j)),
                  pl.BlockSpec((1, n, cb), lambda p, j: (layer, 0, j)),
                  pl.BlockSpec((2 * n, 2 * L), lambda p, j: (0, 0)),
                  pl.BlockSpec((2 * n, n), lambda p, j: (0, 0)),
                  pl.BlockSpec((2 * L, 2 * n), lambda p, j: (0, 0)),
                  pl.BlockSpec((1, 1, cb), lambda p, j: (layer, 0, j))],
        out_specs=pl.BlockSpec((2, L, cb), lambda p, j: (p, 0, j)),
        out_shape=jax.ShapeDtypeStruct((b, L, c), BF16),
        compiler_params=_cparams("parallel", "parallel"),
        name="short_conv",
    )(z, x0, k2s, ff, fk, finv, bias)


def _s5_matrices(lam_re, lam_im, log_step, b_re, b_im, c_re, c_im, d):
    T = S5_CHUNK
    lr, li = lam_re.astype(F32), lam_im.astype(F32)
    dt = jnp.exp(log_step.astype(F32))[..., None]
    mag = jnp.exp(lr * dt)
    a_r, a_i = mag * jnp.cos(li * dt), mag * jnp.sin(li * dt)
    den = lr * lr + li * li
    q_r = ((a_r - 1.0) * lr + a_i * li) / den
    q_i = (a_i * lr - (a_r - 1.0) * li) / den
    br, bi = b_re.astype(F32), b_im.astype(F32)
    bb_r = q_r[..., None] * br - q_i[..., None] * bi
    bb_i = q_r[..., None] * bi + q_i[..., None] * br
    D, _, G, P, H = bb_r.shape
    pr, pi = [jnp.ones_like(a_r)], [jnp.zeros_like(a_i)]
    for _ in range(T):
        pr.append(pr[-1] * a_r - pi[-1] * a_i)
        pi.append(pr[-2] * a_i + pi[-1] * a_r)
    pw_r, pw_i = jnp.stack(pr, axis=-1), jnp.stack(pi, axis=-1)
    ct_r = jnp.swapaxes(c_re.astype(F32), -1, -2)
    ct_i = jnp.swapaxes(c_im.astype(F32), -1, -2)
    cat_r = (pw_r[..., :, None] * ct_r[..., None, :]
             - pw_i[..., :, None] * ct_i[..., None, :]).reshape(D, 2, G, P, (T + 1) * H)
    cat_i = (pw_i[..., :, None] * ct_r[..., None, :]
             + pw_r[..., :, None] * ct_i[..., None, :]).reshape(D, 2, G, P, (T + 1) * H)
    kall = (jnp.einsum('dkgph,dkgpn->dkghn', bb_r, cat_r[..., :T * H], precision=HIGHEST)
            - jnp.einsum('dkgph,dkgpn->dkghn', bb_i, cat_i[..., :T * H], precision=HIGHEST))
    kf = kall[:, 0]
    kb_rev = jnp.flip(kall[:, 1].reshape(D, G, H, T, H), axis=-2).reshape(D, G, H, T * H)
    zpad = jnp.zeros((D, G, H, (T - 1) * H), F32)
    kf_pad = jnp.concatenate([zpad, kf], axis=-1)
    kb_pad = jnp.concatenate([kb_rev, zpad], axis=-1)
    rows = [kf_pad[..., (T - 1 - j) * H:(2 * T - 1 - j) * H] + kb_pad[..., (T - 1 - j) * H:(2 * T - 1 - j) * H]
            for j in range(T)]
    m_mat = jnp.stack(rows, axis=2).reshape(D, G, T * H, T * H)
    bt_r, bt_i = jnp.swapaxes(bb_r, -1, -2), jnp.swapaxes(bb_i, -1, -2)
    pj_r = jnp.moveaxis(pw_r[..., :T], -1, -2)
    pj_i = jnp.moveaxis(pw_i[..., :T], -1, -2)

    def end_state(k, flip):
        qr, qi = pj_r[:, k], pj_i[:, k]
        if flip:
            qr, qi = jnp.flip(qr, axis=-2), jnp.flip(qi, axis=-2)
        er = qr[..., :, None, :] * bt_r[:, k][..., None, :, :] - qi[..., :, None, :] * bt_i[:, k][..., None, :, :]
        ei = qr[..., :, None, :] * bt_i[:, k][..., None, :, :] + qi[..., :, None, :] * bt_r[:, k][..., None, :, :]
        return er.reshape(D, G, T * H, P), ei.reshape(D, G, T * H, P)

    ef_r, ef_i = end_state(0, True)
    eb_r, eb_i = end_state(1, False)
    e_mat = jnp.concatenate([ef_r, eb_r, ef_i, eb_i], axis=-1)
    cf_r, cf_i = cat_r[:, 0][..., H:], -cat_i[:, 0][..., H:]
    rev = lambda v: jnp.flip(v.reshape(D, G, P, T, H), axis=-2).reshape(D, G, P, T * H)
    cb_r, cb_i = rev(cat_r[:, 1][..., H:]), rev(-cat_i[:, 1][..., H:])
    zp = jnp.zeros_like(cf_r)
    w_out = jnp.concatenate([m_mat, cf_r, zp, cf_i, zp, zp, cb_r, zp, cb_i], axis=-2)
    at_r = jnp.concatenate([pw_r[:, 0, :, :, T], pw_r[:, 1, :, :, T]], axis=-1)[:, :, None, :]
    at_i = jnp.concatenate([pw_i[:, 0, :, :, T], pw_i[:, 1, :, :, T]], axis=-1)[:, :, None, :]
    d_t = jnp.tile(d.astype(F32).reshape(D, G, 1, H), (1, 1, 1, T))
    return e_mat.astype(BF16), w_out.astype(BF16), at_r, at_i, d_t


def _s5_kernel(x_ref, e_ref, w_ref, ar_ref, ai_ref, d_ref, init_ref, y_ref, fin_ref,
               e_scr, sa_scr, sb_scr, *, gb, nb, nc, p2):
    nt = nc // 2
    lane = lax.broadcasted_iota(jnp.int32, (2 * nb, p2), 1)
    is_fwd = lane < (p2 // 2)
    first = lax.broadcasted_iota(jnp.int32, (2 * nb, p2), 0) < nb
    swap = lambda v: pltpu.roll(v, nb, axis=0)
    for g in range(gb):
        e_scr[g] = _dot(x_ref[g], e_ref[g])

    def step(k, carry):
        out = []
        rf = pl.ds(pl.multiple_of(k * 2 * nb, 2 * nb), 2 * nb)
        rb = pl.ds(pl.multiple_of((nt - 1 - k) * 2 * nb, 2 * nb), 2 * nb)
        for g in range(gb):
            cr, ci = carry[2 * g], carry[2 * g + 1]
            ar, ai = ar_ref[g], ai_ref[g]
            er = jnp.where(is_fwd, e_scr[g, rf, 0:p2], swap(e_scr[g, rb, 0:p2]))
            ei = jnp.where(is_fwd, e_scr[g, rf, p2:2 * p2], swap(e_scr[g, rb, p2:2 * p2]))
            ur = ar * cr - ai * ci + er
            ui = ar * ci + ai * cr + ei
            ur4, ui4 = swap(ur), swap(ui)
            sr = jnp.where(first, cr, ur4)
            si = jnp.where(first, ci, ui4)
            sa_scr[g, rf, 0:p2] = sr
            sa_scr[g, rf, p2:2 * p2] = si
            sb_scr[g, rb, 0:p2] = swap(sr)
            sb_scr[g, rb, p2:2 * p2] = swap(si)
            xr = jnp.where(first, ur, ur4)
            xi = jnp.where(first, ui, ui4)
            zr = ar * xr - ai * xi + er
            zi = ar * xi + ai * xr + ei
            out.append(jnp.where(first, swap(zr), zr))
            out.append(jnp.where(first, swap(zi), zi))
        return tuple(out)

    init = []
    for g in range(gb):
        init += [init_ref[g, :, 0:p2], init_ref[g, :, p2:2 * p2]]
    fin = lax.fori_loop(0, nt, step, tuple(init))
    for g in range(gb):
        fin_ref[g] = jnp.concatenate([fin[2 * g], fin[2 * g + 1]], axis=1)
        x = x_ref[g]
        lhs = jnp.concatenate([x, sa_scr[g].astype(BF16), sb_scr[g].astype(BF16)], axis=1)
        y = _dot(lhs, w_ref[g]) + x.astype(F32) * d_ref[g]
        y_ref[g] = jax.nn.gelu(y).astype(BF16)


def _s5_scan(xg, mats, layer, init, nb):
    e_mat, w_out, at_r, at_i, d_t = mats
    G, R, th = xg.shape
    p4 = e_mat.shape[-1]
    p2 = p4 // 2
    gb = S5_GROUPS_PER_STEP if G % S5_GROUPS_PER_STEP == 0 else 2
    nc = R // nb
    assert 2 * nb == 8 and nc % 2 == 0, "two chunks of batch rows must fill one 8-sublane tile"
    kern = functools.partial(_s5_kernel, gb=gb, nb=nb, nc=nc, p2=p2)
    g3 = lambda i: (i, 0, 0)
    l4 = lambda i: (layer, i, 0, 0)
    return pl.pallas_call(
        kern,
        grid=(G // gb,),
        in_specs=[pl.BlockSpec((gb, R, th), g3),
                  pl.BlockSpec((None, gb, th, p4), l4),
                  pl.BlockSpec((None, gb, th + 2 * p4, th), l4),
                  pl.BlockSpec((None, gb, 1, p2), l4),
                  pl.BlockSpec((None, gb, 1, p2), l4),
                  pl.BlockSpec((None, gb, 1, th), l4),
                  pl.BlockSpec((gb, 2 * nb, p4), g3)],
        out_specs=[pl.BlockSpec((gb, R, th), g3),
                   pl.BlockSpec((gb, 2 * nb, p4), g3)],
        out_shape=[jax.ShapeDtypeStruct((G, R, th), BF16),
                   jax.ShapeDtypeStruct((G, 2 * nb, p4), F32)],
        scratch_shapes=[pltpu.VMEM((gb, R, p4), F32),
                        pltpu.VMEM((gb, R, p4), F32),
                        pltpu.VMEM((gb, R, p4), F32)],
        compiler_params=_cparams("parallel"),
        name="s5_scan",
    )(xg, e_mat, w_out, at_r, at_i, d_t, init)


def _lane_group(rows, h):
    return lax.broadcasted_iota(jnp.int32, (rows, LANES), 1) // h


def _block_transpose(vs, h):
    n = len(vs)
    assert n * h == LANES and n & (n - 1) == 0
    blk = _lane_group(vs[0].shape[0], h)
    s = 1
    while s < n:
        upper = (blk & s) != 0
        out = list(vs)
        for i in range(n):
            if i & s == 0:
                a, b = vs[i], vs[i | s]
                out[i] = jnp.where(upper, pltpu.roll(b, s * h, axis=1), a)
                out[i | s] = jnp.where(upper, b, pltpu.roll(a, LANES - s * h, axis=1))
        vs = out
        s *= 2
    return vs


def _s5_pack_kernel(u_ref, o_ref, t_scr, *, h, rows):
    gl = LANES // h
    t_scr[...] = _swap_major(u_ref[...].astype(F32))
    pr = min(rows, PACK_ROWS)
    for rb in range(rows // pr):
        rs = slice(rb * pr, (rb + 1) * pr)
        for half in range(S5_CHUNK // gl):
            xs = _block_transpose([t_scr[half * gl + jj, rs, :] for jj in range(gl)], h)
            for g in range(gl):
                o_ref[g, rs, half * LANES:(half + 1) * LANES] = xs[g].astype(BF16)


def _s5_unpack_kernel(y_ref, o_ref, t_scr, *, h, rows):
    gl = LANES // h
    pr = min(rows, PACK_ROWS)
    for rb in range(rows // pr):
        rs = slice(rb * pr, (rb + 1) * pr)
        for half in range(S5_CHUNK // gl):
            ys = _block_transpose([y_ref[g, rs, half * LANES:(half + 1) * LANES].astype(F32) for g in range(gl)], h)
            for tt in range(gl):
                t_scr[half * gl + tt, rs, :] = ys[tt]
    o_ref[...] = _swap_major(t_scr[...]).astype(BF16)


def _s5_pack(u3, h):
    R, t, w = u3.shape
    gl = LANES // h
    rows = min(R, 256)
    kern = functools.partial(_s5_pack_kernel, h=h, rows=rows)
    return pl.pallas_call(
        kern,
        grid=(w // LANES, R // rows),
        in_specs=[pl.BlockSpec((rows, t, LANES), lambda l, i: (i, 0, l))],
        out_specs=pl.BlockSpec((gl, rows, t * h), lambda l, i: (l, i, 0)),
        out_shape=jax.ShapeDtypeStruct((w // h, R, t * h), BF16),
        scratch_shapes=[pltpu.VMEM((t, rows, LANES), F32)],
        compiler_params=_cparams("parallel", "parallel"),
        name="s5_pack",
    )(u3)


def _s5_unpack(yg, h):
    G, R, th = yg.shape
    t = th // h
    gl = LANES // h
    rows = min(R, 256)
    kern = functools.partial(_s5_unpack_kernel, h=h, rows=rows)
    return pl.pallas_call(
        kern,
        grid=(G // gl, R // rows),
        in_specs=[pl.BlockSpec((gl, rows, th), lambda l, i: (l, i, 0))],
        out_specs=pl.BlockSpec((rows, t, LANES), lambda l, i: (i, 0, l)),
        out_shape=jax.ShapeDtypeStruct((R, t, G * h), BF16),
        scratch_shapes=[pltpu.VMEM((t, rows, LANES), F32)],
        compiler_params=_cparams("parallel", "parallel"),
        name="s5_unpack",
    )(yg)


def _s5_mixer(u4, mats, layer, init, h):
    lc, nb, t, w = u4.shape
    yg, fin = _s5_scan(_s5_pack(u4.reshape(lc * nb, t, w), h), mats, layer, init, nb)
    return _s5_unpack(yg, h).reshape(lc, nb, t, w), fin


def _outproj_kernel(yh_ref, ys_ref, x_ref, mod_ref, g_ref, wg_ref, bg_ref, wo_ref,
                    xo_ref, hx_ref, *, d, dh, tm):
    gate = mod_ref[0, :, 2 * d:3 * d]
    shift = mod_ref[0, :, 3 * d:4 * d]
    scale = mod_ref[0, :, 4 * d:5 * d]
    sub = tm // SUB_TILES
    rows = [slice(s * sub, (s + 1) * sub) for s in range(SUB_TILES)]
    ys = [ys_ref[s * sub // S5_CHUNK:(s + 1) * sub // S5_CHUNK, 0, :, :].reshape(sub, ys_ref.shape[-1])
          for s in range(SUB_TILES)]
    pre = [_dot(y, wg_ref[0]) for y in ys]
    glu = [(y.astype(F32) * jax.nn.sigmoid(a + bg_ref[0])).astype(BF16) for y, a in zip(ys, pre)]
    yx = [_dot(yh_ref[0, rs, :], wo_ref[0, 0:dh, :]) + _dot(gl, wo_ref[0, dh:, :]) for rs, gl in zip(rows, glu)]
    for rs, v in zip(rows, yx):
        xo = x_ref[0, rs, :] + gate * _rms(v, g_ref[0, 1:2, :])
        xo_ref[0, rs, :] = xo
        hx_ref[0, rs, :] = (_rms(xo, g_ref[0, 2:3, :]) * (1.0 + scale) + shift).astype(BF16)


def _outproj(yh, ys4, x, mod, mod_row, layer, norm_g, w_glu, b_glu, w_out, tm):
    b, s, d = x.shape
    dh = yh.shape[-1]
    ds5 = ys4.shape[-1]
    tc = tm // S5_CHUNK
    kern = functools.partial(_outproj_kernel, d=d, dh=dh, tm=tm)
    lyr = lambda bi, i: (layer, 0, 0)
    t3 = lambda bi, i: (bi, i, 0)
    return pl.pallas_call(
        kern,
        grid=(b, s // tm),
        in_specs=[pl.BlockSpec((1, tm, dh), t3),
                  pl.BlockSpec((tc, 1, S5_CHUNK, ds5), lambda bi, i: (i, bi, 0, 0)),
                  pl.BlockSpec((1, tm, d), t3),
                  pl.BlockSpec((1, 1, mod.shape[-1]), lambda bi, i: (mod_row(bi), 0, 0)),
                  pl.BlockSpec((1,) + norm_g.shape[1:], lyr),
                  pl.BlockSpec((1, ds5, ds5), lyr), pl.BlockSpec((1, 1, ds5), lyr),
                  pl.BlockSpec((1, dh + ds5, d), lyr)],
        out_specs=[pl.BlockSpec((1, tm, d), t3), pl.BlockSpec((1, tm, d), t3)],
        out_shape=[jax.ShapeDtypeStruct((b, s, d), F32), jax.ShapeDtypeStruct((b, s, d), BF16)],
        compiler_params=_cparams("parallel", "parallel"),
        name="outproj",
    )(yh, ys4, x, mod, norm_g, w_glu, b_glu, w_out)


def _ffn_kernel(hm_ref, hp_ref, hn_ref, x_ref, mod_ref, g_ref, wg_ref, wv_ref, cw_ref, cb_ref, wd_ref,
                o_ref, acc_ref, *, tm, d, wg, vertical, kp):
    k = pl.program_id(3)
    i = pl.program_id(1) * kp + k
    nt = pl.num_programs(1) * kp
    j = pl.program_id(2)
    nj = pl.num_programs(2)
    hm = hm_ref[0]
    cw = cw_ref[0]
    if vertical:
        top = jnp.where(i > 0, 1.0, 0.0).astype(BF16)
        bot = jnp.where(i < nt - 1, 1.0, 0.0).astype(BF16)
        ha = jnp.concatenate([hp_ref[0] * top, hm, hn_ref[0] * bot], axis=0)
    else:
        ha = hm
    g = _dot(ha, wg_ref[0])

    ns = FFN_SUB_TILES if tm % (FFN_SUB_TILES * wg) == 0 else 1
    sub = tm // ns

    def vcol(dx, r0):
        if not vertical:
            return g[r0:r0 + sub] * cw[3 + dx:4 + dx]
        return (g[r0:r0 + sub] * cw[dx:dx + 1] + g[r0 + wg:r0 + wg + sub] * cw[3 + dx:4 + dx]
                + g[r0 + 2 * wg:r0 + 2 * wg + sub] * cw[6 + dx:7 + dx])

    col = lax.broadcasted_iota(jnp.int32, (sub, g.shape[-1]), 0) & (wg - 1)
    vs = [_dot(hm[s * sub:(s + 1) * sub], wv_ref[0]) for s in range(ns)]
    hmid = []
    for s in range(ns):
        r0 = s * sub
        conv = (vcol(1, r0) + jnp.where(col > 0, pltpu.roll(vcol(0, r0), 1, axis=0), 0.0)
                + jnp.where(col < wg - 1, pltpu.roll(vcol(2, r0), sub - 1, axis=0), 0.0) + cb_ref[0])
        hmid.append((jax.nn.gelu(conv) * vs[s]).astype(BF16))
    parts = [_dot(h, wd_ref[0]) for h in hmid]

    @pl.when(j == 0)
    def _():
        for s in range(ns):
            acc_ref[k, s * sub:(s + 1) * sub, :] = parts[s]

    @pl.when(j > 0)
    def _():
        for s in range(ns):
            acc_ref[k, s * sub:(s + 1) * sub, :] += parts[s]

    @pl.when(j == nj - 1)
    def _():
        gate = mod_ref[0, :, 5 * d:6 * d]
        o_ref[0] = x_ref[0] + gate * _rms(acc_ref[k], g_ref[0, 3:4, :])


def _ffn(hx, x, mod, mod_row, layer, norm_g, w_up, conv_w, conv_b, w_down, tm, wg, vertical):
    b, s, d = x.shape
    f = w_down.shape[1]
    fc = f // 2 if (f // 2) % LANES == 0 else f
    nf = f // fc
    nt = s // tm
    r = tm // wg if vertical else 1
    hb = wg if vertical else 16
    nhb = s // hb
    kp = FFN_TILE_GROUP if nt % FFN_TILE_GROUP == 0 else 1
    kern = functools.partial(_ffn_kernel, tm=tm, d=d, wg=wg, vertical=vertical, kp=kp)
    tile = lambda ip, k: ip * kp + k
    cur = lambda bi, ip, j, k: (bi, tile(ip, k), 0)
    fin = lambda bi, ip, j, k: (bi, tile(ip, jnp.where(j == nf - 1, k, 0)), 0)
    return pl.pallas_call(
        kern,
        grid=(b, nt // kp, nf, kp),
        in_specs=[pl.BlockSpec((1, tm, d), cur),
                  pl.BlockSpec((1, hb, d), lambda bi, ip, j, k: (bi, jnp.maximum(tile(ip, k) * r - 1, 0), 0)),
                  pl.BlockSpec((1, hb, d), lambda bi, ip, j, k: (bi, jnp.minimum((tile(ip, k) + 1) * r, nhb - 1), 0)),
                  pl.BlockSpec((1, tm, d), fin),
                  pl.BlockSpec((1, 1, mod.shape[-1]), lambda bi, ip, j, k: (mod_row(bi), 0, 0)),
                  pl.BlockSpec((1,) + norm_g.shape[1:], lambda bi, ip, j, k: (layer, 0, 0)),
                  pl.BlockSpec((1, d, fc), lambda bi, ip, j, k: (layer, 0, j)),
                  pl.BlockSpec((1, d, fc), lambda bi, ip, j, k: (layer, 0, nf + j)),
                  pl.BlockSpec((1, 9, fc), lambda bi, ip, j, k: (layer, 0, j)),
                  pl.BlockSpec((1, 1, fc), lambda bi, ip, j, k: (layer, 0, j)),
                  pl.BlockSpec((1, fc, d), lambda bi, ip, j, k: (layer, j, 0))],
        out_specs=pl.BlockSpec((1, tm, d), fin),
        out_shape=jax.ShapeDtypeStruct((b, s, d), F32),
        scratch_shapes=[pltpu.VMEM((kp, tm, d), F32)],
        compiler_params=_cparams("parallel", "parallel", "arbitrary", "arbitrary"),
        name="conv_glu_ffn",
    )(hx, hx, hx, x, mod, norm_g, w_up, w_up, conv_w, conv_b, w_down)


def kernel(x, c, ctx, c_ctx, w_ada, b_ada, norm_g, w_in, hy_short_w, hy_short_b,
           filt_w_in, filt_b_in, filt_w_hid, filt_b_hid, filt_freq, filt_w_out, hy_bias,
           s5_lam_re, s5_lam_im, s5_log_step, s5_b_re, s5_b_im, s5_c_re, s5_c_im, s5_d,
           s5_w_glu, s5_b_glu, w_out, ffn_w_up, ffn_conv_w, ffn_conv_b, ffn_w_down):
    depth = w_ada.shape[0]
    bsz, seq, d = x.shape
    lctx = ctx.shape[1]
    dh = hy_bias.shape[-1]
    G, P, H = s5_b_re.shape[2], s5_b_re.shape[3], s5_b_re.shape[4]
    dff = ffn_w_down.shape[1]
    assert bsz % 2 == 0 and bsz <= 4 and seq % GRID_W == 0 and GRID_W & (GRID_W - 1) == 0

    cond = jnp.zeros((8, d), F32).at[:bsz].set(c).at[bsz].set(c_ctx)
    mod = _ada_mod(cond, w_ada, b_ada).reshape(depth * 8, 1, 6 * d)

    filt_args = (filt_w_in, filt_b_in, filt_w_hid, filt_b_hid, filt_freq, filt_w_out)
    long_conv = _LongConv(seq, dh, _hyena_filters(seq, *filt_args, dh))
    k_ctx = _hyena_filters(lctx, *filt_args, dh)
    short_c = tuple(jnp.asarray(a, BF16) for a in _short_consts(lctx))

    w_in_b = w_in.astype(BF16)
    w_glu_b = s5_w_glu.astype(BF16)
    w_out_b = w_out.astype(BF16)
    w_up_b = ffn_w_up.astype(BF16)
    w_down_b = ffn_w_down.astype(BF16)
    sb = hy_short_b.reshape(depth, 1, 3 * dh)
    cw = ffn_conv_w.reshape(depth, 9, dff)
    cb = ffn_conv_b.reshape(depth, 1, dff)
    bg = s5_b_glu.reshape(depth, 1, -1)
    hb = hy_bias.reshape(depth, 1, dh)
    mats = _s5_matrices(s5_lam_re, s5_lam_im, s5_log_step, s5_b_re, s5_b_im, s5_c_re, s5_c_im, s5_d)

    tm = min(seq, ROW_TILE)
    tp = min(seq, PROJ_ROW_TILE)
    for l in range(depth):
        last = l == depth - 1
        row_x = lambda bi, l=l: 8 * l + bi
        row_c = lambda bi, l=l: 8 * l + bsz

        x0c, zc, uc = _inproj(ctx, mod, row_c, l, norm_g, w_in_b, hy_short_w, sb, dh, lctx)
        ysc, ctx_state = _s5_mixer(uc, mats, l, jnp.zeros((G, 2 * bsz, 4 * P), F32), H)

        x0, z, u = _inproj(x, mod, row_x, l, norm_g, w_in_b, hy_short_w, sb, dh, tp)
        ys, _ = _s5_mixer(u, mats, l, ctx_state, H)
        yh = long_conv(z, x0, hb, l)
        x, hx = _outproj(yh, ys, x, mod, row_x, l, norm_g, w_glu_b, bg, w_out_b, tp)
        x = _ffn(hx, x, mod, row_x, l, norm_g, w_up_b, cw, cb, w_down_b, tm, GRID_W, True)

        if not last:
            yhc = _short_conv(zc, x0c, k_ctx, l, short_c, hb)
            ctx, hc = _outproj(yhc, ysc, ctx, mod, row_c, l, norm_g, w_glu_b, bg, w_out_b, lctx)
            ctx = _ffn(hc, ctx, mod, row_c, l, norm_g, w_up_b, cw, cb, w_down_b, lctx, lctx, False)
    return x
```

```python
import functools
import math

import numpy as np
import jax
import jax.numpy as jnp
from jax import lax
from jax.experimental import pallas as pl
from jax.experimental.pallas import tpu as pltpu

GRID_W = 64
RMS_EPS = 1e-6
DECAY_TARGET = 1e-2
FAST_DECAY_PCT = 0.3
SLOW_DECAY_PCT = 1.5
S5_CHUNK = 16
S5_GROUPS_PER_STEP = 4
LANES = 128
HALO = 16
ROW_TILE = 512
PROJ_ROW_TILE = 1024
SUB_TILES = 8
OUT_SUB_ROWS = 128
IN_SUB_TILES = 2
FFN_COL_ALIGN = 256
FFN_TILE_GROUP = 2
PACK_ROWS = 128
VMEM_LIMIT = 56 * 1024 * 1024

F32 = jnp.float32
BF16 = jnp.bfloat16
HIGHEST = lax.Precision.HIGHEST


def _cparams(*sem):
    return pltpu.CompilerParams(dimension_semantics=sem, vmem_limit_bytes=VMEM_LIMIT)


def _dot(a, b, **kw):
    return jnp.dot(a, b, preferred_element_type=F32, **kw)


def _ada_kernel(cond_ref, w_ref, b_ref, o_ref):
    cv = cond_ref[...]
    s = cv * jax.nn.sigmoid(cv)
    o_ref[0] = _dot(s, w_ref[0], precision=HIGHEST) + b_ref[0]


def _ada_mod(cond, w_ada, b_ada):
    depth, d, n = w_ada.shape
    tn = n // 4
    return pl.pallas_call(
        _ada_kernel,
        grid=(depth, n // tn),
        in_specs=[pl.BlockSpec((8, d), lambda l, j: (0, 0)),
                  pl.BlockSpec((1, d, tn), lambda l, j: (l, 0, j)),
                  pl.BlockSpec((1, 1, tn), lambda l, j: (l, 0, j))],
        out_specs=pl.BlockSpec((1, 8, tn), lambda l, j: (l, 0, j)),
        out_shape=jax.ShapeDtypeStruct((depth, 8, n), F32),
        compiler_params=_cparams("parallel", "parallel"),
        name="ada_mod",
    )(cond, w_ada, b_ada.reshape(depth, 1, n))


def _rms(v, g):
    ms = jnp.mean(v * v, axis=-1, keepdims=True)
    return v * lax.rsqrt(ms + RMS_EPS) * g


def _inproj_kernel(xm_ref, xp_ref, xn_ref, mod_ref, g_ref, w_ref, sw_ref, sb_ref,
                   x0_ref, z_ref, u_ref, p_scr, *, tm, d, dh):
    i = pl.program_id(1)
    nt = pl.num_programs(1)
    shift = mod_ref[0, :, 0:d]
    scale = mod_ref[0, :, d:2 * d]
    xa = jnp.concatenate([xp_ref[0], xm_ref[0], xn_ref[0]], axis=0)
    sw = sw_ref[0]
    ns = IN_SUB_TILES
    sub = tm // ns
    cut = [0] + [HALO + s * sub for s in range(1, ns)] + [tm + 2 * HALO]

    def project(s):
        lo, hi = cut[s], cut[s + 1]
        xn = (_rms(xa[lo:hi], g_ref[0, 0:1, :]) * (1.0 + scale) + shift).astype(BF16)
        p = _dot(xn, w_ref[0])
        p_scr[lo:hi, :] = p[:, :3 * dh]
        if s == 0:
            inside = jnp.where(i > 0, 1.0, 0.0).astype(F32)
            p_scr[HALO - 8:HALO, :] = p[HALO - 8:HALO, :3 * dh] * inside
        if s == ns - 1:
            inside = jnp.where(i < nt - 1, 1.0, 0.0).astype(F32)
            p_scr[HALO + tm:HALO + tm + 8, :] = p[HALO + tm - lo:HALO + tm + 8 - lo, :3 * dh] * inside
        a, b = max(lo, HALO), min(hi, HALO + tm)
        u = p[a - lo:b - lo, 3 * dh:].astype(BF16)
        u_ref[(a - HALO) // S5_CHUNK:(b - HALO) // S5_CHUNK, 0, :, :] = u.reshape(
            (b - a) // S5_CHUNK, S5_CHUNK, u.shape[-1])

    def conv_gate(s):
        r0 = HALO + s * sub
        conv = (p_scr[pl.ds(r0 - 1, sub), :] * sw[0:1] + p_scr[pl.ds(r0, sub), :] * sw[1:2]
                + p_scr[pl.ds(r0 + 1, sub), :] * sw[2:3] + sb_ref[0])
        rs = slice(s * sub, (s + 1) * sub)
        x0_ref[0, rs, :] = conv[:, :dh].astype(BF16)
        z_ref[0, rs, :] = (conv[:, dh:2 * dh] * conv[:, 2 * dh:]).astype(BF16)

    project(0)
    for s in range(ns):
        if s + 1 < ns:
            project(s + 1)
        conv_gate(s)


def _inproj(x, mod, mod_row, layer, norm_g, w_in, sw, sb, dh, tm):
    b, s, d = x.shape
    dp = w_in.shape[-1]
    ds5 = dp - 3 * dh
    nt = s // tm
    r = tm // HALO
    nh = s // HALO
    tc = tm // S5_CHUNK
    kern = functools.partial(_inproj_kernel, tm=tm, d=d, dh=dh)
    lyr = lambda bi, i: (layer, 0, 0)
    return pl.pallas_call(
        kern,
        grid=(b, nt),
        in_specs=[pl.BlockSpec((1, tm, d), lambda bi, i: (bi, i, 0)),
                  pl.BlockSpec((1, HALO, d), lambda bi, i: (bi, jnp.maximum(i * r - 1, 0), 0)),
                  pl.BlockSpec((1, HALO, d), lambda bi, i: (bi, jnp.minimum((i + 1) * r, nh - 1), 0)),
                  pl.BlockSpec((1, 1, mod.shape[-1]), lambda bi, i: (mod_row(bi), 0, 0)),
                  pl.BlockSpec((1,) + norm_g.shape[1:], lyr),
                  pl.BlockSpec((1, d, dp), lyr),
                  pl.BlockSpec((1, 3, 3 * dh), lyr),
                  pl.BlockSpec((1, 1, 3 * dh), lyr)],
        out_specs=[pl.BlockSpec((1, tm, dh), lambda bi, i: (bi, i, 0)),
                   pl.BlockSpec((1, tm, dh), lambda bi, i: (bi, i, 0)),
                   pl.BlockSpec((tc, 1, S5_CHUNK, ds5), lambda bi, i: (i, bi, 0, 0))],
        out_shape=[jax.ShapeDtypeStruct((b, s, dh), BF16),
                   jax.ShapeDtypeStruct((b, s, dh), BF16),
                   jax.ShapeDtypeStruct((s // S5_CHUNK, b, S5_CHUNK, ds5), BF16)],
        scratch_shapes=[pltpu.VMEM((tm + 2 * HALO, 3 * dh), F32)],
        compiler_params=_cparams("parallel", "arbitrary"),
        name="inproj",
    )(x, x, x, mod, norm_g, w_in, sw, sb)


def _filter_feats(L, emb):
    bands = (emb - 1) // 2
    t = np.linspace(0.0, 1.0, L, dtype=np.float32).astype(np.float64)[:, None]
    w = (2.0 * math.pi / L) * np.arange(L, dtype=np.float64)[:, None]
    f = np.linspace(1e-4, bands - 1, bands, dtype=np.float32).astype(np.float64)[None, :]
    z = np.concatenate([t, np.cos(f * w), -np.sin(f * w)], axis=-1)
    zp = np.zeros((L, LANES), np.float32)
    zp[:, :emb] = z
    return zp


def _filter_kernel(z_ref, win_ref, bin_ref, whid_ref, bhid_ref, fr_ref, wf_ref, wb_ref, dl_ref, o_ref):
    z = z_ref[...]
    fr = fr_ref[0]
    h = jnp.sin(fr * (_dot(z, win_ref[0], precision=HIGHEST) + bin_ref[0]))
    for i in range(whid_ref.shape[1]):
        h = jnp.sin(fr * (_dot(h, whid_ref[0, i], precision=HIGHEST) + bhid_ref[0, i]))
    o_ref[0, 0] = _dot(h, wf_ref[0], precision=HIGHEST) * jnp.exp(-z[:, 0:1] * dl_ref[...])
    hb = _dot(h, wb_ref[0], precision=HIGHEST) * jnp.exp(-z[:, LANES:LANES + 1] * dl_ref[...])
    first = (pl.program_id(1) == 0) & (lax.broadcasted_iota(jnp.int32, (z.shape[0], 1), 0) == 0)
    o_ref[0, 1] = jnp.where(first, 0.0, hb)


def _block_diag2(w):
    zero = jnp.zeros_like(w)
    return jnp.concatenate([jnp.concatenate([w, zero], axis=-1), jnp.concatenate([zero, w], axis=-1)], axis=-2)


def _hyena_filters(L, f_w_in, f_b_in, f_w_hid, f_b_hid, f_freq, f_w_out, dh):
    depth, emb, hid = f_w_in.shape
    n_inner = f_w_hid.shape[1]
    tl = min(L, 1024)
    z1 = _filter_feats(L, emb)
    z = jnp.asarray(np.concatenate([z1, np.concatenate([z1[:1], z1[:0:-1]], axis=0)], axis=1))
    win = _block_diag2(jnp.zeros((depth, LANES, hid), F32).at[:, :emb].set(f_w_in))
    zero = jnp.zeros((depth, hid, dh), F32)
    wf = jnp.concatenate([f_w_out[:, :, :dh], zero], axis=1)
    wb = jnp.concatenate([zero, f_w_out[:, :, dh:]], axis=1)
    twice = lambda v, shp: jnp.tile(v.reshape(shp), (1,) * (len(shp) - 1) + (2,))
    deltas = np.abs(np.linspace(math.log(DECAY_TARGET) / FAST_DECAY_PCT,
                                math.log(DECAY_TARGET) / SLOW_DECAY_PCT, dh, dtype=np.float32))[None, :]
    h2 = 2 * hid
    out = pl.pallas_call(
        _filter_kernel,
        grid=(depth, L // tl),
        in_specs=[pl.BlockSpec((tl, 2 * LANES), lambda l, i: (i, 0)),
                  pl.BlockSpec((1, 2 * LANES, h2), lambda l, i: (l, 0, 0)),
                  pl.BlockSpec((1, 1, h2), lambda l, i: (l, 0, 0)),
                  pl.BlockSpec((1, n_inner, h2, h2), lambda l, i: (l, 0, 0, 0)),
                  pl.BlockSpec((1, n_inner, 1, h2), lambda l, i: (l, 0, 0, 0)),
                  pl.BlockSpec((1, 1, h2), lambda l, i: (l, 0, 0)),
                  pl.BlockSpec((1, h2, dh), lambda l, i: (l, 0, 0)),
                  pl.BlockSpec((1, h2, dh), lambda l, i: (l, 0, 0)),
                  pl.BlockSpec((1, dh), lambda l, i: (0, 0))],
        out_specs=pl.BlockSpec((1, 2, tl, dh), lambda l, i: (l, 0, i, 0)),
        out_shape=jax.ShapeDtypeStruct((depth, 2, L, dh), F32),
        compiler_params=_cparams("parallel", "parallel"),
        name="hyena_filter",
    )(z, win, twice(f_b_in, (depth, 1, hid)), _block_diag2(f_w_hid), twice(f_b_hid, (depth, n_inner, 1, hid)),
      twice(f_freq, (depth, 1, hid)), wf, wb, jnp.asarray(deltas))
    return out.reshape(depth, 2 * L, dh)


def _dft_consts(n1, n2):
    n = n1 * n2
    n2h = n2 // 2
    k2 = np.arange(n2)[:, None]
    a = -2.0 * np.pi * k2 * np.arange(n2)[None, :] / n2
    fr, fi = np.cos(a), np.sin(a)
    f1 = np.block([[fr[:, :n2h], -fi[:, :n2h]], [fi[:, :n2h], fr[:, :n2h]]])
    f1k = np.concatenate([fr, fi], axis=0)
    cr, ci = fr[:n2h] / n, -fi[:n2h] / n
    f3 = np.block([[cr, -ci], [ci, cr]])
    b = -2.0 * np.pi * np.arange(n1)[:, None] * np.arange(n1)[None, :] / n1
    t = -2.0 * np.pi * np.arange(n2)[:, None] * np.arange(n1)[None, :] / n
    return dict(f1=f1.astype(np.float32), f1k=f1k.astype(np.float32), f3=f3.astype(np.float32),
                gr=np.cos(b).astype(np.float32), gi=np.sin(b).astype(np.float32),
                tr=np.cos(t).astype(np.float32)[:, None, :], ti=np.sin(t).astype(np.float32)[:, None, :])


def _tables_kernel(gr_ref, gi_ref, tr_ref, ti_ref, m1_ref, m2_ref, *, kb):
    gr, gi = gr_ref[...], gi_ref[...]
    for q in range(kb):
        tr, ti = tr_ref[q], ti_ref[q]
        re = gr * tr - gi * ti
        im = gr * ti + gi * tr
        m1_ref[q] = jnp.concatenate([jnp.concatenate([re, -im], axis=1),
                                     jnp.concatenate([im, re], axis=1)], axis=0).astype(BF16)
        ret, imt = re.T, im.T
        m2_ref[q] = jnp.concatenate([jnp.concatenate([ret, imt], axis=1),
                                     jnp.concatenate([-imt, ret], axis=1)], axis=0).astype(BF16)


def _dft_tables(c, n1, n2):
    kb = 8
    kern = functools.partial(_tables_kernel, kb=kb)
    shp = jax.ShapeDtypeStruct((n2, 2 * n1, 2 * n1), BF16)
    return pl.pallas_call(
        kern,
        grid=(n2 // kb,),
        in_specs=[pl.BlockSpec((n1, n1), lambda i: (0, 0)),
                  pl.BlockSpec((n1, n1), lambda i: (0, 0)),
                  pl.BlockSpec((kb, 1, n1), lambda i: (i, 0, 0)),
                  pl.BlockSpec((kb, 1, n1), lambda i: (i, 0, 0))],
        out_specs=[pl.BlockSpec((kb, 2 * n1, 2 * n1), lambda i: (i, 0, 0)),
                   pl.BlockSpec((kb, 2 * n1, 2 * n1), lambda i: (i, 0, 0))],
        out_shape=[shp, shp],
        compiler_params=_cparams("parallel"),
        name="dft_tables",
    )(jnp.asarray(c["gr"]), jnp.asarray(c["gi"]), jnp.asarray(c["tr"]), jnp.asarray(c["ti"]))


def _swap_major(v):
    return pltpu.einshape("abc->bac", v)


def _s1_kernel(z_ref, f_ref, o_ref, y_scr, *, n2, nb1):
    zt = [_swap_major(z_ref[s].astype(F32)) for s in range(2)]
    for j in range(nb1):
        x = jnp.concatenate([zt[0][j], zt[1][j]], axis=0).astype(BF16)
        y = _dot(f_ref[...], x)
        y_scr[0, j] = y[:n2]
        y_scr[1, j] = y[n2:]
    for s in range(2):
        o_ref[0, s] = _swap_major(y_scr[s]).astype(BF16)


def _s1k_kernel(k_ref, f_ref, o_ref, y_scr, *, n2, nb1):
    kt = _swap_major(k_ref[0])
    for j in range(nb1):
        y = _dot(f_ref[...], kt[j].astype(BF16))
        y_scr[0, j] = y[:n2]
        y_scr[1, j] = y[n2:]
    for s in range(2):
        o_ref[0, s] = _swap_major(y_scr[s]).astype(BF16)


def _dft_stage1(z4, f1, n2, nb1):
    b, n2h, n1, c = z4.shape
    kern = functools.partial(_s1_kernel, n2=n2, nb1=nb1)
    return pl.pallas_call(
        kern,
        grid=(b // 2, n1 // nb1),
        in_specs=[pl.BlockSpec((2, n2h, nb1, c), lambda p, j: (p, 0, j, 0)),
                  pl.BlockSpec((2 * n2, 2 * n2h), lambda p, j: (0, 0))],
        out_specs=pl.BlockSpec((1, 2, n2, nb1, c), lambda p, j: (p, 0, 0, j, 0)),
        out_shape=jax.ShapeDtypeStruct((b // 2, 2, n2, n1, c), BF16),
        scratch_shapes=[pltpu.VMEM((2, nb1, n2, c), F32)],
        compiler_params=_cparams("parallel", "parallel"),
        name="dft_stage1",
    )(z4, f1)


def _dft_stage1_filter(k4, f1k, nb1):
    depth, n2, n1, c = k4.shape
    kern = functools.partial(_s1k_kernel, n2=n2, nb1=nb1)
    return pl.pallas_call(
        kern,
        grid=(depth, n1 // nb1),
        in_specs=[pl.BlockSpec((1, n2, nb1, c), lambda l, j: (l, 0, j, 0)),
                  pl.BlockSpec((2 * n2, n2), lambda l, j: (0, 0))],
        out_specs=pl.BlockSpec((1, 2, n2, nb1, c), lambda l, j: (l, 0, 0, j, 0)),
        out_shape=jax.ShapeDtypeStruct((depth, 2, n2, n1, c), BF16),
        scratch_shapes=[pltpu.VMEM((2, nb1, n2, c), F32)],
        compiler_params=_cparams("parallel", "parallel"),
        name="dft_stage1_filter",
    )(k4, f1k)


def _s2k_kernel(b_ref, m1_ref, o_ref, *, kb, n1):
    for q in range(kb):
        rows = slice(q * n1, (q + 1) * n1)
        xin = jnp.concatenate([b_ref[0, 0, rows, :], b_ref[0, 1, rows, :]], axis=0)
        xf = _dot(m1_ref[q], xin)
        o_ref[0, 0, rows, :] = xf[:n1]
        o_ref[0, 1, rows, :] = xf[n1:]


def _dft_stage2_filter(bv, m1, n1, kb):
    depth, _, n, c = bv.shape
    kern = functools.partial(_s2k_kernel, kb=kb, n1=n1)
    return pl.pallas_call(
        kern,
        grid=(n // (kb * n1), depth),
        in_specs=[pl.BlockSpec((1, 2, kb * n1, c), lambda k, l: (l, 0, k, 0)),
                  pl.BlockSpec((kb, 2 * n1, 2 * n1), lambda k, l: (k, 0, 0))],
        out_specs=pl.BlockSpec((1, 2, kb * n1, c), lambda k, l: (l, 0, k, 0)),
        out_shape=jax.ShapeDtypeStruct((depth, 2, n, c), F32),
        compiler_params=_cparams("parallel", "parallel"),
        name="dft_stage2_filter",
    )(bv, m1)


def _s2_kernel(b_ref, m1_ref, kf_ref, m2_ref, o_ref, *, kb, n1):
    for q in range(kb):
        rows = slice(q * n1, (q + 1) * n1)
        xin = jnp.concatenate([b_ref[0, 0, rows, :], b_ref[0, 1, rows, :]], axis=0)
        xf = _dot(m1_ref[q], xin)
        xr, xi = xf[:n1], xf[n1:]
        kr, ki = kf_ref[0, 0, rows, :], kf_ref[0, 1, rows, :]
        yin = jnp.concatenate([xr * kr - xi * ki, xr * ki + xi * kr], axis=0).astype(BF16)
        g = _dot(m2_ref[q], yin)
        o_ref[0, 0, rows, :] = g[:n1].astype(BF16)
        o_ref[0, 1, rows, :] = g[n1:].astype(BF16)


def _dft_stage2(bv, m1, kf, layer, m2, n1, kb):
    p, _, n, c = bv.shape
    kern = functools.partial(_s2_kernel, kb=kb, n1=n1)
    return pl.pallas_call(
        kern,
        grid=(n // (kb * n1), p),
        in_specs=[pl.BlockSpec((1, 2, kb * n1, c), lambda k, q: (q, 0, k, 0)),
                  pl.BlockSpec((kb, 2 * n1, 2 * n1), lambda k, q: (k, 0, 0)),
                  pl.BlockSpec((1, 2, kb * n1, c), lambda k, q: (layer, 0, k, 0)),
                  pl.BlockSpec((kb, 2 * n1, 2 * n1), lambda k, q: (k, 0, 0))],
        out_specs=pl.BlockSpec((1, 2, kb * n1, c), lambda k, q: (q, 0, k, 0)),
        out_shape=jax.ShapeDtypeStruct((p, 2, n, c), BF16),
        compiler_params=_cparams("parallel", "arbitrary"),
        name="dft_stage2",
    )(bv, m1, kf, m2)


def _s3_kernel(g_ref, f_ref, z_ref, x0_ref, bias_ref, o_ref, y_scr, *, n2h, nb1):
    gt = [_swap_major(g_ref[0, s].astype(F32)) for s in range(2)]
    for j in range(nb1):
        gin = jnp.concatenate([gt[0][j], gt[1][j]], axis=0).astype(BF16)
        y = _dot(f_ref[...], gin)
        y_scr[0, j] = y[:n2h]
        y_scr[1, j] = y[n2h:]
    for s in range(2):
        yh = _swap_major(y_scr[s]) + z_ref[s].astype(F32) * bias_ref[0]
        o_ref[s] = (x0_ref[s].astype(F32) * yh).astype(BF16)


def _dft_stage3(g5, f3, z4, x04, bias, layer, nb1):
    p, _, n2, n1, c = g5.shape
    n2h = n2 // 2
    kern = functools.partial(_s3_kernel, n2h=n2h, nb1=nb1)
    return pl.pallas_call(
        kern,
        grid=(p, n1 // nb1),
        in_specs=[pl.BlockSpec((1, 2, n2, nb1, c), lambda q, j: (q, 0, 0, j, 0)),
                  pl.BlockSpec((2 * n2h, 2 * n2), lambda q, j: (0, 0)),
                  pl.BlockSpec((2, n2h, nb1, c), lambda q, j: (q, 0, j, 0)),
                  pl.BlockSpec((2, n2h, nb1, c), lambda q, j: (q, 0, j, 0)),
                  pl.BlockSpec((1, 1, c), lambda q, j: (layer, 0, 0))],
        out_specs=pl.BlockSpec((2, n2h, nb1, c), lambda q, j: (q, 0, j, 0)),
        out_shape=jax.ShapeDtypeStruct((2 * p, n2h, n1, c), BF16),
        scratch_shapes=[pltpu.VMEM((2, nb1, n2h, c), F32)],
        compiler_params=_cparams("parallel", "parallel"),
        name="dft_stage3",
    )(g5, f3, z4, x04, bias)


class _LongConv:
    def __init__(self, L, c, k2s):
        n = 2 * L
        n1 = 1 << (int(math.log2(n)) // 2)
        n2 = n // n1
        assert n1 * n2 == n and n1 == n2, "long-convolution path needs 2L to be a square power of two"
        self.L, self.c, self.n1, self.n2 = L, c, n1, n2
        self.nb1 = min(n1, 16)
        self.kb = 8
        cst = _dft_consts(n1, n2)
        self.f1 = jnp.asarray(cst["f1"], BF16)
        self.f3 = jnp.asarray(cst["f3"], BF16)
        self.m1, self.m2 = _dft_tables(cst, n1, n2)
        depth = k2s.shape[0]
        bk = _dft_stage1_filter(k2s.reshape(depth, n2, n1, c), jnp.asarray(cst["f1k"], BF16), self.nb1)
        self.kf = _dft_stage2_filter(bk.reshape(depth, 2, n, c), self.m1, n1, self.kb)

    def __call__(self, z, x0, bias, layer):
        b, L, c = z.shape
        n1, n2 = self.n1, self.n2
        z4 = z.reshape(b, n2 // 2, n1, c)
        x04 = x0.reshape(b, n2 // 2, n1, c)
        b5 = _dft_stage1(z4, self.f1, n2, self.nb1)
        gv = _dft_stage2(b5.reshape(b // 2, 2, n1 * n2, c), self.m1, self.kf, layer, self.m2, n1, self.kb)
        y = _dft_stage3(gv.reshape(b // 2, 2, n2, n1, c), self.f3, z4, x04, bias, layer, self.nb1)
        return y.reshape(b, L, c)


def _short_consts(L):
    n = 2 * L
    a = -2.0 * np.pi * np.arange(n)[:, None] * np.arange(n)[None, :] / n
    fr, fi = np.cos(a), np.sin(a)
    ff = np.block([[fr[:, :L], -fi[:, :L]], [fi[:, :L], fr[:, :L]]])
    fk = np.concatenate([fr, fi], axis=0)
    cr, ci = fr[:L] / n, -fi[:L] / n
    finv = np.block([[cr, -ci], [ci, cr]])
    return ff.astype(np.float32), fk.astype(np.float32), finv.astype(np.float32)


def _short_conv_kernel(z_ref, x0_ref, k_ref, ff_ref, fk_ref, fi_ref, bias_ref, o_ref, *, L):
    n = 2 * L
    x = jnp.concatenate([z_ref[0], z_ref[1]], axis=0)
    xf = _dot(ff_ref[...], x)
    kf = _dot(fk_ref[...], k_ref[0].astype(BF16))
    xr, xi, kr, ki = xf[:n], xf[n:], kf[:n], kf[n:]
    yin = jnp.concatenate([xr * kr - xi * ki, xr * ki + xi * kr], axis=0).astype(BF16)
    y = _dot(fi_ref[...], yin)
    for s in range(2):
        yh = y[s * L:(s + 1) * L] + z_ref[s].astype(F32) * bias_ref[0]
        o_ref[s] = (x0_ref[s].astype(F32) * yh).astype(BF16)


def _short_conv(z, x0, k2s, layer, consts, bias):
    b, L, c = z.shape
    n = 2 * L
    cb = min(c, 256)
    ff, fk, finv = consts
    kern = functools.partial(_short_conv_kernel, L=L)
    return pl.pallas_call(
        kern,
        grid=(b // 2, c // cb),
        in_specs=[pl.BlockSpec((2, L, cb), lambda p, j: (p, 0, j)),
                  pl.BlockSpec((2, L, cb), lambda p, j: (p, 0, j)),
                  pl.BlockSpec((1, n, cb), lambda p, j: (layer, 0, j)),
                  pl.BlockSpec((2 * n, 2 * L), lambda p, j: (0, 0)),
                  pl.BlockSpec((2 * n, n), lambda p, j: (0, 0)),
                  pl.BlockSpec((2 * L, 2 * n), lambda p, j: (0, 0)),
                  pl.BlockSpec((1, 1, cb), lambda p, j: (layer, 0, j))],
        out_specs=pl.BlockSpec((2, L, cb), lambda p, j: (p, 0, j)),
        out_shape=jax.ShapeDtypeStruct((b, L, c), BF16),
        compiler_params=_cparams("parallel", "parallel"),
        name="short_conv",
    )(z, x0, k2s, ff, fk, finv, bias)


def _s5_matrices(lam_re, lam_im, log_step, b_re, b_im, c_re, c_im, d):
    T = S5_CHUNK
    lr, li = lam_re.astype(F32), lam_im.astype(F32)
    dt = jnp.exp(log_step.astype(F32))[..., None]
    mag = jnp.exp(lr * dt)
    a_r, a_i = mag * jnp.cos(li * dt), mag * jnp.sin(li * dt)
    den = lr * lr + li * li
    q_r = ((a_r - 1.0) * lr + a_i * li) / den
    q_i = (a_i * lr - (a_r - 1.0) * li) / den
    br, bi = b_re.astype(F32), b_im.astype(F32)
    bb_r = q_r[..., None] * br - q_i[..., None] * bi
    bb_i = q_r[..., None] * bi + q_i[..., None] * br
    D, _, G, P, H = bb_r.shape
    pr, pi = [jnp.ones_like(a_r)], [jnp.zeros_like(a_i)]
    for _ in range(T):
        pr.append(pr[-1] * a_r - pi[-1] * a_i)
        pi.append(pr[-2] * a_i + pi[-1] * a_r)
    pw_r, pw_i = jnp.stack(pr, axis=-1), jnp.stack(pi, axis=-1)
    ct_r = jnp.swapaxes(c_re.astype(F32), -1, -2)
    ct_i = jnp.swapaxes(c_im.astype(F32), -1, -2)
    cat_r = (pw_r[..., :, None] * ct_r[..., None, :]
             - pw_i[..., :, None] * ct_i[..., None, :]).reshape(D, 2, G, P, (T + 1) * H)
    cat_i = (pw_i[..., :, None] * ct_r[..., None, :]
             + pw_r[..., :, None] * ct_i[..., None, :]).reshape(D, 2, G, P, (T + 1) * H)
    kall = (jnp.einsum('dkgph,dkgpn->dkghn', bb_r, cat_r[..., :T * H], precision=HIGHEST)
            - jnp.einsum('dkgph,dkgpn->dkghn', bb_i, cat_i[..., :T * H], precision=HIGHEST))
    kf = kall[:, 0]
    kb_rev = jnp.flip(kall[:, 1].reshape(D, G, H, T, H), axis=-2).reshape(D, G, H, T * H)
    zpad = jnp.zeros((D, G, H, (T - 1) * H), F32)
    kf_pad = jnp.concatenate([zpad, kf], axis=-1)
    kb_pad = jnp.concatenate([kb_rev, zpad], axis=-1)
    rows = [kf_pad[..., (T - 1 - j) * H:(2 * T - 1 - j) * H] + kb_pad[..., (T - 1 - j) * H:(2 * T - 1 - j) * H]
            for j in range(T)]
    m_mat = jnp.stack(rows, axis=2).reshape(D, G, T * H, T * H)
    bt_r, bt_i = jnp.swapaxes(bb_r, -1, -2), jnp.swapaxes(bb_i, -1, -2)
    pj_r = jnp.moveaxis(pw_r[..., :T], -1, -2)
    pj_i = jnp.moveaxis(pw_i[..., :T], -1, -2)

    def end_state(k, flip):
        qr, qi = pj_r[:, k], pj_i[:, k]
        if flip:
            qr, qi = jnp.flip(qr, axis=-2), jnp.flip(qi, axis=-2)
        er = qr[..., :, None, :] * bt_r[:, k][..., None, :, :] - qi[..., :, None, :] * bt_i[:, k][..., None, :, :]
        ei = qr[..., :, None, :] * bt_i[:, k][..., None, :, :] + qi[..., :, None, :] * bt_r[:, k][..., None, :, :]
        return er.reshape(D, G, T * H, P), ei.reshape(D, G, T * H, P)

    ef_r, ef_i = end_state(0, True)
    eb_r, eb_i = end_state(1, False)
    e_mat = jnp.concatenate([ef_r, eb_r, ef_i, eb_i], axis=-1)
    cf_r, cf_i = cat_r[:, 0][..., H:], -cat_i[:, 0][..., H:]
    rev = lambda v: jnp.flip(v.reshape(D, G, P, T, H), axis=-2).reshape(D, G, P, T * H)
    cb_r, cb_i = rev(cat_r[:, 1][..., H:]), rev(-cat_i[:, 1][..., H:])
    zp = jnp.zeros_like(cf_r)
    w_out = jnp.concatenate([m_mat, cf_r, zp, cf_i, zp, zp, cb_r, zp, cb_i], axis=-2)
    at_r = jnp.concatenate([pw_r[:, 0, :, :, T], pw_r[:, 1, :, :, T]], axis=-1)[:, :, None, :]
    at_i = jnp.concatenate([pw_i[:, 0, :, :, T], pw_i[:, 1, :, :, T]], axis=-1)[:, :, None, :]
    d_t = jnp.tile(d.astype(F32).reshape(D, G, 1, H), (1, 1, 1, T))
    return e_mat.astype(BF16), w_out.astype(BF16), at_r, at_i, d_t


def _s5_kernel(x_ref, e_ref, w_ref, ar_ref, ai_ref, d_ref, init_ref, y_ref, fin_ref,
               e_scr, sa_scr, sb_scr, *, gb, nb, nc, p2):
    nt = nc // 2
    lane = lax.broadcasted_iota(jnp.int32, (2 * nb, p2), 1)
    is_fwd = lane < (p2 // 2)
    first = lax.broadcasted_iota(jnp.int32, (2 * nb, p2), 0) < nb
    swap = lambda v: pltpu.roll(v, nb, axis=0)
    for g in range(gb):
        e_scr[g] = _dot(x_ref[g], e_ref[g])

    def step(k, carry):
        out = []
        rf = pl.ds(pl.multiple_of(k * 2 * nb, 2 * nb), 2 * nb)
        rb = pl.ds(pl.multiple_of((nt - 1 - k) * 2 * nb, 2 * nb), 2 * nb)
        for g in range(gb):
            cr, ci = carry[2 * g], carry[2 * g + 1]
            ar, ai = ar_ref[g], ai_ref[g]
            er = jnp.where(is_fwd, e_scr[g, rf, 0:p2], swap(e_scr[g, rb, 0:p2]))
            ei = jnp.where(is_fwd, e_scr[g, rf, p2:2 * p2], swap(e_scr[g, rb, p2:2 * p2]))
            ur = ar * cr - ai * ci + er
            ui = ar * ci + ai * cr + ei
            ur4, ui4 = swap(ur), swap(ui)
            sr = jnp.where(first, cr, ur4)
            si = jnp.where(first, ci, ui4)
            sa_scr[g, rf, 0:p2] = sr
            sa_scr[g, rf, p2:2 * p2] = si
            sb_scr[g, rb, 0:p2] = swap(sr)
            sb_scr[g, rb, p2:2 * p2] = swap(si)
            xr = jnp.where(first, ur, ur4)
            xi = jnp.where(first, ui, ui4)
            zr = ar * xr - ai * xi + er
            zi = ar * xi + ai * xr + ei
            out.append(jnp.where(first, swap(zr), zr))
            out.append(jnp.where(first, swap(zi), zi))
        return tuple(out)

    init = []
    for g in range(gb):
        init += [init_ref[g, :, 0:p2], init_ref[g, :, p2:2 * p2]]
    fin = lax.fori_loop(0, nt, step, tuple(init))
    for g in range(gb):
        fin_ref[g] = jnp.concatenate([fin[2 * g], fin[2 * g + 1]], axis=1)
        x = x_ref[g]
        lhs = jnp.concatenate([x, sa_scr[g].astype(BF16), sb_scr[g].astype(BF16)], axis=1)
        y = _dot(lhs, w_ref[g]) + x.astype(F32) * d_ref[g]
        y_ref[g] = jax.nn.gelu(y).astype(BF16)


def _s5_scan(xg, mats, layer, init, nb):
    e_mat, w_out, at_r, at_i, d_t = mats
    G, R, th = xg.shape
    p4 = e_mat.shape[-1]
    p2 = p4 // 2
    gb = S5_GROUPS_PER_STEP if G % S5_GROUPS_PER_STEP == 0 else 2
    nc = R // nb
    assert 2 * nb == 8 and nc % 2 == 0, "two chunks of batch rows must fill one 8-sublane tile"
    kern = functools.partial(_s5_kernel, gb=gb, nb=nb, nc=nc, p2=p2)
    g3 = lambda i: (i, 0, 0)
    l4 = lambda i: (layer, i, 0, 0)
    return pl.pallas_call(
        kern,
        grid=(G // gb,),
        in_specs=[pl.BlockSpec((gb, R, th), g3),
                  pl.BlockSpec((None, gb, th, p4), l4),
                  pl.BlockSpec((None, gb, th + 2 * p4, th), l4),
                  pl.BlockSpec((None, gb, 1, p2), l4),
                  pl.BlockSpec((None, gb, 1, p2), l4),
                  pl.BlockSpec((None, gb, 1, th), l4),
                  pl.BlockSpec((gb, 2 * nb, p4), g3)],
        out_specs=[pl.BlockSpec((gb, R, th), g3),
                   pl.BlockSpec((gb, 2 * nb, p4), g3)],
        out_shape=[jax.ShapeDtypeStruct((G, R, th), BF16),
                   jax.ShapeDtypeStruct((G, 2 * nb, p4), F32)],
        scratch_shapes=[pltpu.VMEM((gb, R, p4), F32),
                        pltpu.VMEM((gb, R, p4), F32),
                        pltpu.VMEM((gb, R, p4), F32)],
        compiler_params=_cparams("parallel"),
        name="s5_scan",
    )(xg, e_mat, w_out, at_r, at_i, d_t, init)


def _lane_group(rows, h):
    return lax.broadcasted_iota(jnp.int32, (rows, LANES), 1) // h


def _block_transpose(vs, h):
    n = len(vs)
    assert n * h == LANES and n & (n - 1) == 0
    blk = _lane_group(vs[0].shape[0], h)
    s = 1
    while s < n:
        upper = (blk & s) != 0
        out = list(vs)
        for i in range(n):
            if i & s == 0:
                a, b = vs[i], vs[i | s]
                out[i] = jnp.where(upper, pltpu.roll(b, s * h, axis=1), a)
                out[i | s] = jnp.where(upper, b, pltpu.roll(a, LANES - s * h, axis=1))
        vs = out
        s *= 2
    return vs


def _s5_pack_kernel(u_ref, o_ref, t_scr, *, h, rows):
    gl = LANES // h
    t_scr[...] = _swap_major(u_ref[...].astype(F32))
    pr = min(rows, PACK_ROWS)
    for rb in range(rows // pr):
        rs = slice(rb * pr, (rb + 1) * pr)
        for half in range(S5_CHUNK // gl):
            xs = _block_transpose([t_scr[half * gl + jj, rs, :] for jj in range(gl)], h)
            for g in range(gl):
                o_ref[g, rs, half * LANES:(half + 1) * LANES] = xs[g].astype(BF16)


def _s5_unpack_kernel(y_ref, o_ref, t_scr, *, h, rows):
    gl = LANES // h
    pr = min(rows, PACK_ROWS)
    for rb in range(rows // pr):
        rs = slice(rb * pr, (rb + 1) * pr)
        for half in range(S5_CHUNK // gl):
            ys = _block_transpose([y_ref[g, rs, half * LANES:(half + 1) * LANES].astype(F32) for g in range(gl)], h)
            for tt in range(gl):
                t_scr[half * gl + tt, rs, :] = ys[tt]
    o_ref[...] = _swap_major(t_scr[...]).astype(BF16)


def _s5_pack(u3, h):
    R, t, w = u3.shape
    gl = LANES // h
    rows = min(R, 256)
    kern = functools.partial(_s5_pack_kernel, h=h, rows=rows)
    return pl.pallas_call(
        kern,
        grid=(w // LANES, R // rows),
        in_specs=[pl.BlockSpec((rows, t, LANES), lambda l, i: (i, 0, l))],
        out_specs=pl.BlockSpec((gl, rows, t * h), lambda l, i: (l, i, 0)),
        out_shape=jax.ShapeDtypeStruct((w // h, R, t * h), BF16),
        scratch_shapes=[pltpu.VMEM((t, rows, LANES), F32)],
        compiler_params=_cparams("parallel", "parallel"),
        name="s5_pack",
    )(u3)


def _s5_unpack(yg, h):
    G, R, th = yg.shape
    t = th // h
    gl = LANES // h
    rows = min(R, 256)
    kern = functools.partial(_s5_unpack_kernel, h=h, rows=rows)
    return pl.pallas_call(
        kern,
        grid=(G // gl, R // rows),
        in_specs=[pl.BlockSpec((gl, rows, th), lambda l, i: (l, i, 0))],
        out_specs=pl.BlockSpec((rows, t, LANES), lambda l, i: (i, 0, l)),
        out_shape=jax.ShapeDtypeStruct((R, t, G * h), BF16),
        scratch_shapes=[pltpu.VMEM((t, rows, LANES), F32)],
        compiler_params=_cparams("parallel", "parallel"),
        name="s5_unpack",
    )(yg)


def _s5_mixer(u4, mats, layer, init, h):
    lc, nb, t, w = u4.shape
    yg, fin = _s5_scan(_s5_pack(u4.reshape(lc * nb, t, w), h), mats, layer, init, nb)
    return _s5_unpack(yg, h).reshape(lc, nb, t, w), fin


def _outproj_kernel(yh_ref, ys_ref, x_ref, mod_ref, g_ref, wg_ref, bg_ref, wo_ref,
                    xo_ref, hx_ref, *, d, dh, tm):
    gate = mod_ref[0, :, 2 * d:3 * d]
    shift = mod_ref[0, :, 3 * d:4 * d]
    scale = mod_ref[0, :, 4 * d:5 * d]
    ns = max(1, min(SUB_TILES, tm // OUT_SUB_ROWS))
    sub = tm // ns
    rows = [slice(s * sub, (s + 1) * sub) for s in range(ns)]
    ys = [ys_ref[s * sub // S5_CHUNK:(s + 1) * sub // S5_CHUNK, 0, :, :].reshape(sub, ys_ref.shape[-1])
          for s in range(ns)]
    pre = [_dot(y, wg_ref[0]) for y in ys]
    glu = [(y.astype(F32) * jax.nn.sigmoid(a + bg_ref[0])).astype(BF16) for y, a in zip(ys, pre)]
    yx = [_dot(yh_ref[0, rs, :], wo_ref[0, 0:dh, :]) + _dot(gl, wo_ref[0, dh:, :]) for rs, gl in zip(rows, glu)]
    for rs, v in zip(rows, yx):
        xo = x_ref[0, rs, :] + gate * _rms(v, g_ref[0, 1:2, :])
        xo_ref[0, rs, :] = xo
        hx_ref[0, rs, :] = (_rms(xo, g_ref[0, 2:3, :]) * (1.0 + scale) + shift).astype(BF16)


def _outproj(yh, ys4, x, mod, mod_row, layer, norm_g, w_glu, b_glu, w_out, tm):
    b, s, d = x.shape
    dh = yh.shape[-1]
    ds5 = ys4.shape[-1]
    tc = tm // S5_CHUNK
    kern = functools.partial(_outproj_kernel, d=d, dh=dh, tm=tm)
    lyr = lambda bi, i: (layer, 0, 0)
    t3 = lambda bi, i: (bi, i, 0)
    return pl.pallas_call(
        kern,
        grid=(b, s // tm),
        in_specs=[pl.BlockSpec((1, tm, dh), t3),
                  pl.BlockSpec((tc, 1, S5_CHUNK, ds5), lambda bi, i: (i, bi, 0, 0)),
                  pl.BlockSpec((1, tm, d), t3),
                  pl.BlockSpec((1, 1, mod.shape[-1]), lambda bi, i: (mod_row(bi), 0, 0)),
                  pl.BlockSpec((1,) + norm_g.shape[1:], lyr),
                  pl.BlockSpec((1, ds5, ds5), lyr), pl.BlockSpec((1, 1, ds5), lyr),
                  pl.BlockSpec((1, dh + ds5, d), lyr)],
        out_specs=[pl.BlockSpec((1, tm, d), t3), pl.BlockSpec((1, tm, d), t3)],
        out_shape=[jax.ShapeDtypeStruct((b, s, d), F32), jax.ShapeDtypeStruct((b, s, d), BF16)],
        compiler_params=_cparams("parallel", "parallel"),
        name="outproj",
    )(yh, ys4, x, mod, norm_g, w_glu, b_glu, w_out)


def _ffn_kernel(hm_ref, hp_ref, hn_ref, x_ref, mod_ref, g_ref, wg_ref, wv_ref, cw_ref, cb_ref, wd_ref,
                o_ref, acc_ref, *, tm, d, wg, vertical, kp):
    k = pl.program_id(3)
    i = pl.program_id(1) * kp + k
    nt = pl.num_programs(1) * kp
    j = pl.program_id(2)
    nj = pl.num_programs(2)
    hm = hm_ref[0]
    cw = cw_ref[0]
    if vertical:
        top = jnp.where(i > 0, 1.0, 0.0).astype(BF16)
        bot = jnp.where(i < nt - 1, 1.0, 0.0).astype(BF16)
        ha = jnp.concatenate([hp_ref[0] * top, hm, hn_ref[0] * bot], axis=0)
    else:
        ha = hm
    fc = wg_ref.shape[-1]
    c1 = -(-(fc // 2) // FFN_COL_ALIGN) * FFN_COL_ALIGN
    segs = [(0, c1), (c1, fc)] if 0 < c1 < fc else [(0, fc)]
    gs = [_dot(ha, wg_ref[0, :, a:b]) for a, b in segs]
    vs = [_dot(hm, wv_ref[0, :, a:b]) for a, b in segs]

    def vcol(g, w, dx):
        if not vertical:
            return g * w[3 + dx:4 + dx]
        return (g[0:tm] * w[dx:dx + 1] + g[wg:wg + tm] * w[3 + dx:4 + dx]
                + g[2 * wg:2 * wg + tm] * w[6 + dx:7 + dx])

    hmid = []
    for (a, b), g, v in zip(segs, gs, vs):
        w = cw[:, a:b]
        col = lax.broadcasted_iota(jnp.int32, (tm, b - a), 0) & (wg - 1)
        conv = (vcol(g, w, 1) + jnp.where(col > 0, pltpu.roll(vcol(g, w, 0), 1, axis=0), 0.0)
                + jnp.where(col < wg - 1, pltpu.roll(vcol(g, w, 2), tm - 1, axis=0), 0.0) + cb_ref[0, :, a:b])
        hmid.append((jax.nn.gelu(conv) * v).astype(BF16))
    part = _dot(hmid[0], wd_ref[0, segs[0][0]:segs[0][1], :])
    for (a, b), h in zip(segs[1:], hmid[1:]):
        part = part + _dot(h, wd_ref[0, a:b, :])

    @pl.when(j == 0)
    def _():
        acc_ref[k] = part

    @pl.when(j > 0)
    def _():
        acc_ref[k] += part

    @pl.when(j == nj - 1)
    def _():
        gate = mod_ref[0, :, 5 * d:6 * d]
        o_ref[0] = x_ref[0] + gate * _rms(acc_ref[k], g_ref[0, 3:4, :])


def _ffn(hx, x, mod, mod_row, layer, norm_g, w_up, conv_w, conv_b, w_down, tm, wg, vertical):
    b, s, d = x.shape
    f = w_down.shape[1]
    fc = f // 2 if (f // 2) % LANES == 0 else f
    nf = f // fc
    nt = s // tm
    r = tm // wg if vertical else 1
    hb = wg if vertical else 16
    nhb = s // hb
    kp = FFN_TILE_GROUP if nt % FFN_TILE_GROUP == 0 else 1
    kern = functools.partial(_ffn_kernel, tm=tm, d=d, wg=wg, vertical=vertical, kp=kp)
    tile = lambda ip, k: ip * kp + k
    cur = lambda bi, ip, j, k: (bi, tile(ip, k), 0)
    fin = lambda bi, ip, j, k: (bi, tile(ip, jnp.where(j == nf - 1, k, 0)), 0)
    return pl.pallas_call(
        kern,
        grid=(b, nt // kp, nf, kp),
        in_specs=[pl.BlockSpec((1, tm, d), cur),
                  pl.BlockSpec((1, hb, d), lambda bi, ip, j, k: (bi, jnp.maximum(tile(ip, k) * r - 1, 0), 0)),
                  pl.BlockSpec((1, hb, d), lambda bi, ip, j, k: (bi, jnp.minimum((tile(ip, k) + 1) * r, nhb - 1), 0)),
                  pl.BlockSpec((1, tm, d), fin),
                  pl.BlockSpec((1, 1, mod.shape[-1]), lambda bi, ip, j, k: (mod_row(bi), 0, 0)),
                  pl.BlockSpec((1,) + norm_g.shape[1:], lambda bi, ip, j, k: (layer, 0, 0)),
                  pl.BlockSpec((1, d, fc), lambda bi, ip, j, k: (layer, 0, j)),
                  pl.BlockSpec((1, d, fc), lambda bi, ip, j, k: (layer, 0, nf + j)),
                  pl.BlockSpec((1, 9, fc), lambda bi, ip, j, k: (layer, 0, j)),
                  pl.BlockSpec((1, 1, fc), lambda bi, ip, j, k: (layer, 0, j)),
                  pl.BlockSpec((1, fc, d), lambda bi, ip, j, k: (layer, j, 0))],
        out_specs=pl.BlockSpec((1, tm, d), fin),
        out_shape=jax.ShapeDtypeStruct((b, s, d), F32),
        scratch_shapes=[pltpu.VMEM((kp, tm, d), F32)],
        compiler_params=_cparams("parallel", "parallel", "arbitrary", "arbitrary"),
        name="conv_glu_ffn",
    )(hx, hx, hx, x, mod, norm_g, w_up, w_up, conv_w, conv_b, w_down)


def kernel(x, c, ctx, c_ctx, w_ada, b_ada, norm_g, w_in, hy_short_w, hy_short_b,
           filt_w_in, filt_b_in, filt_w_hid, filt_b_hid, filt_freq, filt_w_out, hy_bias,
           s5_lam_re, s5_lam_im, s5_log_step, s5_b_re, s5_b_im, s5_c_re, s5_c_im, s5_d,
           s5_w_glu, s5_b_glu, w_out, ffn_w_up, ffn_conv_w, ffn_conv_b, ffn_w_down):
    depth = w_ada.shape[0]
    bsz, seq, d = x.shape
    lctx = ctx.shape[1]
    dh = hy_bias.shape[-1]
    G, P, H = s5_b_re.shape[2], s5_b_re.shape[3], s5_b_re.shape[4]
    dff = ffn_w_down.shape[1]
    assert bsz % 2 == 0 and bsz <= 4 and seq % GRID_W == 0 and GRID_W & (GRID_W - 1) == 0

    cond = jnp.zeros((8, d), F32).at[:bsz].set(c).at[bsz].set(c_ctx)
    mod = _ada_mod(cond, w_ada, b_ada).reshape(depth * 8, 1, 6 * d)

    filt_args = (filt_w_in, filt_b_in, filt_w_hid, filt_b_hid, filt_freq, filt_w_out)
    long_conv = _LongConv(seq, dh, _hyena_filters(seq, *filt_args, dh))
    k_ctx = _hyena_filters(lctx, *filt_args, dh)
    short_c = tuple(jnp.asarray(a, BF16) for a in _short_consts(lctx))

    w_in_b = w_in.astype(BF16)
    w_glu_b = s5_w_glu.astype(BF16)
    w_out_b = w_out.astype(BF16)
    w_up_b = ffn_w_up.astype(BF16)
    w_down_b = ffn_w_down.astype(BF16)
    sb = hy_short_b.reshape(depth, 1, 3 * dh)
    cw = ffn_conv_w.reshape(depth, 9, dff)
    cb = ffn_conv_b.reshape(depth, 1, dff)
    bg = s5_b_glu.reshape(depth, 1, -1)
    hb = hy_bias.reshape(depth, 1, dh)
    mats = _s5_matrices(s5_lam_re, s5_lam_im, s5_log_step, s5_b_re, s5_b_im, s5_c_re, s5_c_im, s5_d)

    tm = min(seq, ROW_TILE)
    tp = min(seq, PROJ_ROW_TILE)
    for l in range(depth):
        last = l == depth - 1
        row_x = lambda bi, l=l: 8 * l + bi
        row_c = lambda bi, l=l: 8 * l + bsz

        x0c, zc, uc = _inproj(ctx, mod, row_c, l, norm_g, w_in_b, hy_short_w, sb, dh, lctx)
        ysc, ctx_state = _s5_mixer(uc, mats, l, jnp.zeros((G, 2 * bsz, 4 * P), F32), H)

        x0, z, u = _inproj(x, mod, row_x, l, norm_g, w_in_b, hy_short_w, sb, dh, tp)
        ys, _ = _s5_mixer(u, mats, l, ctx_state, H)
        yh = long_conv(z, x0, hb, l)
        x, hx = _outproj(yh, ys, x, mod, row_x, l, norm_g, w_glu_b, bg, w_out_b, tp)
        x = _ffn(hx, x, mod, row_x, l, norm_g, w_up_b, cw, cb, w_down_b, tm, GRID_W, True)

        if not last:
            yhc = _short_conv(zc, x0c, k_ctx, l, short_c, hb)
            ctx, hc = _outproj(yhc, ysc, ctx, mod, row_c, l, norm_g, w_glu_b, bg, w_out_b, lctx)
            ctx = _ffn(hc, ctx, mod, row_c, l, norm_g, w_up_b, cw, cb, w_down_b, lctx, lctx, False)
    return x
```

```python
import functools
import math

import numpy as np
import jax
import jax.numpy as jnp
from jax import lax
from jax.experimental import pallas as pl
from jax.experimental.pallas import tpu as pltpu

GRID_W = 64
RMS_EPS = 1e-6
DECAY_TARGET = 1e-2
FAST_DECAY_PCT = 0.3
SLOW_DECAY_PCT = 1.5
S5_CHUNK = 16
S5_GROUPS_PER_STEP = 4
LANES = 128
HALO = 16
ROW_TILE = 512
PROJ_ROW_TILE = 1024
SUB_TILES = 8
OUT_SUB_ROWS = 128
IN_SUB_TILES = 2
FFN_COL_ALIGN = 256
FFN_TILE_GROUP = 2
PACK_ROWS = 128
VMEM_LIMIT = 56 * 1024 * 1024

F32 = jnp.float32
BF16 = jnp.bfloat16
HIGHEST = lax.Precision.HIGHEST


def _cparams(*sem):
    return pltpu.CompilerParams(dimension_semantics=sem, vmem_limit_bytes=VMEM_LIMIT)


def _dot(a, b, **kw):
    return jnp.dot(a, b, preferred_element_type=F32, **kw)


def _ada_kernel(cond_ref, w_ref, b_ref, o_ref):
    cv = cond_ref[...]
    s = cv * jax.nn.sigmoid(cv)
    o_ref[0] = _dot(s, w_ref[0], precision=HIGHEST) + b_ref[0]


def _ada_mod(cond, w_ada, b_ada):
    depth, d, n = w_ada.shape
    tn = n // 4
    return pl.pallas_call(
        _ada_kernel,
        grid=(depth, n // tn),
        in_specs=[pl.BlockSpec((8, d), lambda l, j: (0, 0)),
                  pl.BlockSpec((1, d, tn), lambda l, j: (l, 0, j)),
                  pl.BlockSpec((1, 1, tn), lambda l, j: (l, 0, j))],
        out_specs=pl.BlockSpec((1, 8, tn), lambda l, j: (l, 0, j)),
        out_shape=jax.ShapeDtypeStruct((depth, 8, n), F32),
        compiler_params=_cparams("parallel", "parallel"),
        name="ada_mod",
    )(cond, w_ada, b_ada.reshape(depth, 1, n))


def _rms(v, g):
    ms = jnp.mean(v * v, axis=-1, keepdims=True)
    return v * lax.rsqrt(ms + RMS_EPS) * g


def _inproj_kernel(xm_ref, xp_ref, xn_ref, mod_ref, g_ref, w_ref, sw_ref, sb_ref,
                   x0_ref, z_ref, u_ref, p_scr, *, tm, d, dh):
    i = pl.program_id(1)
    nt = pl.num_programs(1)
    shift = mod_ref[0, :, 0:d]
    scale = mod_ref[0, :, d:2 * d]
    xa = jnp.concatenate([xp_ref[0], xm_ref[0], xn_ref[0]], axis=0)
    sw = sw_ref[0]
    ns = IN_SUB_TILES
    sub = tm // ns
    cut = [0] + [HALO + s * sub for s in range(1, ns)] + [tm + 2 * HALO]

    def project(s):
        lo, hi = cut[s], cut[s + 1]
        xn = (_rms(xa[lo:hi], g_ref[0, 0:1, :]) * (1.0 + scale) + shift).astype(BF16)
        p = _dot(xn, w_ref[0])
        p_scr[lo:hi, :] = p[:, :3 * dh]
        if s == 0:
            inside = jnp.where(i > 0, 1.0, 0.0).astype(F32)
            p_scr[HALO - 8:HALO, :] = p[HALO - 8:HALO, :3 * dh] * inside
        if s == ns - 1:
            inside = jnp.where(i < nt - 1, 1.0, 0.0).astype(F32)
            p_scr[HALO + tm:HALO + tm + 8, :] = p[HALO + tm - lo:HALO + tm + 8 - lo, :3 * dh] * inside
        a, b = max(lo, HALO), min(hi, HALO + tm)
        u = p[a - lo:b - lo, 3 * dh:].astype(BF16)
        u_ref[(a - HALO) // S5_CHUNK:(b - HALO) // S5_CHUNK, 0, :, :] = u.reshape(
            (b - a) // S5_CHUNK, S5_CHUNK, u.shape[-1])

    def conv_gate(s):
        r0 = HALO + s * sub
        conv = (p_scr[pl.ds(r0 - 1, sub), :] * sw[0:1] + p_scr[pl.ds(r0, sub), :] * sw[1:2]
                + p_scr[pl.ds(r0 + 1, sub), :] * sw[2:3] + sb_ref[0])
        rs = slice(s * sub, (s + 1) * sub)
        x0_ref[0, rs, :] = conv[:, :dh].astype(BF16)
        z_ref[0, rs, :] = (conv[:, dh:2 * dh] * conv[:, 2 * dh:]).astype(BF16)

    project(0)
    for s in range(ns):
        if s + 1 < ns:
            project(s + 1)
        conv_gate(s)


def _inproj(x, mod, mod_row, layer, norm_g, w_in, sw, sb, dh, tm):
    b, s, d = x.shape
    dp = w_in.shape[-1]
    ds5 = dp - 3 * dh
    nt = s // tm
    r = tm // HALO
    nh = s // HALO
    tc = tm // S5_CHUNK
    kern = functools.partial(_inproj_kernel, tm=tm, d=d, dh=dh)
    lyr = lambda bi, i: (layer, 0, 0)
    return pl.pallas_call(
        kern,
        grid=(b, nt),
        in_specs=[pl.BlockSpec((1, tm, d), lambda bi, i: (bi, i, 0)),
                  pl.BlockSpec((1, HALO, d), lambda bi, i: (bi, jnp.maximum(i * r - 1, 0), 0)),
                  pl.BlockSpec((1, HALO, d), lambda bi, i: (bi, jnp.minimum((i + 1) * r, nh - 1), 0)),
                  pl.BlockSpec((1, 1, mod.shape[-1]), lambda bi, i: (mod_row(bi), 0, 0)),
                  pl.BlockSpec((1,) + norm_g.shape[1:], lyr),
                  pl.BlockSpec((1, d, dp), lyr),
                  pl.BlockSpec((1, 3, 3 * dh), lyr),
                  pl.BlockSpec((1, 1, 3 * dh), lyr)],
        out_specs=[pl.BlockSpec((1, tm, dh), lambda bi, i: (bi, i, 0)),
                   pl.BlockSpec((1, tm, dh), lambda bi, i: (bi, i, 0)),
                   pl.BlockSpec((tc, 1, S5_CHUNK, ds5), lambda bi, i: (i, bi, 0, 0))],
        out_shape=[jax.ShapeDtypeStruct((b, s, dh), BF16),
                   jax.ShapeDtypeStruct((b, s, dh), BF16),
                   jax.ShapeDtypeStruct((s // S5_CHUNK, b, S5_CHUNK, ds5), BF16)],
        scratch_shapes=[pltpu.VMEM((tm + 2 * HALO, 3 * dh), F32)],
        compiler_params=_cparams("parallel", "arbitrary"),
        name="inproj",
    )(x, x, x, mod, norm_g, w_in, sw, sb)


def _filter_feats(L, emb):
    bands = (emb - 1) // 2
    t = np.linspace(0.0, 1.0, L, dtype=np.float32).astype(np.float64)[:, None]
    w = (2.0 * math.pi / L) * np.arange(L, dtype=np.float64)[:, None]
    f = np.linspace(1e-4, bands - 1, bands, dtype=np.float32).astype(np.float64)[None, :]
    z = np.concatenate([t, np.cos(f * w), -np.sin(f * w)], axis=-1)
    zp = np.zeros((L, LANES), np.float32)
    zp[:, :emb] = z
    return zp


def _filter_kernel(z_ref, win_ref, bin_ref, whid_ref, bhid_ref, fr_ref, wf_ref, wb_ref, dl_ref, o_ref):
    z = z_ref[...]
    fr = fr_ref[0]
    h = jnp.sin(fr * (_dot(z, win_ref[0], precision=HIGHEST) + bin_ref[0]))
    for i in range(whid_ref.shape[1]):
        h = jnp.sin(fr * (_dot(h, whid_ref[0, i], precision=HIGHEST) + bhid_ref[0, i]))
    o_ref[0, 0] = _dot(h, wf_ref[0], precision=HIGHEST) * jnp.exp(-z[:, 0:1] * dl_ref[...])
    hb = _dot(h, wb_ref[0], precision=HIGHEST) * jnp.exp(-z[:, LANES:LANES + 1] * dl_ref[...])
    first = (pl.program_id(1) == 0) & (lax.broadcasted_iota(jnp.int32, (z.shape[0], 1), 0) == 0)
    o_ref[0, 1] = jnp.where(first, 0.0, hb)


def _block_diag2(w):
    zero = jnp.zeros_like(w)
    return jnp.concatenate([jnp.concatenate([w, zero], axis=-1), jnp.concatenate([zero, w], axis=-1)], axis=-2)


def _hyena_filters(L, f_w_in, f_b_in, f_w_hid, f_b_hid, f_freq, f_w_out, dh):
    depth, emb, hid = f_w_in.shape
    n_inner = f_w_hid.shape[1]
    tl = min(L, 1024)
    z1 = _filter_feats(L, emb)
    z = jnp.asarray(np.concatenate([z1, np.concatenate([z1[:1], z1[:0:-1]], axis=0)], axis=1))
    win = _block_diag2(jnp.zeros((depth, LANES, hid), F32).at[:, :emb].set(f_w_in))
    zero = jnp.zeros((depth, hid, dh), F32)
    wf = jnp.concatenate([f_w_out[:, :, :dh], zero], axis=1)
    wb = jnp.concatenate([zero, f_w_out[:, :, dh:]], axis=1)
    twice = lambda v, shp: jnp.tile(v.reshape(shp), (1,) * (len(shp) - 1) + (2,))
    deltas = np.abs(np.linspace(math.log(DECAY_TARGET) / FAST_DECAY_PCT,
                                math.log(DECAY_TARGET) / SLOW_DECAY_PCT, dh, dtype=np.float32))[None, :]
    h2 = 2 * hid
    out = pl.pallas_call(
        _filter_kernel,
        grid=(depth, L // tl),
        in_specs=[pl.BlockSpec((tl, 2 * LANES), lambda l, i: (i, 0)),
                  pl.BlockSpec((1, 2 * LANES, h2), lambda l, i: (l, 0, 0)),
                  pl.BlockSpec((1, 1, h2), lambda l, i: (l, 0, 0)),
                  pl.BlockSpec((1, n_inner, h2, h2), lambda l, i: (l, 0, 0, 0)),
                  pl.BlockSpec((1, n_inner, 1, h2), lambda l, i: (l, 0, 0, 0)),
                  pl.BlockSpec((1, 1, h2), lambda l, i: (l, 0, 0)),
                  pl.BlockSpec((1, h2, dh), lambda l, i: (l, 0, 0)),
                  pl.BlockSpec((1, h2, dh), lambda l, i: (l, 0, 0)),
                  pl.BlockSpec((1, dh), lambda l, i: (0, 0))],
        out_specs=pl.BlockSpec((1, 2, tl, dh), lambda l, i: (l, 0, i, 0)),
        out_shape=jax.ShapeDtypeStruct((depth, 2, L, dh), F32),
        compiler_params=_cparams("parallel", "parallel"),
        name="hyena_filter",
    )(z, win, twice(f_b_in, (depth, 1, hid)), _block_diag2(f_w_hid), twice(f_b_hid, (depth, n_inner, 1, hid)),
      twice(f_freq, (depth, 1, hid)), wf, wb, jnp.asarray(deltas))
    return out.reshape(depth, 2 * L, dh)


def _dft_consts(n1, n2):
    n = n1 * n2
    n2h = n2 // 2
    k2 = np.arange(n2)[:, None]
    a = -2.0 * np.pi * k2 * np.arange(n2)[None, :] / n2
    fr, fi = np.cos(a), np.sin(a)
    f1 = np.block([[fr[:, :n2h], -fi[:, :n2h]], [fi[:, :n2h], fr[:, :n2h]]])
    f1k = np.concatenate([fr, fi], axis=0)
    cr, ci = fr[:n2h] / n, -fi[:n2h] / n
    f3 = np.block([[cr, -ci], [ci, cr]])
    b = -2.0 * np.pi * np.arange(n1)[:, None] * np.arange(n1)[None, :] / n1
    t = -2.0 * np.pi * np.arange(n2)[:, None] * np.arange(n1)[None, :] / n
    return dict(f1=f1.astype(np.float32), f1k=f1k.astype(np.float32), f3=f3.astype(np.float32),
                gr=np.cos(b).astype(np.float32), gi=np.sin(b).astype(np.float32),
                tr=np.cos(t).astype(np.float32)[:, None, :], ti=np.sin(t).astype(np.float32)[:, None, :])


def _tables_kernel(gr_ref, gi_ref, tr_ref, ti_ref, m1_ref, m2_ref, *, kb):
    gr, gi = gr_ref[...], gi_ref[...]
    for q in range(kb):
        tr, ti = tr_ref[q], ti_ref[q]
        re = gr * tr - gi * ti
        im = gr * ti + gi * tr
        m1_ref[q] = jnp.concatenate([jnp.concatenate([re, -im], axis=1),
                                     jnp.concatenate([im, re], axis=1)], axis=0).astype(BF16)
        ret, imt = re.T, im.T
        m2_ref[q] = jnp.concatenate([jnp.concatenate([ret, imt], axis=1),
                                     jnp.concatenate([-imt, ret], axis=1)], axis=0).astype(BF16)


def _dft_tables(c, n1, n2):
    kb = 8
    kern = functools.partial(_tables_kernel, kb=kb)
    shp = jax.ShapeDtypeStruct((n2, 2 * n1, 2 * n1), BF16)
    return pl.pallas_call(
        kern,
        grid=(n2 // kb,),
        in_specs=[pl.BlockSpec((n1, n1), lambda i: (0, 0)),
                  pl.BlockSpec((n1, n1), lambda i: (0, 0)),
                  pl.BlockSpec((kb, 1, n1), lambda i: (i, 0, 0)),
                  pl.BlockSpec((kb, 1, n1), lambda i: (i, 0, 0))],
        out_specs=[pl.BlockSpec((kb, 2 * n1, 2 * n1), lambda i: (i, 0, 0)),
                   pl.BlockSpec((kb, 2 * n1, 2 * n1), lambda i: (i, 0, 0))],
        out_shape=[shp, shp],
        compiler_params=_cparams("parallel"),
        name="dft_tables",
    )(jnp.asarray(c["gr"]), jnp.asarray(c["gi"]), jnp.asarray(c["tr"]), jnp.asarray(c["ti"]))


def _swap_major(v):
    return pltpu.einshape("abc->bac", v)


def _s1_kernel(z_ref, f_ref, o_ref, y_scr, *, n2, nb1):
    zt = [_swap_major(z_ref[s]) for s in range(2)]
    for j in range(nb1):
        y = _dot(f_ref[...], jnp.concatenate([zt[0][j], zt[1][j]], axis=0))
        y_scr[0, j] = y[:n2].astype(BF16)
        y_scr[1, j] = y[n2:].astype(BF16)
    for s in range(2):
        o_ref[0, s] = _swap_major(y_scr[s])


def _s1k_kernel(k_ref, f_ref, o_ref, y_scr, *, n2, nb1):
    kt = _swap_major(k_ref[0].astype(BF16))
    for j in range(nb1):
        y = _dot(f_ref[...], kt[j])
        y_scr[0, j] = y[:n2].astype(BF16)
        y_scr[1, j] = y[n2:].astype(BF16)
    for s in range(2):
        o_ref[0, s] = _swap_major(y_scr[s])


def _dft_stage1(z4, f1, n2, nb1):
    b, n2h, n1, c = z4.shape
    kern = functools.partial(_s1_kernel, n2=n2, nb1=nb1)
    return pl.pallas_call(
        kern,
        grid=(b // 2, n1 // nb1),
        in_specs=[pl.BlockSpec((2, n2h, nb1, c), lambda p, j: (p, 0, j, 0)),
                  pl.BlockSpec((2 * n2, 2 * n2h), lambda p, j: (0, 0))],
        out_specs=pl.BlockSpec((1, 2, n2, nb1, c), lambda p, j: (p, 0, 0, j, 0)),
        out_shape=jax.ShapeDtypeStruct((b // 2, 2, n2, n1, c), BF16),
        scratch_shapes=[pltpu.VMEM((2, nb1, n2, c), BF16)],
        compiler_params=_cparams("parallel", "parallel"),
        name="dft_stage1",
    )(z4, f1)


def _dft_stage1_filter(k4, f1k, nb1):
    depth, n2, n1, c = k4.shape
    kern = functools.partial(_s1k_kernel, n2=n2, nb1=nb1)
    return pl.pallas_call(
        kern,
        grid=(depth, n1 // nb1),
        in_specs=[pl.BlockSpec((1, n2, nb1, c), lambda l, j: (l, 0, j, 0)),
                  pl.BlockSpec((2 * n2, n2), lambda l, j: (0, 0))],
        out_specs=pl.BlockSpec((1, 2, n2, nb1, c), lambda l, j: (l, 0, 0, j, 0)),
        out_shape=jax.ShapeDtypeStruct((depth, 2, n2, n1, c), BF16),
        scratch_shapes=[pltpu.VMEM((2, nb1, n2, c), BF16)],
        compiler_params=_cparams("parallel", "parallel"),
        name="dft_stage1_filter",
    )(k4, f1k)


def _s2k_kernel(b_ref, m1_ref, o_ref, *, kb, n1):
    for q in range(kb):
        rows = slice(q * n1, (q + 1) * n1)
        xin = jnp.concatenate([b_ref[0, 0, rows, :], b_ref[0, 1, rows, :]], axis=0)
        xf = _dot(m1_ref[q], xin)
        o_ref[0, 0, rows, :] = xf[:n1]
        o_ref[0, 1, rows, :] = xf[n1:]


def _dft_stage2_filter(bv, m1, n1, kb):
    depth, _, n, c = bv.shape
    kern = functools.partial(_s2k_kernel, kb=kb, n1=n1)
    return pl.pallas_call(
        kern,
        grid=(n // (kb * n1), depth),
        in_specs=[pl.BlockSpec((1, 2, kb * n1, c), lambda k, l: (l, 0, k, 0)),
                  pl.BlockSpec((kb, 2 * n1, 2 * n1), lambda k, l: (k, 0, 0))],
        out_specs=pl.BlockSpec((1, 2, kb * n1, c), lambda k, l: (l, 0, k, 0)),
        out_shape=jax.ShapeDtypeStruct((depth, 2, n, c), F32),
        compiler_params=_cparams("parallel", "parallel"),
        name="dft_stage2_filter",
    )(bv, m1)


def _s2_kernel(b_ref, m1_ref, kf_ref, m2_ref, o_ref, *, kb, n1):
    for q in range(kb):
        rows = slice(q * n1, (q + 1) * n1)
        xin = jnp.concatenate([b_ref[0, 0, rows, :], b_ref[0, 1, rows, :]], axis=0)
        xf = _dot(m1_ref[q], xin)
        xr, xi = xf[:n1], xf[n1:]
        kr, ki = kf_ref[0, 0, rows, :], kf_ref[0, 1, rows, :]
        yin = jnp.concatenate([xr * kr - xi * ki, xr * ki + xi * kr], axis=0).astype(BF16)
        g = _dot(m2_ref[q], yin)
        o_ref[0, 0, rows, :] = g[:n1].astype(BF16)
        o_ref[0, 1, rows, :] = g[n1:].astype(BF16)


def _dft_stage2(bv, m1, kf, layer, m2, n1, kb):
    p, _, n, c = bv.shape
    kern = functools.partial(_s2_kernel, kb=kb, n1=n1)
    return pl.pallas_call(
        kern,
        grid=(n // (kb * n1), p),
        in_specs=[pl.BlockSpec((1, 2, kb * n1, c), lambda k, q: (q, 0, k, 0)),
                  pl.BlockSpec((kb, 2 * n1, 2 * n1), lambda k, q: (k, 0, 0)),
                  pl.BlockSpec((1, 2, kb * n1, c), lambda k, q: (layer, 0, k, 0)),
                  pl.BlockSpec((kb, 2 * n1, 2 * n1), lambda k, q: (k, 0, 0))],
        out_specs=pl.BlockSpec((1, 2, kb * n1, c), lambda k, q: (q, 0, k, 0)),
        out_shape=jax.ShapeDtypeStruct((p, 2, n, c), BF16),
        compiler_params=_cparams("parallel", "arbitrary"),
        name="dft_stage2",
    )(bv, m1, kf, m2)


def _s3_kernel(g_ref, f_ref, z_ref, x0_ref, bias_ref, o_ref, y_scr, *, n2h, nb1):
    gt = [_swap_major(g_ref[0, s]) for s in range(2)]
    for j in range(nb1):
        y = _dot(f_ref[...], jnp.concatenate([gt[0][j], gt[1][j]], axis=0))
        y_scr[0, j] = y[:n2h]
        y_scr[1, j] = y[n2h:]
    for s in range(2):
        yh = _swap_major(y_scr[s]) + z_ref[s].astype(F32) * bias_ref[0]
        o_ref[s] = (x0_ref[s].astype(F32) * yh).astype(BF16)


def _dft_stage3(g5, f3, z4, x04, bias, layer, nb1):
    p, _, n2, n1, c = g5.shape
    n2h = n2 // 2
    kern = functools.partial(_s3_kernel, n2h=n2h, nb1=nb1)
    return pl.pallas_call(
        kern,
        grid=(p, n1 // nb1),
        in_specs=[pl.BlockSpec((1, 2, n2, nb1, c), lambda q, j: (q, 0, 0, j, 0)),
                  pl.BlockSpec((2 * n2h, 2 * n2), lambda q, j: (0, 0)),
                  pl.BlockSpec((2, n2h, nb1, c), lambda q, j: (q, 0, j, 0)),
                  pl.BlockSpec((2, n2h, nb1, c), lambda q, j: (q, 0, j, 0)),
                  pl.BlockSpec((1, 1, c), lambda q, j: (layer, 0, 0))],
        out_specs=pl.BlockSpec((2, n2h, nb1, c), lambda q, j: (q, 0, j, 0)),
        out_shape=jax.ShapeDtypeStruct((2 * p, n2h, n1, c), BF16),
        scratch_shapes=[pltpu.VMEM((2, nb1, n2h, c), F32)],
        compiler_params=_cparams("parallel", "parallel"),
        name="dft_stage3",
    )(g5, f3, z4, x04, bias)


class _LongConv:
    def __init__(self, L, c, k2s):
        n = 2 * L
        n1 = 1 << (int(math.log2(n)) // 2)
        n2 = n // n1
        assert n1 * n2 == n and n1 == n2, "long-convolution path needs 2L to be a square power of two"
        self.L, self.c, self.n1, self.n2 = L, c, n1, n2
        self.nb1 = min(n1, 16)
        self.kb = 8
        cst = _dft_consts(n1, n2)
        self.f1 = jnp.asarray(cst["f1"], BF16)
        self.f3 = jnp.asarray(cst["f3"], BF16)
        self.m1, self.m2 = _dft_tables(cst, n1, n2)
        depth = k2s.shape[0]
        bk = _dft_stage1_filter(k2s.reshape(depth, n2, n1, c), jnp.asarray(cst["f1k"], BF16), self.nb1)
        self.kf = _dft_stage2_filter(bk.reshape(depth, 2, n, c), self.m1, n1, self.kb)

    def __call__(self, z, x0, bias, layer):
        b, L, c = z.shape
        n1, n2 = self.n1, self.n2
        z4 = z.reshape(b, n2 // 2, n1, c)
        x04 = x0.reshape(b, n2 // 2, n1, c)
        b5 = _dft_stage1(z4, self.f1, n2, self.nb1)
        gv = _dft_stage2(b5.reshape(b // 2, 2, n1 * n2, c), self.m1, self.kf, layer, self.m2, n1, self.kb)
        y = _dft_stage3(gv.reshape(b // 2, 2, n2, n1, c), self.f3, z4, x04, bias, layer, self.nb1)
        return y.reshape(b, L, c)


def _short_consts(L):
    n = 2 * L
    a = -2.0 * np.pi * np.arange(n)[:, None] * np.arange(n)[None, :] / n
    fr, fi = np.cos(a), np.sin(a)
    ff = np.block([[fr[:, :L], -fi[:, :L]], [fi[:, :L], fr[:, :L]]])
    fk = np.concatenate([fr, fi], axis=0)
    cr, ci = fr[:L] / n, -fi[:L] / n
    finv = np.block([[cr, -ci], [ci, cr]])
    return ff.astype(np.float32), fk.astype(np.float32), finv.astype(np.float32)


def _short_conv_kernel(z_ref, x0_ref, k_ref, ff_ref, fk_ref, fi_ref, bias_ref, o_ref, *, L):
    n = 2 * L
    x = jnp.concatenate([z_ref[0], z_ref[1]], axis=0)
    xf = _dot(ff_ref[...], x)
    kf = _dot(fk_ref[...], k_ref[0].astype(BF16))
    xr, xi, kr, ki = xf[:n], xf[n:], kf[:n], kf[n:]
    yin = jnp.concatenate([xr * kr - xi * ki, xr * ki + xi * kr], axis=0).astype(BF16)
    y = _dot(fi_ref[...], yin)
    for s in range(2):
        yh = y[s * L:(s + 1) * L] + z_ref[s].astype(F32) * bias_ref[0]
        o_ref[s] = (x0_ref[s].astype(F32) * yh).astype(BF16)


def _short_conv(z, x0, k2s, layer, consts, bias):
    b, L, c = z.shape
    n = 2 * L
    cb = min(c, 256)
    ff, fk, finv = consts
    kern = functools.partial(_short_conv_kernel, L=L)
    return pl.pallas_call(
        kern,
        grid=(b // 2, c // cb),
        in_specs=[pl.BlockSpec((2, L, cb), lambda p, j: (p, 0, j)),
                  pl.BlockSpec((2, L, cb), lambda p, j: (p, 0, j)),
                  pl.BlockSpec((1, n, cb), lambda p, j: (layer, 0, j)),
                  pl.BlockSpec((2 * n, 2 * L), lambda p, j: (0, 0)),
                  pl.BlockSpec((2 * n, n), lambda p, j: (0, 0)),
                  pl.BlockSpec((2 * L, 2 * n), lambda p, j: (0, 0)),
                  pl.BlockSpec((1, 1, cb), lambda p, j: (layer, 0, j))],
        out_specs=pl.BlockSpec((2, L, cb), lambda p, j: (p, 0, j)),
        out_shape=jax.ShapeDtypeStruct((b, L, c), BF16),
        compiler_params=_cparams("parallel", "parallel"),
        name="short_conv",
    )(z, x0, k2s, ff, fk, finv, bias)


def _s5_matrices(lam_re, lam_im, log_step, b_re, b_im, c_re, c_im, d):
    T = S5_CHUNK
    lr, li = lam_re.astype(F32), lam_im.astype(F32)
    dt = jnp.exp(log_step.astype(F32))[..., None]
    mag = jnp.exp(lr * dt)
    a_r, a_i = mag * jnp.cos(li * dt), mag * jnp.sin(li * dt)
    den = lr * lr + li * li
    q_r = ((a_r - 1.0) * lr + a_i * li) / den
    q_i = (a_i * lr - (a_r - 1.0) * li) / den
    br, bi = b_re.astype(F32), b_im.astype(F32)
    bb_r = q_r[..., None] * br - q_i[..., None] * bi
    bb_i = q_r[..., None] * bi + q_i[..., None] * br
    D, _, G, P, H = bb_r.shape
    pr, pi = [jnp.ones_like(a_r)], [jnp.zeros_like(a_i)]
    for _ in range(T):
        pr.append(pr[-1] * a_r - pi[-1] * a_i)
        pi.append(pr[-2] * a_i + pi[-1] * a_r)
    pw_r, pw_i = jnp.stack(pr, axis=-1), jnp.stack(pi, axis=-1)
    ct_r = jnp.swapaxes(c_re.astype(F32), -1, -2)
    ct_i = jnp.swapaxes(c_im.astype(F32), -1, -2)
    cat_r = (pw_r[..., :, None] * ct_r[..., None, :]
             - pw_i[..., :, None] * ct_i[..., None, :]).reshape(D, 2, G, P, (T + 1) * H)
    cat_i = (pw_i[..., :, None] * ct_r[..., None, :]
             + pw_r[..., :, None] * ct_i[..., None, :]).reshape(D, 2, G, P, (T + 1) * H)
    kall = (jnp.einsum('dkgph,dkgpn->dkghn', bb_r, cat_r[..., :T * H], precision=HIGHEST)
            - jnp.einsum('dkgph,dkgpn->dkghn', bb_i, cat_i[..., :T * H], precision=HIGHEST))
    kf = kall[:, 0]
    kb_rev = jnp.flip(kall[:, 1].reshape(D, G, H, T, H), axis=-2).reshape(D, G, H, T * H)
    zpad = jnp.zeros((D, G, H, (T - 1) * H), F32)
    kf_pad = jnp.concatenate([zpad, kf], axis=-1)
    kb_pad = jnp.concatenate([kb_rev, zpad], axis=-1)
    rows = [kf_pad[..., (T - 1 - j) * H:(2 * T - 1 - j) * H] + kb_pad[..., (T - 1 - j) * H:(2 * T - 1 - j) * H]
            for j in range(T)]
    m_mat = jnp.stack(rows, axis=2).reshape(D, G, T * H, T * H)
    bt_r, bt_i = jnp.swapaxes(bb_r, -1, -2), jnp.swapaxes(bb_i, -1, -2)
    pj_r = jnp.moveaxis(pw_r[..., :T], -1, -2)
    pj_i = jnp.moveaxis(pw_i[..., :T], -1, -2)

    def end_state(k, flip):
        qr, qi = pj_r[:, k], pj_i[:, k]
        if flip:
            qr, qi = jnp.flip(qr, axis=-2), jnp.flip(qi, axis=-2)
        er = qr[..., :, None, :] * bt_r[:, k][..., None, :, :] - qi[..., :, None, :] * bt_i[:, k][..., None, :, :]
        ei = qr[..., :, None, :] * bt_i[:, k][..., None, :, :] + qi[..., :, None, :] * bt_r[:, k][..., None, :, :]
        return er.reshape(D, G, T * H, P), ei.reshape(D, G, T * H, P)

    ef_r, ef_i = end_state(0, True)
    eb_r, eb_i = end_state(1, False)
    e_mat = jnp.concatenate([ef_r, eb_r, ef_i, eb_i], axis=-1)
    cf_r, cf_i = cat_r[:, 0][..., H:], -cat_i[:, 0][..., H:]
    rev = lambda v: jnp.flip(v.reshape(D, G, P, T, H), axis=-2).reshape(D, G, P, T * H)
    cb_r, cb_i = rev(cat_r[:, 1][..., H:]), rev(-cat_i[:, 1][..., H:])
    zp = jnp.zeros_like(cf_r)
    w_out = jnp.concatenate([m_mat, cf_r, zp, cf_i, zp, zp, cb_r, zp, cb_i], axis=-2)
    at_r = jnp.concatenate([pw_r[:, 0, :, :, T], pw_r[:, 1, :, :, T]], axis=-1)[:, :, None, :]
    at_i = jnp.concatenate([pw_i[:, 0, :, :, T], pw_i[:, 1, :, :, T]], axis=-1)[:, :, None, :]
    d_t = jnp.tile(d.astype(F32).reshape(D, G, 1, H), (1, 1, 1, T))
    return e_mat.astype(BF16), w_out.astype(BF16), at_r, at_i, d_t


def _s5_kernel(x_ref, e_ref, w_ref, ar_ref, ai_ref, d_ref, init_ref, y_ref, fin_ref,
               e_scr, sa_scr, sb_scr, *, gb, nb, nc, p2):
    nt = nc // 2
    lane = lax.broadcasted_iota(jnp.int32, (2 * nb, p2), 1)
    is_fwd = lane < (p2 // 2)
    first = lax.broadcasted_iota(jnp.int32, (2 * nb, p2), 0) < nb
    swap = lambda v: pltpu.roll(v, nb, axis=0)
    for g in range(gb):
        e_scr[g] = _dot(x_ref[g], e_ref[g])

    def step(k, carry):
        out = []
        rf = pl.ds(pl.multiple_of(k * 2 * nb, 2 * nb), 2 * nb)
        rb = pl.ds(pl.multiple_of((nt - 1 - k) * 2 * nb, 2 * nb), 2 * nb)
        for g in range(gb):
            cr, ci = carry[2 * g], carry[2 * g + 1]
            ar, ai = ar_ref[g], ai_ref[g]
            er = jnp.where(is_fwd, e_scr[g, rf, 0:p2], swap(e_scr[g, rb, 0:p2]))
            ei = jnp.where(is_fwd, e_scr[g, rf, p2:2 * p2], swap(e_scr[g, rb, p2:2 * p2]))
            ur = ar * cr - ai * ci + er
            ui = ar * ci + ai * cr + ei
            ur4, ui4 = swap(ur), swap(ui)
            sr = jnp.where(first, cr, ur4)
            si = jnp.where(first, ci, ui4)
            sa_scr[g, rf, 0:p2] = sr
            sa_scr[g, rf, p2:2 * p2] = si
            sb_scr[g, rb, 0:p2] = swap(sr)
            sb_scr[g, rb, p2:2 * p2] = swap(si)
            xr = jnp.where(first, ur, ur4)
            xi = jnp.where(first, ui, ui4)
            zr = ar * xr - ai * xi + er
            zi = ar * xi + ai * xr + ei
            out.append(jnp.where(first, swap(zr), zr))
            out.append(jnp.where(first, swap(zi), zi))
        return tuple(out)

    init = []
    for g in range(gb):
        init += [init_ref[g, :, 0:p2], init_ref[g, :, p2:2 * p2]]
    fin = lax.fori_loop(0, nt, step, tuple(init))
    for g in range(gb):
        fin_ref[g] = jnp.concatenate([fin[2 * g], fin[2 * g + 1]], axis=1)
        x = x_ref[g]
        lhs = jnp.concatenate([x, sa_scr[g].astype(BF16), sb_scr[g].astype(BF16)], axis=1)
        y = _dot(lhs, w_ref[g]) + x.astype(F32) * d_ref[g]
        y_ref[g] = jax.nn.gelu(y).astype(BF16)


def _s5_scan(xg, mats, layer, init, nb):
    e_mat, w_out, at_r, at_i, d_t = mats
    G, R, th = xg.shape
    p4 = e_mat.shape[-1]
    p2 = p4 // 2
    gb = S5_GROUPS_PER_STEP if G % S5_GROUPS_PER_STEP == 0 else 2
    nc = R // nb
    assert 2 * nb == 8 and nc % 2 == 0, "two chunks of batch rows must fill one 8-sublane tile"
    kern = functools.partial(_s5_kernel, gb=gb, nb=nb, nc=nc, p2=p2)
    g3 = lambda i: (i, 0, 0)
    l4 = lambda i: (layer, i, 0, 0)
    return pl.pallas_call(
        kern,
        grid=(G // gb,),
        in_specs=[pl.BlockSpec((gb, R, th), g3),
                  pl.BlockSpec((None, gb, th, p4), l4),
                  pl.BlockSpec((None, gb, th + 2 * p4, th), l4),
                  pl.BlockSpec((None, gb, 1, p2), l4),
                  pl.BlockSpec((None, gb, 1, p2), l4),
                  pl.BlockSpec((None, gb, 1, th), l4),
                  pl.BlockSpec((gb, 2 * nb, p4), g3)],
        out_specs=[pl.BlockSpec((gb, R, th), g3),
                   pl.BlockSpec((gb, 2 * nb, p4), g3)],
        out_shape=[jax.ShapeDtypeStruct((G, R, th), BF16),
                   jax.ShapeDtypeStruct((G, 2 * nb, p4), F32)],
        scratch_shapes=[pltpu.VMEM((gb, R, p4), F32),
                        pltpu.VMEM((gb, R, p4), F32),
                        pltpu.VMEM((gb, R, p4), F32)],
        compiler_params=_cparams("parallel"),
        name="s5_scan",
    )(xg, e_mat, w_out, at_r, at_i, d_t, init)


def _lane_group(rows, h):
    return lax.broadcasted_iota(jnp.int32, (rows, LANES), 1) // h


def _block_transpose(vs, h):
    n = len(vs)
    assert n * h == LANES and n & (n - 1) == 0
    blk = _lane_group(vs[0].shape[0], h)
    s = 1
    while s < n:
        upper = (blk & s) != 0
        out = list(vs)
        for i in range(n):
            if i & s == 0:
                a, b = vs[i], vs[i | s]
                out[i] = jnp.where(upper, pltpu.roll(b, s * h, axis=1), a)
                out[i | s] = jnp.where(upper, b, pltpu.roll(a, LANES - s * h, axis=1))
        vs = out
        s *= 2
    return vs


def _s5_pack_kernel(u_ref, o_ref, t_scr, *, h, rows):
    gl = LANES // h
    t_scr[...] = _swap_major(u_ref[...]).astype(F32)
    pr = min(rows, PACK_ROWS)
    for rb in range(rows // pr):
        rs = slice(rb * pr, (rb + 1) * pr)
        for half in range(S5_CHUNK // gl):
            xs = _block_transpose([t_scr[half * gl + jj, rs, :] for jj in range(gl)], h)
            for g in range(gl):
                o_ref[g, rs, half * LANES:(half + 1) * LANES] = xs[g].astype(BF16)


def _s5_unpack_kernel(y_ref, o_ref, t_scr, *, h, rows):
    gl = LANES // h
    pr = min(rows, PACK_ROWS)
    for rb in range(rows // pr):
        rs = slice(rb * pr, (rb + 1) * pr)
        for half in range(S5_CHUNK // gl):
            ys = _block_transpose([y_ref[g, rs, half * LANES:(half + 1) * LANES].astype(F32) for g in range(gl)], h)
            for tt in range(gl):
                t_scr[half * gl + tt, rs, :] = ys[tt].astype(BF16)
    o_ref[...] = _swap_major(t_scr[...])


def _s5_pack(u3, h):
    R, t, w = u3.shape
    gl = LANES // h
    rows = min(R, 256)
    kern = functools.partial(_s5_pack_kernel, h=h, rows=rows)
    return pl.pallas_call(
        kern,
        grid=(w // LANES, R // rows),
        in_specs=[pl.BlockSpec((rows, t, LANES), lambda l, i: (i, 0, l))],
        out_specs=pl.BlockSpec((gl, rows, t * h), lambda l, i: (l, i, 0)),
        out_shape=jax.ShapeDtypeStruct((w // h, R, t * h), BF16),
        scratch_shapes=[pltpu.VMEM((t, rows, LANES), F32)],
        compiler_params=_cparams("parallel", "parallel"),
        name="s5_pack",
    )(u3)


def _s5_unpack(yg, h):
    G, R, th = yg.shape
    t = th // h
    gl = LANES // h
    rows = min(R, 256)
    kern = functools.partial(_s5_unpack_kernel, h=h, rows=rows)
    return pl.pallas_call(
        kern,
        grid=(G // gl, R // rows),
        in_specs=[pl.BlockSpec((gl, rows, th), lambda l, i: (l, i, 0))],
        out_specs=pl.BlockSpec((rows, t, LANES), lambda l, i: (i, 0, l)),
        out_shape=jax.ShapeDtypeStruct((R, t, G * h), BF16),
        scratch_shapes=[pltpu.VMEM((t, rows, LANES), BF16)],
        compiler_params=_cparams("parallel", "parallel"),
        name="s5_unpack",
    )(yg)


def _s5_mixer(u4, mats, layer, init, h):
    lc, nb, t, w = u4.shape
    yg, fin = _s5_scan(_s5_pack(u4.reshape(lc * nb, t, w), h), mats, layer, init, nb)
    return _s5_unpack(yg, h).reshape(lc, nb, t, w), fin


def _outproj_kernel(yh_ref, ys_ref, x_ref, mod_ref, g_ref, wg_ref, bg_ref, wo_ref,
                    xo_ref, hx_ref, *, d, dh, tm):
    gate = mod_ref[0, :, 2 * d:3 * d]
    shift = mod_ref[0, :, 3 * d:4 * d]
    scale = mod_ref[0, :, 4 * d:5 * d]
    ns = max(1, min(SUB_TILES, tm // OUT_SUB_ROWS))
    sub = tm // ns
    rows = [slice(s * sub, (s + 1) * sub) for s in range(ns)]
    ys = [ys_ref[s * sub // S5_CHUNK:(s + 1) * sub // S5_CHUNK, 0, :, :].reshape(sub, ys_ref.shape[-1])
          for s in range(ns)]
    pre = [_dot(y, wg_ref[0]) for y in ys]
    glu = [(y.astype(F32) * jax.nn.sigmoid(a + bg_ref[0])).astype(BF16) for y, a in zip(ys, pre)]
    yx = [_dot(yh_ref[0, rs, :], wo_ref[0, 0:dh, :]) + _dot(gl, wo_ref[0, dh:, :]) for rs, gl in zip(rows, glu)]
    for rs, v in zip(rows, yx):
        xo = x_ref[0, rs, :] + gate * _rms(v, g_ref[0, 1:2, :])
        xo_ref[0, rs, :] = xo
        hx_ref[0, rs, :] = (_rms(xo, g_ref[0, 2:3, :]) * (1.0 + scale) + shift).astype(BF16)


def _outproj(yh, ys4, x, mod, mod_row, layer, norm_g, w_glu, b_glu, w_out, tm):
    b, s, d = x.shape
    dh = yh.shape[-1]
    ds5 = ys4.shape[-1]
    tc = tm // S5_CHUNK
    kern = functools.partial(_outproj_kernel, d=d, dh=dh, tm=tm)
    lyr = lambda bi, i: (layer, 0, 0)
    t3 = lambda bi, i: (bi, i, 0)
    return pl.pallas_call(
        kern,
        grid=(b, s // tm),
        in_specs=[pl.BlockSpec((1, tm, dh), t3),
                  pl.BlockSpec((tc, 1, S5_CHUNK, ds5), lambda bi, i: (i, bi, 0, 0)),
                  pl.BlockSpec((1, tm, d), t3),
                  pl.BlockSpec((1, 1, mod.shape[-1]), lambda bi, i: (mod_row(bi), 0, 0)),
                  pl.BlockSpec((1,) + norm_g.shape[1:], lyr),
                  pl.BlockSpec((1, ds5, ds5), lyr), pl.BlockSpec((1, 1, ds5), lyr),
                  pl.BlockSpec((1, dh + ds5, d), lyr)],
        out_specs=[pl.BlockSpec((1, tm, d), t3), pl.BlockSpec((1, tm, d), t3)],
        out_shape=[jax.ShapeDtypeStruct((b, s, d), F32), jax.ShapeDtypeStruct((b, s, d), BF16)],
        compiler_params=_cparams("parallel", "parallel"),
        name="outproj",
    )(yh, ys4, x, mod, norm_g, w_glu, b_glu, w_out)


def _ffn_kernel(hm_ref, hp_ref, hn_ref, x_ref, mod_ref, g_ref, wg_ref, wv_ref, cw_ref, cb_ref, wd_ref,
                o_ref, acc_ref, *, tm, d, wg, vertical, kp):
    k = pl.program_id(3)
    i = pl.program_id(1) * kp + k
    nt = pl.num_programs(1) * kp
    j = pl.program_id(2)
    nj = pl.num_programs(2)
    hm = hm_ref[0]
    cw = cw_ref[0]
    if vertical:
        top = jnp.where(i > 0, 1.0, 0.0).astype(BF16)
        bot = jnp.where(i < nt - 1, 1.0, 0.0).astype(BF16)
        ha = jnp.concatenate([hp_ref[0] * top, hm, hn_ref[0] * bot], axis=0)
    else:
        ha = hm
    fc = wg_ref.shape[-1]
    c1 = -(-(fc // 2) // FFN_COL_ALIGN) * FFN_COL_ALIGN
    segs = [(0, c1), (c1, fc)] if 0 < c1 < fc else [(0, fc)]
    gs = [_dot(ha, wg_ref[0, :, a:b]) for a, b in segs]
    vs = [_dot(hm, wv_ref[0, :, a:b]) for a, b in segs]

    def vcol(g, w, dx):
        if not vertical:
            return g * w[3 + dx:4 + dx]
        return (g[0:tm] * w[dx:dx + 1] + g[wg:wg + tm] * w[3 + dx:4 + dx]
                + g[2 * wg:2 * wg + tm] * w[6 + dx:7 + dx])

    hmid = []
    for (a, b), g, v in zip(segs, gs, vs):
        w = cw[:, a:b]
        col = lax.broadcasted_iota(jnp.int32, (tm, b - a), 0) & (wg - 1)
        conv = (vcol(g, w, 1) + jnp.where(col > 0, pltpu.roll(vcol(g, w, 0), 1, axis=0), 0.0)
                + jnp.where(col < wg - 1, pltpu.roll(vcol(g, w, 2), tm - 1, axis=0), 0.0) + cb_ref[0, :, a:b])
        hmid.append((jax.nn.gelu(conv) * v).astype(BF16))
    part = _dot(hmid[0], wd_ref[0, segs[0][0]:segs[0][1], :])
    for (a, b), h in zip(segs[1:], hmid[1:]):
        part = part + _dot(h, wd_ref[0, a:b, :])

    @pl.when(j == 0)
    def _():
        acc_ref[k] = part

    @pl.when(j > 0)
    def _():
        acc_ref[k] += part

    @pl.when(j == nj - 1)
    def _():
        gate = mod_ref[0, :, 5 * d:6 * d]
        o_ref[0] = x_ref[0] + gate * _rms(acc_ref[k], g_ref[0, 3:4, :])


def _ffn(hx, x, mod, mod_row, layer, norm_g, w_up, conv_w, conv_b, w_down, tm, wg, vertical):
    b, s, d = x.shape
    f = w_down.shape[1]
    fc = f // 2 if (f // 2) % LANES == 0 else f
    nf = f // fc
    nt = s // tm
    r = tm // wg if vertical else 1
    hb = wg if vertical else 16
    nhb = s // hb
    kp = FFN_TILE_GROUP if nt % FFN_TILE_GROUP == 0 else 1
    kern = functools.partial(_ffn_kernel, tm=tm, d=d, wg=wg, vertical=vertical, kp=kp)
    tile = lambda ip, k: ip * kp + k
    cur = lambda bi, ip, j, k: (bi, tile(ip, k), 0)
    fin = lambda bi, ip, j, k: (bi, tile(ip, jnp.where(j == nf - 1, k, 0)), 0)
    return pl.pallas_call(
        kern,
        grid=(b, nt // kp, nf, kp),
        in_specs=[pl.BlockSpec((1, tm, d), cur),
                  pl.BlockSpec((1, hb, d), lambda bi, ip, j, k: (bi, jnp.maximum(tile(ip, k) * r - 1, 0), 0)),
                  pl.BlockSpec((1, hb, d), lambda bi, ip, j, k: (bi, jnp.minimum((tile(ip, k) + 1) * r, nhb - 1), 0)),
                  pl.BlockSpec((1, tm, d), fin),
                  pl.BlockSpec((1, 1, mod.shape[-1]), lambda bi, ip, j, k: (mod_row(bi), 0, 0)),
                  pl.BlockSpec((1,) + norm_g.shape[1:], lambda bi, ip, j, k: (layer, 0, 0)),
                  pl.BlockSpec((1, d, fc), lambda bi, ip, j, k: (layer, 0, j)),
                  pl.BlockSpec((1, d, fc), lambda bi, ip, j, k: (layer, 0, nf + j)),
                  pl.BlockSpec((1, 9, fc), lambda bi, ip, j, k: (layer, 0, j)),
                  pl.BlockSpec((1, 1, fc), lambda bi, ip, j, k: (layer, 0, j)),
                  pl.BlockSpec((1, fc, d), lambda bi, ip, j, k: (layer, j, 0))],
        out_specs=pl.BlockSpec((1, tm, d), fin),
        out_shape=jax.ShapeDtypeStruct((b, s, d), F32),
        scratch_shapes=[pltpu.VMEM((kp, tm, d), F32)],
        compiler_params=_cparams("parallel", "parallel", "arbitrary", "arbitrary"),
        name="conv_glu_ffn",
    )(hx, hx, hx, x, mod, norm_g, w_up, w_up, conv_w, conv_b, w_down)


def kernel(x, c, ctx, c_ctx, w_ada, b_ada, norm_g, w_in, hy_short_w, hy_short_b,
           filt_w_in, filt_b_in, filt_w_hid, filt_b_hid, filt_freq, filt_w_out, hy_bias,
           s5_lam_re, s5_lam_im, s5_log_step, s5_b_re, s5_b_im, s5_c_re, s5_c_im, s5_d,
           s5_w_glu, s5_b_glu, w_out, ffn_w_up, ffn_conv_w, ffn_conv_b, ffn_w_down):
    depth = w_ada.shape[0]
    bsz, seq, d = x.shape
    lctx = ctx.shape[1]
    dh = hy_bias.shape[-1]
    G, P, H = s5_b_re.shape[2], s5_b_re.shape[3], s5_b_re.shape[4]
    dff = ffn_w_down.shape[1]
    assert bsz % 2 == 0 and bsz <= 4 and seq % GRID_W == 0 and GRID_W & (GRID_W - 1) == 0

    cond = jnp.zeros((8, d), F32).at[:bsz].set(c).at[bsz].set(c_ctx)
    mod = _ada_mod(cond, w_ada, b_ada).reshape(depth * 8, 1, 6 * d)

    filt_args = (filt_w_in, filt_b_in, filt_w_hid, filt_b_hid, filt_freq, filt_w_out)
    long_conv = _LongConv(seq, dh, _hyena_filters(seq, *filt_args, dh))
    k_ctx = _hyena_filters(lctx, *filt_args, dh)
    short_c = tuple(jnp.asarray(a, BF16) for a in _short_consts(lctx))

    w_in_b = w_in.astype(BF16)
    w_glu_b = s5_w_glu.astype(BF16)
    w_out_b = w_out.astype(BF16)
    w_up_b = ffn_w_up.astype(BF16)
    w_down_b = ffn_w_down.astype(BF16)
    sb = hy_short_b.reshape(depth, 1, 3 * dh)
    cw = ffn_conv_w.reshape(depth, 9, dff)
    cb = ffn_conv_b.reshape(depth, 1, dff)
    bg = s5_b_glu.reshape(depth, 1, -1)
    hb = hy_bias.reshape(depth, 1, dh)
    mats = _s5_matrices(s5_lam_re, s5_lam_im, s5_log_step, s5_b_re, s5_b_im, s5_c_re, s5_c_im, s5_d)

    tm = min(seq, ROW_TILE)
    tp = min(seq, PROJ_ROW_TILE)
    for l in range(depth):
        last = l == depth - 1
        row_x = lambda bi, l=l: 8 * l + bi
        row_c = lambda bi, l=l: 8 * l + bsz

        x0c, zc, uc = _inproj(ctx, mod, row_c, l, norm_g, w_in_b, hy_short_w, sb, dh, lctx)
        ysc, ctx_state = _s5_mixer(uc, mats, l, jnp.zeros((G, 2 * bsz, 4 * P), F32), H)

        x0, z, u = _inproj(x, mod, row_x, l, norm_g, w_in_b, hy_short_w, sb, dh, tp)
        ys, _ = _s5_mixer(u, mats, l, ctx_state, H)
        yh = long_conv(z, x0, hb, l)
        x, hx = _outproj(yh, ys, x, mod, row_x, l, norm_g, w_glu_b, bg, w_out_b, tp)
        x = _ffn(hx, x, mod, row_x, l, norm_g, w_up_b, cw, cb, w_down_b, tm, GRID_W, True)

        if not last:
            yhc = _short_conv(zc, x0c, k_ctx, l, short_c, hb)
            ctx, hc = _outproj(yhc, ysc, ctx, mod, row_c, l, norm_g, w_glu_b, bg, w_out_b, lctx)
            ctx = _ffn(hc, ctx, mod, row_c, l, norm_g, w_up_b, cw, cb, w_down_b, lctx, lctx, False)
    return x
```

```python
import functools
import math

import numpy as np
import jax
import jax.numpy as jnp
from jax import lax
from jax.experimental import pallas as pl
from jax.experimental.pallas import tpu as pltpu

GRID_W = 64
RMS_EPS = 1e-6
DECAY_TARGET = 1e-2
FAST_DECAY_PCT = 0.3
SLOW_DECAY_PCT = 1.5
S5_CHUNK = 16
S5_GROUPS_PER_STEP = 4
LANES = 128
HALO = 16
ROW_TILE = 512
PROJ_ROW_TILE = 1024
SUB_TILES = 8
OUT_SUB_ROWS = 128
IN_SUB_TILES = 2
FFN_COL_ALIGN = 256
FFN_TILE_GROUP = 2
PACK_ROWS = 128
VMEM_LIMIT = 56 * 1024 * 1024

F32 = jnp.float32
BF16 = jnp.bfloat16
HIGHEST = lax.Precision.HIGHEST


def _cparams(*sem):
    return pltpu.CompilerParams(dimension_semantics=sem, vmem_limit_bytes=VMEM_LIMIT)


def _dot(a, b, **kw):
    return jnp.dot(a, b, preferred_element_type=F32, **kw)


def _ada_kernel(cond_ref, w_ref, b_ref, o_ref):
    cv = cond_ref[...]
    s = cv * jax.nn.sigmoid(cv)
    o_ref[0] = _dot(s, w_ref[0], precision=HIGHEST) + b_ref[0]


def _ada_mod(cond, w_ada, b_ada):
    depth, d, n = w_ada.shape
    tn = n // 4
    return pl.pallas_call(
        _ada_kernel,
        grid=(depth, n // tn),
        in_specs=[pl.BlockSpec((8, d), lambda l, j: (0, 0)),
                  pl.BlockSpec((1, d, tn), lambda l, j: (l, 0, j)),
                  pl.BlockSpec((1, 1, tn), lambda l, j: (l, 0, j))],
        out_specs=pl.BlockSpec((1, 8, tn), lambda l, j: (l, 0, j)),
        out_shape=jax.ShapeDtypeStruct((depth, 8, n), F32),
        compiler_params=_cparams("parallel", "parallel"),
        name="ada_mod",
    )(cond, w_ada, b_ada.reshape(depth, 1, n))


def _rms(v, g):
    ms = jnp.mean(v * v, axis=-1, keepdims=True)
    return v * lax.rsqrt(ms + RMS_EPS) * g


def _inproj_kernel(xm_ref, xp_ref, xn_ref, mod_ref, g_ref, w_ref, sw_ref, sb_ref,
                   x0_ref, z_ref, u_ref, p_scr, *, tm, d, dh):
    i = pl.program_id(1)
    nt = pl.num_programs(1)
    shift = mod_ref[0, :, 0:d]
    scale = mod_ref[0, :, d:2 * d]
    xa = jnp.concatenate([xp_ref[0], xm_ref[0], xn_ref[0]], axis=0)
    sw = sw_ref[0]
    ns = IN_SUB_TILES
    sub = tm // ns
    cut = [0] + [HALO + s * sub for s in range(1, ns)] + [tm + 2 * HALO]

    def project(s):
        lo, hi = cut[s], cut[s + 1]
        xn = (_rms(xa[lo:hi], g_ref[0, 0:1, :]) * (1.0 + scale) + shift).astype(BF16)
        p = _dot(xn, w_ref[0])
        p_scr[lo:hi, :] = p[:, :3 * dh]
        if s == 0:
            inside = jnp.where(i > 0, 1.0, 0.0).astype(F32)
            p_scr[HALO - 8:HALO, :] = p[HALO - 8:HALO, :3 * dh] * inside
        if s == ns - 1:
            inside = jnp.where(i < nt - 1, 1.0, 0.0).astype(F32)
            p_scr[HALO + tm:HALO + tm + 8, :] = p[HALO + tm - lo:HALO + tm + 8 - lo, :3 * dh] * inside
        a, b = max(lo, HALO), min(hi, HALO + tm)
        u = p[a - lo:b - lo, 3 * dh:].astype(BF16)
        u_ref[(a - HALO) // S5_CHUNK:(b - HALO) // S5_CHUNK, 0, :, :] = u.reshape(
            (b - a) // S5_CHUNK, S5_CHUNK, u.shape[-1])

    def conv_gate(s):
        r0 = HALO + s * sub
        conv = (p_scr[pl.ds(r0 - 1, sub), :] * sw[0:1] + p_scr[pl.ds(r0, sub), :] * sw[1:2]
                + p_scr[pl.ds(r0 + 1, sub), :] * sw[2:3] + sb_ref[0])
        rs = slice(s * sub, (s + 1) * sub)
        x0_ref[0, rs, :] = conv[:, :dh].astype(BF16)
        z_ref[0, rs, :] = (conv[:, dh:2 * dh] * conv[:, 2 * dh:]).astype(BF16)

    project(0)
    for s in range(ns):
        if s + 1 < ns:
            project(s + 1)
        conv_gate(s)


def _inproj(x, mod, mod_row, layer, norm_g, w_in, sw, sb, dh, tm):
    b, s, d = x.shape
    dp = w_in.shape[-1]
    ds5 = dp - 3 * dh
    nt = s // tm
    r = tm // HALO
    nh = s // HALO
    tc = tm // S5_CHUNK
    kern = functools.partial(_inproj_kernel, tm=tm, d=d, dh=dh)
    lyr = lambda bi, i: (layer, 0, 0)
    return pl.pallas_call(
        kern,
        grid=(b, nt),
        in_specs=[pl.BlockSpec((1, tm, d), lambda bi, i: (bi, i, 0)),
                  pl.BlockSpec((1, HALO, d), lambda bi, i: (bi, jnp.maximum(i * r - 1, 0), 0)),
                  pl.BlockSpec((1, HALO, d), lambda bi, i: (bi, jnp.minimum((i + 1) * r, nh - 1), 0)),
                  pl.BlockSpec((1, 1, mod.shape[-1]), lambda bi, i: (mod_row(bi), 0, 0)),
                  pl.BlockSpec((1,) + norm_g.shape[1:], lyr),
                  pl.BlockSpec((1, d, dp), lyr),
                  pl.BlockSpec((1, 3, 3 * dh), lyr),
                  pl.BlockSpec((1, 1, 3 * dh), lyr)],
        out_specs=[pl.BlockSpec((1, tm, dh), lambda bi, i: (bi, i, 0)),
                   pl.BlockSpec((1, tm, dh), lambda bi, i: (bi, i, 0)),
                   pl.BlockSpec((tc, 1, S5_CHUNK, ds5), lambda bi, i: (i, bi, 0, 0))],
        out_shape=[jax.ShapeDtypeStruct((b, s, dh), BF16),
                   jax.ShapeDtypeStruct((b, s, dh), BF16),
                   jax.ShapeDtypeStruct((s // S5_CHUNK, b, S5_CHUNK, ds5), BF16)],
        scratch_shapes=[pltpu.VMEM((tm + 2 * HALO, 3 * dh), F32)],
        compiler_params=_cparams("parallel", "arbitrary"),
        name="inproj",
    )(x, x, x, mod, norm_g, w_in, sw, sb)


def _filter_feats(L, emb):
    bands = (emb - 1) // 2
    t = np.linspace(0.0, 1.0, L, dtype=np.float32).astype(np.float64)[:, None]
    w = (2.0 * math.pi / L) * np.arange(L, dtype=np.float64)[:, None]
    f = np.linspace(1e-4, bands - 1, bands, dtype=np.float32).astype(np.float64)[None, :]
    z = np.concatenate([t, np.cos(f * w), -np.sin(f * w)], axis=-1)
    zp = np.zeros((L, LANES), np.float32)
    zp[:, :emb] = z
    return zp


def _filter_kernel(z_ref, win_ref, bin_ref, whid_ref, bhid_ref, fr_ref, wf_ref, wb_ref, dl_ref, o_ref):
    z = z_ref[...]
    fr = fr_ref[0]
    h = jnp.sin(fr * (_dot(z, win_ref[0], precision=HIGHEST) + bin_ref[0]))
    for i in range(whid_ref.shape[1]):
        h = jnp.sin(fr * (_dot(h, whid_ref[0, i], precision=HIGHEST) + bhid_ref[0, i]))
    o_ref[0, 0] = _dot(h, wf_ref[0], precision=HIGHEST) * jnp.exp(-z[:, 0:1] * dl_ref[...])
    hb = _dot(h, wb_ref[0], precision=HIGHEST) * jnp.exp(-z[:, LANES:LANES + 1] * dl_ref[...])
    first = (pl.program_id(1) == 0) & (lax.broadcasted_iota(jnp.int32, (z.shape[0], 1), 0) == 0)
    o_ref[0, 1] = jnp.where(first, 0.0, hb)


def _block_diag2(w):
    zero = jnp.zeros_like(w)
    return jnp.concatenate([jnp.concatenate([w, zero], axis=-1), jnp.concatenate([zero, w], axis=-1)], axis=-2)


def _hyena_filters(L, f_w_in, f_b_in, f_w_hid, f_b_hid, f_freq, f_w_out, dh):
    depth, emb, hid = f_w_in.shape
    n_inner = f_w_hid.shape[1]
    tl = min(L, 1024)
    z1 = _filter_feats(L, emb)
    z = jnp.asarray(np.concatenate([z1, np.concatenate([z1[:1], z1[:0:-1]], axis=0)], axis=1))
    win = _block_diag2(jnp.zeros((depth, LANES, hid), F32).at[:, :emb].set(f_w_in))
    zero = jnp.zeros((depth, hid, dh), F32)
    wf = jnp.concatenate([f_w_out[:, :, :dh], zero], axis=1)
    wb = jnp.concatenate([zero, f_w_out[:, :, dh:]], axis=1)
    twice = lambda v, shp: jnp.tile(v.reshape(shp), (1,) * (len(shp) - 1) + (2,))
    deltas = np.abs(np.linspace(math.log(DECAY_TARGET) / FAST_DECAY_PCT,
                                math.log(DECAY_TARGET) / SLOW_DECAY_PCT, dh, dtype=np.float32))[None, :]
    h2 = 2 * hid
    out = pl.pallas_call(
        _filter_kernel,
        grid=(depth, L // tl),
        in_specs=[pl.BlockSpec((tl, 2 * LANES), lambda l, i: (i, 0)),
                  pl.BlockSpec((1, 2 * LANES, h2), lambda l, i: (l, 0, 0)),
                  pl.BlockSpec((1, 1, h2), lambda l, i: (l, 0, 0)),
                  pl.BlockSpec((1, n_inner, h2, h2), lambda l, i: (l, 0, 0, 0)),
                  pl.BlockSpec((1, n_inner, 1, h2), lambda l, i: (l, 0, 0, 0)),
                  pl.BlockSpec((1, 1, h2), lambda l, i: (l, 0, 0)),
                  pl.BlockSpec((1, h2, dh), lambda l, i: (l, 0, 0)),
                  pl.BlockSpec((1, h2, dh), lambda l, i: (l, 0, 0)),
                  pl.BlockSpec((1, dh), lambda l, i: (0, 0))],
        out_specs=pl.BlockSpec((1, 2, tl, dh), lambda l, i: (l, 0, i, 0)),
        out_shape=jax.ShapeDtypeStruct((depth, 2, L, dh), F32),
        compiler_params=_cparams("parallel", "parallel"),
        name="hyena_filter",
    )(z, win, twice(f_b_in, (depth, 1, hid)), _block_diag2(f_w_hid), twice(f_b_hid, (depth, n_inner, 1, hid)),
      twice(f_freq, (depth, 1, hid)), wf, wb, jnp.asarray(deltas))
    return out.reshape(depth, 2 * L, dh)


def _dft_consts(n1, n2):
    n = n1 * n2
    n2h = n2 // 2
    k2 = np.arange(n2)[:, None]
    a = -2.0 * np.pi * k2 * np.arange(n2)[None, :] / n2
    fr, fi = np.cos(a), np.sin(a)
    f1 = np.block([[fr[:, :n2h], -fi[:, :n2h]], [fi[:, :n2h], fr[:, :n2h]]])
    f1k = np.concatenate([fr, fi], axis=0)
    cr, ci = fr[:n2h] / n, -fi[:n2h] / n
    f3 = np.block([[cr, -ci], [ci, cr]])
    b = -2.0 * np.pi * np.arange(n1)[:, None] * np.arange(n1)[None, :] / n1
    t = -2.0 * np.pi * np.arange(n2)[:, None] * np.arange(n1)[None, :] / n
    return dict(f1=f1.astype(np.float32), f1k=f1k.astype(np.float32), f3=f3.astype(np.float32),
                gr=np.cos(b).astype(np.float32), gi=np.sin(b).astype(np.float32),
                tr=np.cos(t).astype(np.float32)[:, None, :], ti=np.sin(t).astype(np.float32)[:, None, :])


def _tables_kernel(gr_ref, gi_ref, tr_ref, ti_ref, m1_ref, m2_ref, *, kb):
    gr, gi = gr_ref[...], gi_ref[...]
    for q in range(kb):
        tr, ti = tr_ref[q], ti_ref[q]
        re = gr * tr - gi * ti
        im = gr * ti + gi * tr
        m1_ref[q] = jnp.concatenate([jnp.concatenate([re, -im], axis=1),
                                     jnp.concatenate([im, re], axis=1)], axis=0).astype(BF16)
        ret, imt = re.T, im.T
        m2_ref[q] = jnp.concatenate([jnp.concatenate([ret, imt], axis=1),
                                     jnp.concatenate([-imt, ret], axis=1)], axis=0).astype(BF16)


def _dft_tables(c, n1, n2):
    kb = 8
    kern = functools.partial(_tables_kernel, kb=kb)
    shp = jax.ShapeDtypeStruct((n2, 2 * n1, 2 * n1), BF16)
    return pl.pallas_call(
        kern,
        grid=(n2 // kb,),
        in_specs=[pl.BlockSpec((n1, n1), lambda i: (0, 0)),
                  pl.BlockSpec((n1, n1), lambda i: (0, 0)),
                  pl.BlockSpec((kb, 1, n1), lambda i: (i, 0, 0)),
                  pl.BlockSpec((kb, 1, n1), lambda i: (i, 0, 0))],
        out_specs=[pl.BlockSpec((kb, 2 * n1, 2 * n1), lambda i: (i, 0, 0)),
                   pl.BlockSpec((kb, 2 * n1, 2 * n1), lambda i: (i, 0, 0))],
        out_shape=[shp, shp],
        compiler_params=_cparams("parallel"),
        name="dft_tables",
    )(jnp.asarray(c["gr"]), jnp.asarray(c["gi"]), jnp.asarray(c["tr"]), jnp.asarray(c["ti"]))


def _swap_major(v):
    return pltpu.einshape("abc->bac", v)


def _s1_kernel(z_ref, f_ref, o_ref, y_scr, *, n2, nb1):
    zt = [_swap_major(z_ref[s]) for s in range(2)]
    for j in range(nb1):
        y = _dot(f_ref[...], jnp.concatenate([zt[0][j], zt[1][j]], axis=0))
        y_scr[0, j] = y[:n2].astype(BF16)
        y_scr[1, j] = y[n2:].astype(BF16)
    for s in range(2):
        o_ref[0, s] = _swap_major(y_scr[s])


def _s1k_kernel(k_ref, f_ref, o_ref, y_scr, *, n2, nb1):
    kt = _swap_major(k_ref[0].astype(BF16))
    for j in range(nb1):
        y = _dot(f_ref[...], kt[j])
        y_scr[0, j] = y[:n2].astype(BF16)
        y_scr[1, j] = y[n2:].astype(BF16)
    for s in range(2):
        o_ref[0, s] = _swap_major(y_scr[s])


def _dft_stage1(z4, f1, n2, nb1):
    b, n2h, n1, c = z4.shape
    kern = functools.partial(_s1_kernel, n2=n2, nb1=nb1)
    return pl.pallas_call(
        kern,
        grid=(b // 2, n1 // nb1),
        in_specs=[pl.BlockSpec((2, n2h, nb1, c), lambda p, j: (p, 0, j, 0)),
                  pl.BlockSpec((2 * n2, 2 * n2h), lambda p, j: (0, 0))],
        out_specs=pl.BlockSpec((1, 2, n2, nb1, c), lambda p, j: (p, 0, 0, j, 0)),
        out_shape=jax.ShapeDtypeStruct((b // 2, 2, n2, n1, c), BF16),
        scratch_shapes=[pltpu.VMEM((2, nb1, n2, c), BF16)],
        compiler_params=_cparams("parallel", "parallel"),
        name="dft_stage1",
    )(z4, f1)


def _dft_stage1_filter(k4, f1k, nb1):
    depth, n2, n1, c = k4.shape
    kern = functools.partial(_s1k_kernel, n2=n2, nb1=nb1)
    return pl.pallas_call(
        kern,
        grid=(depth, n1 // nb1),
        in_specs=[pl.BlockSpec((1, n2, nb1, c), lambda l, j: (l, 0, j, 0)),
                  pl.BlockSpec((2 * n2, n2), lambda l, j: (0, 0))],
        out_specs=pl.BlockSpec((1, 2, n2, nb1, c), lambda l, j: (l, 0, 0, j, 0)),
        out_shape=jax.ShapeDtypeStruct((depth, 2, n2, n1, c), BF16),
        scratch_shapes=[pltpu.VMEM((2, nb1, n2, c), BF16)],
        compiler_params=_cparams("parallel", "parallel"),
        name="dft_stage1_filter",
    )(k4, f1k)


def _s2k_kernel(b_ref, m1_ref, o_ref, *, kb, n1):
    for q in range(kb):
        rows = slice(q * n1, (q + 1) * n1)
        xin = jnp.concatenate([b_ref[0, 0, rows, :], b_ref[0, 1, rows, :]], axis=0)
        xf = _dot(m1_ref[q], xin)
        o_ref[0, 0, rows, :] = xf[:n1]
        o_ref[0, 1, rows, :] = xf[n1:]


def _dft_stage2_filter(bv, m1, n1, kb):
    depth, _, n, c = bv.shape
    kern = functools.partial(_s2k_kernel, kb=kb, n1=n1)
    return pl.pallas_call(
        kern,
        grid=(n // (kb * n1), depth),
        in_specs=[pl.BlockSpec((1, 2, kb * n1, c), lambda k, l: (l, 0, k, 0)),
                  pl.BlockSpec((kb, 2 * n1, 2 * n1), lambda k, l: (k, 0, 0))],
        out_specs=pl.BlockSpec((1, 2, kb * n1, c), lambda k, l: (l, 0, k, 0)),
        out_shape=jax.ShapeDtypeStruct((depth, 2, n, c), F32),
        compiler_params=_cparams("parallel", "parallel"),
        name="dft_stage2_filter",
    )(bv, m1)


def _s2_kernel(b_ref, m1_ref, kf_ref, m2_ref, o_ref, *, kb, n1):
    for q in range(kb):
        rows = slice(q * n1, (q + 1) * n1)
        xin = jnp.concatenate([b_ref[0, 0, rows, :], b_ref[0, 1, rows, :]], axis=0)
        xf = _dot(m1_ref[q], xin)
        xr, xi = xf[:n1], xf[n1:]
        kr, ki = kf_ref[0, 0, rows, :], kf_ref[0, 1, rows, :]
        yin = jnp.concatenate([xr * kr - xi * ki, xr * ki + xi * kr], axis=0).astype(BF16)
        g = _dot(m2_ref[q], yin)
        o_ref[0, 0, rows, :] = g[:n1].astype(BF16)
        o_ref[0, 1, rows, :] = g[n1:].astype(BF16)


def _dft_stage2(bv, m1, kf, layer, m2, n1, kb):
    p, _, n, c = bv.shape
    kern = functools.partial(_s2_kernel, kb=kb, n1=n1)
    return pl.pallas_call(
        kern,
        grid=(n // (kb * n1), p),
        in_specs=[pl.BlockSpec((1, 2, kb * n1, c), lambda k, q: (q, 0, k, 0)),
                  pl.BlockSpec((kb, 2 * n1, 2 * n1), lambda k, q: (k, 0, 0)),
                  pl.BlockSpec((1, 2, kb * n1, c), lambda k, q: (layer, 0, k, 0)),
                  pl.BlockSpec((kb, 2 * n1, 2 * n1), lambda k, q: (k, 0, 0))],
        out_specs=pl.BlockSpec((1, 2, kb * n1, c), lambda k, q: (q, 0, k, 0)),
        out_shape=jax.ShapeDtypeStruct((p, 2, n, c), BF16),
        compiler_params=_cparams("parallel", "arbitrary"),
        name="dft_stage2",
    )(bv, m1, kf, m2)


def _s3_kernel(g_ref, f_ref, z_ref, x0_ref, bias_ref, o_ref, y_scr, *, n2h, nb1):
    gt = [_swap_major(g_ref[0, s]) for s in range(2)]
    for j in range(nb1):
        y = _dot(f_ref[...], jnp.concatenate([gt[0][j], gt[1][j]], axis=0))
        y_scr[0, j] = y[:n2h]
        y_scr[1, j] = y[n2h:]
    for s in range(2):
        yh = _swap_major(y_scr[s]) + z_ref[s].astype(F32) * bias_ref[0]
        o_ref[s] = (x0_ref[s].astype(F32) * yh).astype(BF16)


def _dft_stage3(g5, f3, z4, x04, bias, layer, nb1):
    p, _, n2, n1, c = g5.shape
    n2h = n2 // 2
    kern = functools.partial(_s3_kernel, n2h=n2h, nb1=nb1)
    return pl.pallas_call(
        kern,
        grid=(p, n1 // nb1),
        in_specs=[pl.BlockSpec((1, 2, n2, nb1, c), lambda q, j: (q, 0, 0, j, 0)),
                  pl.BlockSpec((2 * n2h, 2 * n2), lambda q, j: (0, 0)),
                  pl.BlockSpec((2, n2h, nb1, c), lambda q, j: (q, 0, j, 0)),
                  pl.BlockSpec((2, n2h, nb1, c), lambda q, j: (q, 0, j, 0)),
                  pl.BlockSpec((1, 1, c), lambda q, j: (layer, 0, 0))],
        out_specs=pl.BlockSpec((2, n2h, nb1, c), lambda q, j: (q, 0, j, 0)),
        out_shape=jax.ShapeDtypeStruct((2 * p, n2h, n1, c), BF16),
        scratch_shapes=[pltpu.VMEM((2, nb1, n2h, c), F32)],
        compiler_params=_cparams("parallel", "parallel"),
        name="dft_stage3",
    )(g5, f3, z4, x04, bias)


class _LongConv:
    def __init__(self, L, c, k2s):
        n = 2 * L
        n1 = 1 << (int(math.log2(n)) // 2)
        n2 = n // n1
        assert n1 * n2 == n and n1 == n2, "long-convolution path needs 2L to be a square power of two"
        self.L, self.c, self.n1, self.n2 = L, c, n1, n2
        self.nb1 = min(n1, 16)
        self.kb = 8
        cst = _dft_consts(n1, n2)
        self.f1 = jnp.asarray(cst["f1"], BF16)
        self.f3 = jnp.asarray(cst["f3"], BF16)
        self.m1, self.m2 = _dft_tables(cst, n1, n2)
        depth = k2s.shape[0]
        bk = _dft_stage1_filter(k2s.reshape(depth, n2, n1, c), jnp.asarray(cst["f1k"], BF16), self.nb1)
        self.kf = _dft_stage2_filter(bk.reshape(depth, 2, n, c), self.m1, n1, self.kb)

    def __call__(self, z, x0, bias, layer):
        b, L, c = z.shape
        n1, n2 = self.n1, self.n2
        z4 = z.reshape(b, n2 // 2, n1, c)
        x04 = x0.reshape(b, n2 // 2, n1, c)
        b5 = _dft_stage1(z4, self.f1, n2, self.nb1)
        gv = _dft_stage2(b5.reshape(b // 2, 2, n1 * n2, c), self.m1, self.kf, layer, self.m2, n1, self.kb)
        y = _dft_stage3(gv.reshape(b // 2, 2, n2, n1, c), self.f3, z4, x04, bias, layer, self.nb1)
        return y.reshape(b, L, c)


def _short_consts(L):
    n = 2 * L
    a = -2.0 * np.pi * np.arange(n)[:, None] * np.arange(n)[None, :] / n
    fr, fi = np.cos(a), np.sin(a)
    ff = np.block([[fr[:, :L], -fi[:, :L]], [fi[:, :L], fr[:, :L]]])
    fk = np.concatenate([fr, fi], axis=0)
    cr, ci = fr[:L] / n, -fi[:L] / n
    finv = np.block([[cr, -ci], [ci, cr]])
    return ff.astype(np.float32), fk.astype(np.float32), finv.astype(np.float32)


def _short_conv_kernel(z_ref, x0_ref, k_ref, ff_ref, fk_ref, fi_ref, bias_ref, o_ref, *, L):
    n = 2 * L
    x = jnp.concatenate([z_ref[0], z_ref[1]], axis=0)
    xf = _dot(ff_ref[...], x)
    kf = _dot(fk_ref[...], k_ref[0].astype(BF16))
    xr, xi, kr, ki = xf[:n], xf[n:], kf[:n], kf[n:]
    yin = jnp.concatenate([xr * kr - xi * ki, xr * ki + xi * kr], axis=0).astype(BF16)
    y = _dot(fi_ref[...], yin)
    for s in range(2):
        yh = y[s * L:(s + 1) * L] + z_ref[s].astype(F32) * bias_ref[0]
        o_ref[s] = (x0_ref[s].astype(F32) * yh).astype(BF16)


def _short_conv(z, x0, k2s, layer, consts, bias):
    b, L, c = z.shape
    n = 2 * L
    cb = min(c, 256)
    ff, fk, finv = consts
    kern = functools.partial(_short_conv_kernel, L=L)
    return pl.pallas_call(
        kern,
        grid=(b // 2, c // cb),
        in_specs=[pl.BlockSpec((2, L, cb), lambda p, j: (p, 0, j)),
                  pl.BlockSpec((2, L, cb), lambda p, j: (p, 0, j)),
                  pl.BlockSpec((1, n, cb), lambda p, j: (layer, 0, j)),
                  pl.BlockSpec((2 * n, 2 * L), lambda p, j: (0, 0)),
                  pl.BlockSpec((2 * n, n), lambda p, j: (0, 0)),
                  pl.BlockSpec((2 * L, 2 * n), lambda p, j: (0, 0)),
                  pl.BlockSpec((1, 1, cb), lambda p, j: (layer, 0, j))],
        out_specs=pl.BlockSpec((2, L, cb), lambda p, j: (p, 0, j)),
        out_shape=jax.ShapeDtypeStruct((b, L, c), BF16),
        compiler_params=_cparams("parallel", "parallel"),
        name="short_conv",
    )(z, x0, k2s, ff, fk, finv, bias)


def _s5_matrices(lam_re, lam_im, log_step, b_re, b_im, c_re, c_im, d):
    T = S5_CHUNK
    lr, li = lam_re.astype(F32), lam_im.astype(F32)
    dt = jnp.exp(log_step.astype(F32))[..., None]
    mag = jnp.exp(lr * dt)
    a_r, a_i = mag * jnp.cos(li * dt), mag * jnp.sin(li * dt)
    den = lr * lr + li * li
    q_r = ((a_r - 1.0) * lr + a_i * li) / den
    q_i = (a_i * lr - (a_r - 1.0) * li) / den
    br, bi = b_re.astype(F32), b_im.astype(F32)
    bb_r = q_r[..., None] * br - q_i[..., None] * bi
    bb_i = q_r[..., None] * bi + q_i[..., None] * br
    D, _, G, P, H = bb_r.shape
    pr, pi = [jnp.ones_like(a_r)], [jnp.zeros_like(a_i)]
    for _ in range(T):
        pr.append(pr[-1] * a_r - pi[-1] * a_i)
        pi.append(pr[-2] * a_i + pi[-1] * a_r)
    pw_r, pw_i = jnp.stack(pr, axis=-1), jnp.stack(pi, axis=-1)
    ct_r = jnp.swapaxes(c_re.astype(F32), -1, -2)
    ct_i = jnp.swapaxes(c_im.astype(F32), -1, -2)
    cat_r = (pw_r[..., :, None] * ct_r[..., None, :]
             - pw_i[..., :, None] * ct_i[..., None, :]).reshape(D, 2, G, P, (T + 1) * H)
    cat_i = (pw_i[..., :, None] * ct_r[..., None, :]
             + pw_r[..., :, None] * ct_i[..., None, :]).reshape(D, 2, G, P, (T + 1) * H)
    kall = (jnp.einsum('dkgph,dkgpn->dkghn', bb_r, cat_r[..., :T * H], precision=HIGHEST)
            - jnp.einsum('dkgph,dkgpn->dkghn', bb_i, cat_i[..., :T * H], precision=HIGHEST))
    kf = kall[:, 0]
    kb_rev = jnp.flip(kall[:, 1].reshape(D, G, H, T, H), axis=-2).reshape(D, G, H, T * H)
    zpad = jnp.zeros((D, G, H, (T - 1) * H), F32)
    kf_pad = jnp.concatenate([zpad, kf], axis=-1)
    kb_pad = jnp.concatenate([kb_rev, zpad], axis=-1)
    rows = [kf_pad[..., (T - 1 - j) * H:(2 * T - 1 - j) * H] + kb_pad[..., (T - 1 - j) * H:(2 * T - 1 - j) * H]
            for j in range(T)]
    m_mat = jnp.stack(rows, axis=2).reshape(D, G, T * H, T * H)
    bt_r, bt_i = jnp.swapaxes(bb_r, -1, -2), jnp.swapaxes(bb_i, -1, -2)
    pj_r = jnp.moveaxis(pw_r[..., :T], -1, -2)
    pj_i = jnp.moveaxis(pw_i[..., :T], -1, -2)

    def end_state(k, flip):
        qr, qi = pj_r[:, k], pj_i[:, k]
        if flip:
            qr, qi = jnp.flip(qr, axis=-2), jnp.flip(qi, axis=-2)
        er = qr[..., :, None, :] * bt_r[:, k][..., None, :, :] - qi[..., :, None, :] * bt_i[:, k][..., None, :, :]
        ei = qr[..., :, None, :] * bt_i[:, k][..., None, :, :] + qi[..., :, None, :] * bt_r[:, k][..., None, :, :]
        return er.reshape(D, G, T * H, P), ei.reshape(D, G, T * H, P)

    ef_r, ef_i = end_state(0, True)
    eb_r, eb_i = end_state(1, False)
    e_mat = jnp.concatenate([ef_r, eb_r, ef_i, eb_i], axis=-1)
    cf_r, cf_i = cat_r[:, 0][..., H:], -cat_i[:, 0][..., H:]
    rev = lambda v: jnp.flip(v.reshape(D, G, P, T, H), axis=-2).reshape(D, G, P, T * H)
    cb_r, cb_i = rev(cat_r[:, 1][..., H:]), rev(-cat_i[:, 1][..., H:])
    zp = jnp.zeros_like(cf_r)
    w_out = jnp.concatenate([m_mat, cf_r, zp, cf_i, zp, zp, cb_r, zp, cb_i], axis=-2)
    at_r = jnp.concatenate([pw_r[:, 0, :, :, T], pw_r[:, 1, :, :, T]], axis=-1)[:, :, None, :]
    at_i = jnp.concatenate([pw_i[:, 0, :, :, T], pw_i[:, 1, :, :, T]], axis=-1)[:, :, None, :]
    d_t = jnp.tile(d.astype(F32).reshape(D, G, 1, H), (1, 1, 1, T))
    return e_mat.astype(BF16), w_out.astype(BF16), at_r, at_i, d_t


def _s5_kernel(x_ref, e_ref, w_ref, ar_ref, ai_ref, d_ref, init_ref, y_ref, fin_ref,
               e_scr, sa_scr, sb_scr, *, gb, nb, nc, p2):
    nt = nc // 2
    lane = lax.broadcasted_iota(jnp.int32, (2 * nb, p2), 1)
    is_fwd = lane < (p2 // 2)
    first = lax.broadcasted_iota(jnp.int32, (2 * nb, p2), 0) < nb
    swap = lambda v: pltpu.roll(v, nb, axis=0)
    for g in range(gb):
        e_scr[g] = _dot(x_ref[g], e_ref[g])

    def step(k, carry):
        out = []
        rf = pl.ds(pl.multiple_of(k * 2 * nb, 2 * nb), 2 * nb)
        rb = pl.ds(pl.multiple_of((nt - 1 - k) * 2 * nb, 2 * nb), 2 * nb)
        for g in range(gb):
            cr, ci = carry[2 * g], carry[2 * g + 1]
            ar, ai = ar_ref[g], ai_ref[g]
            er = jnp.where(is_fwd, e_scr[g, rf, 0:p2], swap(e_scr[g, rb, 0:p2]))
            ei = jnp.where(is_fwd, e_scr[g, rf, p2:2 * p2], swap(e_scr[g, rb, p2:2 * p2]))
            ur = ar * cr - ai * ci + er
            ui = ar * ci + ai * cr + ei
            ur4, ui4 = swap(ur), swap(ui)
            sr = jnp.where(first, cr, ur4)
            si = jnp.where(first, ci, ui4)
            sa_scr[g, rf, 0:p2] = sr
            sa_scr[g, rf, p2:2 * p2] = si
            sb_scr[g, rb, 0:p2] = swap(sr)
            sb_scr[g, rb, p2:2 * p2] = swap(si)
            xr = jnp.where(first, ur, ur4)
            xi = jnp.where(first, ui, ui4)
            zr = ar * xr - ai * xi + er
            zi = ar * xi + ai * xr + ei
            out.append(jnp.where(first, swap(zr), zr))
            out.append(jnp.where(first, swap(zi), zi))
        return tuple(out)

    init = []
    for g in range(gb):
        init += [init_ref[g, :, 0:p2], init_ref[g, :, p2:2 * p2]]
    fin = lax.fori_loop(0, nt, step, tuple(init))
    for g in range(gb):
        fin_ref[g] = jnp.concatenate([fin[2 * g], fin[2 * g + 1]], axis=1)
        x = x_ref[g]
        lhs = jnp.concatenate([x, sa_scr[g].astype(BF16), sb_scr[g].astype(BF16)], axis=1)
        y = _dot(lhs, w_ref[g]) + x.astype(F32) * d_ref[g]
        y_ref[g] = jax.nn.gelu(y).astype(BF16)


def _s5_scan(xg, mats, layer, init, nb):
    e_mat, w_out, at_r, at_i, d_t = mats
    G, R, th = xg.shape
    p4 = e_mat.shape[-1]
    p2 = p4 // 2
    gb = S5_GROUPS_PER_STEP if G % S5_GROUPS_PER_STEP == 0 else 2
    nc = R // nb
    assert 2 * nb == 8 and nc % 2 == 0, "two chunks of batch rows must fill one 8-sublane tile"
    kern = functools.partial(_s5_kernel, gb=gb, nb=nb, nc=nc, p2=p2)
    g3 = lambda i: (i, 0, 0)
    l4 = lambda i: (layer, i, 0, 0)
    return pl.pallas_call(
        kern,
        grid=(G // gb,),
        in_specs=[pl.BlockSpec((gb, R, th), g3),
                  pl.BlockSpec((None, gb, th, p4), l4),
                  pl.BlockSpec((None, gb, th + 2 * p4, th), l4),
                  pl.BlockSpec((None, gb, 1, p2), l4),
                  pl.BlockSpec((None, gb, 1, p2), l4),
                  pl.BlockSpec((None, gb, 1, th), l4),
                  pl.BlockSpec((gb, 2 * nb, p4), g3)],
        out_specs=[pl.BlockSpec((gb, R, th), g3),
                   pl.BlockSpec((gb, 2 * nb, p4), g3)],
        out_shape=[jax.ShapeDtypeStruct((G, R, th), BF16),
                   jax.ShapeDtypeStruct((G, 2 * nb, p4), F32)],
        scratch_shapes=[pltpu.VMEM((gb, R, p4), F32),
                        pltpu.VMEM((gb, R, p4), F32),
                        pltpu.VMEM((gb, R, p4), F32)],
        compiler_params=_cparams("parallel"),
        name="s5_scan",
    )(xg, e_mat, w_out, at_r, at_i, d_t, init)


def _lane_group(rows, h):
    return lax.broadcasted_iota(jnp.int32, (rows, LANES), 1) // h


def _block_transpose(vs, h):
    n = len(vs)
    assert n * h == LANES and n & (n - 1) == 0
    blk = _lane_group(vs[0].shape[0], h)
    s = 1
    while s < n:
        upper = (blk & s) != 0
        out = list(vs)
        for i in range(n):
            if i & s == 0:
                a, b = vs[i], vs[i | s]
                out[i] = jnp.where(upper, pltpu.roll(b, s * h, axis=1), a)
                out[i | s] = jnp.where(upper, b, pltpu.roll(a, LANES - s * h, axis=1))
        vs = out
        s *= 2
    return vs


def _s5_pack_kernel(u_ref, o_ref, t_scr, *, h, rows):
    gl = LANES // h
    t_scr[...] = _swap_major(u_ref[...]).astype(F32)
    pr = min(rows, PACK_ROWS)
    for rb in range(rows // pr):
        rs = slice(rb * pr, (rb + 1) * pr)
        for half in range(S5_CHUNK // gl):
            xs = _block_transpose([t_scr[half * gl + jj, rs, :] for jj in range(gl)], h)
            for g in range(gl):
                o_ref[g, rs, half * LANES:(half + 1) * LANES] = xs[g].astype(BF16)


def _s5_unpack_kernel(y_ref, o_ref, t_scr, *, h, rows):
    gl = LANES // h
    pr = min(rows, PACK_ROWS)
    for rb in range(rows // pr):
        rs = slice(rb * pr, (rb + 1) * pr)
        for half in range(S5_CHUNK // gl):
            ys = _block_transpose([y_ref[g, rs, half * LANES:(half + 1) * LANES].astype(F32) for g in range(gl)], h)
            for tt in range(gl):
                t_scr[half * gl + tt, rs, :] = ys[tt].astype(BF16)
    o_ref[...] = _swap_major(t_scr[...])


def _s5_pack(u3, h):
    R, t, w = u3.shape
    gl = LANES // h
    rows = min(R, 256)
    kern = functools.partial(_s5_pack_kernel, h=h, rows=rows)
    return pl.pallas_call(
        kern,
        grid=(w // LANES, R // rows),
        in_specs=[pl.BlockSpec((rows, t, LANES), lambda l, i: (i, 0, l))],
        out_specs=pl.BlockSpec((gl, rows, t * h), lambda l, i: (l, i, 0)),
        out_shape=jax.ShapeDtypeStruct((w // h, R, t * h), BF16),
        scratch_shapes=[pltpu.VMEM((t, rows, LANES), F32)],
        compiler_params=_cparams("parallel", "parallel"),
        name="s5_pack",
    )(u3)


def _s5_unpack(yg, h):
    G, R, th = yg.shape
    t = th // h
    gl = LANES // h
    rows = min(R, 256)
    kern = functools.partial(_s5_unpack_kernel, h=h, rows=rows)
    return pl.pallas_call(
        kern,
        grid=(G // gl, R // rows),
        in_specs=[pl.BlockSpec((gl, rows, th), lambda l, i: (l, i, 0))],
        out_specs=pl.BlockSpec((rows, t, LANES), lambda l, i: (i, 0, l)),
        out_shape=jax.ShapeDtypeStruct((R, t, G * h), BF16),
        scratch_shapes=[pltpu.VMEM((t, rows, LANES), BF16)],
        compiler_params=_cparams("parallel", "parallel"),
        name="s5_unpack",
    )(yg)


def _s5_mixer(u4, mats, layer, init, h):
    lc, nb, t, w = u4.shape
    yg, fin = _s5_scan(_s5_pack(u4.reshape(lc * nb, t, w), h), mats, layer, init, nb)
    return _s5_unpack(yg, h).reshape(lc, nb, t, w), fin


def _outproj_kernel(yh_ref, ys_ref, x_ref, mod_ref, g_ref, wg_ref, bg_ref, wo_ref,
                    xo_ref, hx_ref, *, d, dh, tm):
    gate = mod_ref[0, :, 2 * d:3 * d]
    shift = mod_ref[0, :, 3 * d:4 * d]
    scale = mod_ref[0, :, 4 * d:5 * d]
    ns = max(1, min(SUB_TILES, tm // OUT_SUB_ROWS))
    sub = tm // ns
    rows = [slice(s * sub, (s + 1) * sub) for s in range(ns)]
    ys = [ys_ref[s * sub // S5_CHUNK:(s + 1) * sub // S5_CHUNK, 0, :, :].reshape(sub, ys_ref.shape[-1])
          for s in range(ns)]
    pre = [_dot(y, wg_ref[0]) for y in ys]
    glu = [(y.astype(F32) * jax.nn.sigmoid(a + bg_ref[0])).astype(BF16) for y, a in zip(ys, pre)]
    yx = [_dot(yh_ref[0, rs, :], wo_ref[0, 0:dh, :]) + _dot(gl, wo_ref[0, dh:, :]) for rs, gl in zip(rows, glu)]
    for rs, v in zip(rows, yx):
        xo = x_ref[0, rs, :] + gate * _rms(v, g_ref[0, 1:2, :])
        xo_ref[0, rs, :] = xo
        hx_ref[0, rs, :] = (_rms(xo, g_ref[0, 2:3, :]) * (1.0 + scale) + shift).astype(BF16)


def _outproj(yh, ys4, x, mod, mod_row, layer, norm_g, w_glu, b_glu, w_out, tm):
    b, s, d = x.shape
    dh = yh.shape[-1]
    ds5 = ys4.shape[-1]
    tc = tm // S5_CHUNK
    kern = functools.partial(_outproj_kernel, d=d, dh=dh, tm=tm)
    lyr = lambda bi, i: (layer, 0, 0)
    t3 = lambda bi, i: (bi, i, 0)
    return pl.pallas_call(
        kern,
        grid=(b, s // tm),
        in_specs=[pl.BlockSpec((1, tm, dh), t3),
                  pl.BlockSpec((tc, 1, S5_CHUNK, ds5), lambda bi, i: (i, bi, 0, 0)),
                  pl.BlockSpec((1, tm, d), t3),
                  pl.BlockSpec((1, 1, mod.shape[-1]), lambda bi, i: (mod_row(bi), 0, 0)),
                  pl.BlockSpec((1,) + norm_g.shape[1:], lyr),
                  pl.BlockSpec((1, ds5, ds5), lyr), pl.BlockSpec((1, 1, ds5), lyr),
                  pl.BlockSpec((1, dh + ds5, d), lyr)],
        out_specs=[pl.BlockSpec((1, tm, d), t3), pl.BlockSpec((1, tm, d), t3)],
        out_shape=[jax.ShapeDtypeStruct((b, s, d), F32), jax.ShapeDtypeStruct((b, s, d), BF16)],
        compiler_params=_cparams("parallel", "parallel"),
        name="outproj",
    )(yh, ys4, x, mod, norm_g, w_glu, b_glu, w_out)


def _ffn_kernel(hm_ref, hp_ref, hn_ref, x_ref, mod_ref, g_ref, wg_ref, wv_ref, cw_ref, cb_ref, wd_ref,
                o_ref, acc_ref, *, tm, d, wg, vertical, kp):
    k = pl.program_id(3)
    i = pl.program_id(1) * kp + k
    nt = pl.num_programs(1) * kp
    j = pl.program_id(2)
    nj = pl.num_programs(2)
    hm = hm_ref[0]
    cw = cw_ref[0]
    if vertical:
        top = jnp.where(i > 0, 1.0, 0.0).astype(BF16)
        bot = jnp.where(i < nt - 1, 1.0, 0.0).astype(BF16)
        ha = jnp.concatenate([hp_ref[0] * top, hm, hn_ref[0] * bot], axis=0)
    else:
        ha = hm
    fc = wg_ref.shape[-1]
    c1 = -(-(fc // 2) // FFN_COL_ALIGN) * FFN_COL_ALIGN
    segs = [(0, c1), (c1, fc)] if 0 < c1 < fc else [(0, fc)]
    gs = [_dot(ha, wg_ref[0, :, a:b]) for a, b in segs]
    vs = [_dot(hm, wv_ref[0, :, a:b]) for a, b in segs]

    def vcol(g, w, dx):
        if not vertical:
            return g * w[3 + dx:4 + dx]
        return (g[0:tm] * w[dx:dx + 1] + g[wg:wg + tm] * w[3 + dx:4 + dx]
                + g[2 * wg:2 * wg + tm] * w[6 + dx:7 + dx])

    hmid = []
    for (a, b), g, v in zip(segs, gs, vs):
        w = cw[:, a:b]
        col = lax.broadcasted_iota(jnp.int32, (tm, b - a), 0) & (wg - 1)
        conv = (vcol(g, w, 1) + jnp.where(col > 0, pltpu.roll(vcol(g, w, 0), 1, axis=0), 0.0)
                + jnp.where(col < wg - 1, pltpu.roll(vcol(g, w, 2), tm - 1, axis=0), 0.0) + cb_ref[0, :, a:b])
        hmid.append((jax.nn.gelu(conv) * v).astype(BF16))
    part = _dot(hmid[0], wd_ref[0, segs[0][0]:segs[0][1], :])
    for (a, b), h in zip(segs[1:], hmid[1:]):
        part = part + _dot(h, wd_ref[0, a:b, :])

    @pl.when(j == 0)
    def _():
        acc_ref[k] = part

    @pl.when(j > 0)
    def _():
        acc_ref[k] += part

    @pl.when(j == nj - 1)
    def _():
        gate = mod_ref[0, :, 5 * d:6 * d]
        o_ref[0] = x_ref[0] + gate * _rms(acc_ref[k], g_ref[0, 3:4, :])


def _ffn(hx, x, mod, mod_row, layer, norm_g, w_up, conv_w, conv_b, w_down, tm, wg, vertical):
    b, s, d = x.shape
    f = w_down.shape[1]
    fc = f
    nf = f // fc
    once = dict(pipeline_mode=pl.Buffered(1))
    nt = s // tm
    r = tm // wg if vertical else 1
    hb = wg if vertical else 16
    nhb = s // hb
    kp = FFN_TILE_GROUP if nt % FFN_TILE_GROUP == 0 else 1
    kern = functools.partial(_ffn_kernel, tm=tm, d=d, wg=wg, vertical=vertical, kp=kp)
    tile = lambda ip, k: ip * kp + k
    cur = lambda bi, ip, j, k: (bi, tile(ip, k), 0)
    fin = lambda bi, ip, j, k: (bi, tile(ip, jnp.where(j == nf - 1, k, 0)), 0)
    return pl.pallas_call(
        kern,
        grid=(b, nt // kp, nf, kp),
        in_specs=[pl.BlockSpec((1, tm, d), cur),
                  pl.BlockSpec((1, hb, d), lambda bi, ip, j, k: (bi, jnp.maximum(tile(ip, k) * r - 1, 0), 0)),
                  pl.BlockSpec((1, hb, d), lambda bi, ip, j, k: (bi, jnp.minimum((tile(ip, k) + 1) * r, nhb - 1), 0)),
                  pl.BlockSpec((1, tm, d), fin),
                  pl.BlockSpec((1, 1, mod.shape[-1]), lambda bi, ip, j, k: (mod_row(bi), 0, 0)),
                  pl.BlockSpec((1,) + norm_g.shape[1:], lambda bi, ip, j, k: (layer, 0, 0)),
                  pl.BlockSpec((1, d, fc), lambda bi, ip, j, k: (layer, 0, j), **once),
                  pl.BlockSpec((1, d, fc), lambda bi, ip, j, k: (layer, 0, nf + j), **once),
                  pl.BlockSpec((1, 9, fc), lambda bi, ip, j, k: (layer, 0, j), **once),
                  pl.BlockSpec((1, 1, fc), lambda bi, ip, j, k: (layer, 0, j), **once),
                  pl.BlockSpec((1, fc, d), lambda bi, ip, j, k: (layer, j, 0), **once)],
        out_specs=pl.BlockSpec((1, tm, d), fin),
        out_shape=jax.ShapeDtypeStruct((b, s, d), F32),
        scratch_shapes=[pltpu.VMEM((kp, tm, d), F32)],
        compiler_params=_cparams("parallel", "parallel", "arbitrary", "arbitrary"),
        name="conv_glu_ffn",
    )(hx, hx, hx, x, mod, norm_g, w_up, w_up, conv_w, conv_b, w_down)


def kernel(x, c, ctx, c_ctx, w_ada, b_ada, norm_g, w_in, hy_short_w, hy_short_b,
           filt_w_in, filt_b_in, filt_w_hid, filt_b_hid, filt_freq, filt_w_out, hy_bias,
           s5_lam_re, s5_lam_im, s5_log_step, s5_b_re, s5_b_im, s5_c_re, s5_c_im, s5_d,
           s5_w_glu, s5_b_glu, w_out, ffn_w_up, ffn_conv_w, ffn_conv_b, ffn_w_down):
    depth = w_ada.shape[0]
    bsz, seq, d = x.shape
    lctx = ctx.shape[1]
    dh = hy_bias.shape[-1]
    G, P, H = s5_b_re.shape[2], s5_b_re.shape[3], s5_b_re.shape[4]
    dff = ffn_w_down.shape[1]
    assert bsz % 2 == 0 and bsz <= 4 and seq % GRID_W == 0 and GRID_W & (GRID_W - 1) == 0

    cond = jnp.zeros((8, d), F32).at[:bsz].set(c).at[bsz].set(c_ctx)
    mod = _ada_mod(cond, w_ada, b_ada).reshape(depth * 8, 1, 6 * d)

    filt_args = (filt_w_in, filt_b_in, filt_w_hid, filt_b_hid, filt_freq, filt_w_out)
    long_conv = _LongConv(seq, dh, _hyena_filters(seq, *filt_args, dh))
    k_ctx = _hyena_filters(lctx, *filt_args, dh)
    short_c = tuple(jnp.asarray(a, BF16) for a in _short_consts(lctx))

    w_in_b = w_in.astype(BF16)
    w_glu_b = s5_w_glu.astype(BF16)
    w_out_b = w_out.astype(BF16)
    w_up_b = ffn_w_up.astype(BF16)
    w_down_b = ffn_w_down.astype(BF16)
    sb = hy_short_b.reshape(depth, 1, 3 * dh)
    cw = ffn_conv_w.reshape(depth, 9, dff)
    cb = ffn_conv_b.reshape(depth, 1, dff)
    bg = s5_b_glu.reshape(depth, 1, -1)
    hb = hy_bias.reshape(depth, 1, dh)
    mats = _s5_matrices(s5_lam_re, s5_lam_im, s5_log_step, s5_b_re, s5_b_im, s5_c_re, s5_c_im, s5_d)

    tm = min(seq, ROW_TILE)
    tp = min(seq, PROJ_ROW_TILE)
    for l in range(depth):
        last = l == depth - 1
        row_x = lambda bi, l=l: 8 * l + bi
        row_c = lambda bi, l=l: 8 * l + bsz

        x0c, zc, uc = _inproj(ctx, mod, row_c, l, norm_g, w_in_b, hy_short_w, sb, dh, lctx)
        ysc, ctx_state = _s5_mixer(uc, mats, l, jnp.zeros((G, 2 * bsz, 4 * P), F32), H)

        x0, z, u = _inproj(x, mod, row_x, l, norm_g, w_in_b, hy_short_w, sb, dh, tp)
        ys, _ = _s5_mixer(u, mats, l, ctx_state, H)
        yh = long_conv(z, x0, hb, l)
        x, hx = _outproj(yh, ys, x, mod, row_x, l, norm_g, w_glu_b, bg, w_out_b, tp)
        x = _ffn(hx, x, mod, row_x, l, norm_g, w_up_b, cw, cb, w_down_b, tm, GRID_W, True)

        if not last:
            yhc = _short_conv(zc, x0c, k_ctx, l, short_c, hb)
            ctx, hc = _outproj(yhc, ysc, ctx, mod, row_c, l, norm_g, w_glu_b, bg, w_out_b, lctx)
            ctx = _ffn(hc, ctx, mod, row_c, l, norm_g, w_up_b, cw, cb, w_down_b, lctx, lctx, False)
    return x
```
